```python
import jax, jax.numpy as jnp
from jax import lax
import numpy as np

D_MODEL = 1024
BATCH = 16
SEQ = 2048
DEPTH = 2

GRID_W = 64
CTX_LEN = 256
HEAD_DIM = 64
N_HEADS_NA = 8
D_NA = N_HEADS_NA * HEAD_DIM
N_HEADS_RW = 8
D_RW = N_HEADS_RW * HEAD_DIM
D_MIX = D_NA + D_RW
WIN_ROWS = 8
WIN_COLS = 16
N_DIR = 2
DECAY_LORA = 32
ICLR_LORA = 32
GATE_LORA = 96
D_RW_IN = 3 * D_RW + N_DIR * (DECAY_LORA + ICLR_LORA) + GATE_LORA
RW_SPLITS = (D_RW, 2 * D_RW, 3 * D_RW, 3 * D_RW + N_DIR * DECAY_LORA, 3 * D_RW + N_DIR * (DECAY_LORA + ICLR_LORA))
D_IN = 3 * D_NA + D_RW_IN
N_EXPERTS = 32
N_GROUPS = 8
EXPERTS_PER_GROUP = N_EXPERTS // N_GROUPS
TOP_K = 2
D_EXPERT = 512
MOE_BLOCK = 128
ROPE_BASE = 10000.0
LN_EPS = 1e-6
GN_EPS = 64e-5
ALPHA = (2 * DEPTH) ** 0.25
BETA = (8 * DEPTH) ** -0.25

kernel_name = 'hybrid_na_rwkv7_moe_diffusion_trunk'


def layer_norm(z, g, b):
    zf = z.astype(jnp.float32)
    mu = jnp.mean(zf, -1, keepdims=True)
    var = jnp.mean(jnp.square(zf - mu), -1, keepdims=True)
    return ((zf - mu) * lax.rsqrt(var + LN_EPS) * g + b).astype(z.dtype)


def modulate(z, shift, scale):
    return z * (1 + scale) + shift


def axial_rope_tables(seq_len):
    t = jnp.arange(seq_len)
    row = (t // GRID_W).astype(jnp.float32)
    col = (t % GRID_W).astype(jnp.float32)
    n_freq = HEAD_DIM // 4
    inv = ROPE_BASE ** (-jnp.arange(n_freq, dtype=jnp.float32) / n_freq)
    ar = row[:, None] * inv
    ac = col[:, None] * inv
    ang = jnp.concatenate([ar, ar, ac, ac], -1)
    return jnp.cos(ang)[:, None, :], jnp.sin(ang)[:, None, :]


def apply_axial_rope(z, cos, sin):
    a, b, cc, d = jnp.split(z, 4, axis=-1)
    rot = jnp.concatenate([-b, a, -d, cc], -1)
    return z * cos + rot * sin


def token_shift(z, mu_prev, mu_next):
    prev = jnp.pad(z[:, :-1], ((0, 0), (1, 0), (0, 0)))
    nxt = jnp.pad(z[:, 1:], ((0, 0), (0, 1), (0, 0)))
    return z + mu_prev * (prev - z) + mu_next * (nxt - z)


def wkv7_scan(s0, r, decay, k, v, kk, a, reverse):
    def step(s, inp):
        r_t, w_t, k_t, v_t, kk_t, a_t = inp
        s_kk = jnp.einsum('bhvk,bhk->bhv', s, kk_t)
        s = s * w_t[:, :, None, :] - s_kk[..., None] * (kk_t * a_t)[:, :, None, :] + v_t[..., None] * k_t[:, :, None, :]
        return s, jnp.einsum('bhvk,bhk->bhv', s, r_t)
    xs = tuple(jnp.moveaxis(z, 1, 0) for z in (r, decay, k, v, kk, a))
    s_fin, ys = lax.scan(step, s0, xs, reverse=reverse)
    return jnp.moveaxis(ys, 0, 1), s_fin


def rwkv7_bidir(p, rope, init_states, mu_prev, mu_next, w0, w2, a0, a2, g2, k_k, k_a, r_k, gn_g, gn_b):
    B, T = p.shape[0], p.shape[1]
    H, N = N_HEADS_RW, HEAD_DIM
    z = token_shift(p.astype(jnp.float32), mu_prev, mu_next)
    r, k, v, lw, la, lg = jnp.split(z, RW_SPLITS, axis=-1)
    r = r.reshape(B, T, H, N)
    k = k.reshape(B, T, H, N)
    v = v.reshape(B, T, H, N)
    if rope is not None:
        r = apply_axial_rope(r, rope[0], rope[1])
        k = apply_axial_rope(k, rope[0], rope[1])
    kk = k * k_k.reshape(H, N)
    kk = kk / jnp.maximum(jnp.sqrt(jnp.sum(kk * kk, -1, keepdims=True)), 1e-12)
    lw = lw.reshape(B, T, N_DIR, DECAY_LORA)
    la = la.reshape(B, T, N_DIR, ICLR_LORA)
    w_log = -jax.nn.softplus(-(w0 + jnp.einsum('btdr,drc->btdc', jnp.tanh(lw), w2))) - 0.5
    decay = jnp.exp(-jnp.exp(w_log)).reshape(B, T, N_DIR, H, N)
    a = jax.nn.sigmoid(a0 + jnp.einsum('btdr,drc->btdc', la, a2)).reshape(B, T, N_DIR, H, N)
    k_dir = k[:, :, None] * (1 + (a - 1) * k_a.reshape(H, N))
    g = jax.nn.sigmoid(lg) @ g2
    outs = []
    states = []
    for d in range(N_DIR):
        s0 = jnp.zeros((B, H, N, N), jnp.float32) if init_states is None else init_states[d]
        y_d, s_d = wkv7_scan(s0, r, decay[:, :, d], k_dir[:, :, d], v, kk, a[:, :, d], reverse=(d == 1))
        outs.append(y_d)
        states.append(s_d)
    y = outs[0] + outs[1]
    mu = jnp.mean(y, -1, keepdims=True)
    var = jnp.mean(jnp.square(y - mu), -1, keepdims=True)
    y = (y - mu) * lax.rsqrt(var + GN_EPS) * gn_g.reshape(H, N) + gn_b.reshape(H, N)
    bonus = jnp.sum(r[:, :, None] * k_dir * r_k, -1, keepdims=True) * v[:, :, None]
    y = y + jnp.sum(bonus, axis=2)
    out = y.reshape(B, T, D_RW) * g
    return out.astype(p.dtype), (states[0], states[1])


def neighbourhood_attention(q, k, v, k_ctx, v_ctx, rpb):
    B, S = q.shape[0], q.shape[1]
    rows = S // GRID_W
    kr = min(WIN_ROWS, rows)

    def grid(z):
        return z.reshape(B, rows, GRID_W, N_HEADS_NA, HEAD_DIM).transpose(1, 0, 3, 2, 4)

    qg, kg, vg = grid(q), grid(k), grid(v)
    cols = jnp.arange(GRID_W)
    col_start = jnp.clip(cols - WIN_COLS // 2, 0, GRID_W - WIN_COLS)
    col_idx = col_start[:, None] + jnp.arange(WIN_COLS)[None, :]
    col_off = col_idx - cols[:, None] + (WIN_COLS - 1)
    rpb_c = rpb[:, :, col_off]

    def one_row(i):
        rs = jnp.clip(i - kr // 2, 0, rows - kr)
        q_i = qg[i]
        k_win = lax.dynamic_slice_in_dim(kg, rs, kr, axis=0)[:, :, :, col_idx]
        v_win = lax.dynamic_slice_in_dim(vg, rs, kr, axis=0)[:, :, :, col_idx]
        row_off = rs + jnp.arange(kr) - i + (WIN_ROWS - 1)
        bias = rpb_c[:, row_off].transpose(0, 2, 1, 3)
        s_loc = jnp.einsum('bhqd,rbhqcd->bhqrc', q_i, k_win).astype(jnp.float32) + bias.astype(jnp.float32)
        s_ctx = jnp.einsum('bhqd,bnhd->bhqn', q_i, k_ctx).astype(jnp.float32)
        s = jnp.concatenate([s_loc.reshape(B, N_HEADS_NA, GRID_W, kr * WIN_COLS), s_ctx], -1)
        pr = jax.nn.softmax(s, axis=-1).astype(v.dtype)
        p_loc = pr[..., :kr * WIN_COLS].reshape(B, N_HEADS_NA, GRID_W, kr, WIN_COLS)
        p_ctx = pr[..., kr * WIN_COLS:]
        return jnp.einsum('bhqrc,rbhqcd->bhqd', p_loc, v_win) + jnp.einsum('bhqn,bnhd->bhqd', p_ctx, v_ctx)

    out = lax.map(one_row, jnp.arange(rows))
    return out.transpose(1, 0, 3, 2, 4).reshape(B, S, D_NA)


def context_attention(q, k, v):
    B, C = q.shape[0], q.shape[1]
    s = jnp.einsum('bqhd,bkhd->bhqk', q, k).astype(jnp.float32)
    pr = jax.nn.softmax(s, axis=-1).astype(v.dtype)
    return jnp.einsum('bhqk,bkhd->bqhd', pr, v).reshape(B, C, D_NA)


def moe_grouped_top2(h, router_w, router_bias, w1, w3, w2):
    N, D = h.shape
    scores = jax.nn.sigmoid(jnp.dot(h, router_w).astype(jnp.float32))
    sel = (scores + router_bias.astype(jnp.float32)).reshape(N, N_GROUPS, EXPERTS_PER_GROUP)
    group_score = jnp.sum(lax.top_k(sel, TOP_K)[0], -1)
    g_idx = jnp.argmax(group_score, -1)
    sel_in_group = jnp.take_along_axis(sel, g_idx[:, None, None], axis=1)[:, 0]
    _, local = lax.top_k(sel_in_group, TOP_K)
    e_idx = (g_idx[:, None] * EXPERTS_PER_GROUP + local).astype(jnp.int32)
    gate = jnp.take_along_axis(scores, e_idx, -1)
    gate = gate / jnp.sum(gate, -1, keepdims=True)
    flat_e = e_idx.reshape(-1)
    nk = flat_e.shape[0]
    order = jnp.argsort(flat_e)
    sorted_e = flat_e[order]
    counts = jnp.bincount(flat_e, length=N_EXPERTS)
    padded = (counts + MOE_BLOCK - 1) // MOE_BLOCK * MOE_BLOCK
    pad_end = jnp.cumsum(padded)
    pad_start = pad_end - padded
    seg_start = jnp.cumsum(counts) - counts
    dest_sorted = (pad_start[sorted_e] + jnp.arange(nk) - seg_start[sorted_e]).astype(jnp.int32)
    dest = jnp.zeros((nk,), jnp.int32).at[order].set(dest_sorted)
    n_blocks = -(-nk // MOE_BLOCK) + N_EXPERTS
    buf_len = n_blocks * MOE_BLOCK
    src_tok = jnp.full((buf_len,), N, jnp.int32).at[dest].set(jnp.arange(nk, dtype=jnp.int32) // TOP_K)
    h_pad = jnp.concatenate([h, jnp.zeros((1, D), h.dtype)], 0)
    xb = h_pad[src_tok].reshape(n_blocks, MOE_BLOCK, D)
    block_expert = jnp.minimum(jnp.searchsorted(pad_end, jnp.arange(n_blocks) * MOE_BLOCK, side='right'), N_EXPERTS - 1)

    def expert_block(args):
        xblk, e = args
        return (jax.nn.silu(xblk @ w1[e]) * (xblk @ w3[e])) @ w2[e]

    yb = lax.map(expert_block, (xb, block_expert)).reshape(buf_len, D)
    y_assign = yb[dest].reshape(N, TOP_K, D)
    return jnp.einsum('nk,nkd->nd', gate.astype(h.dtype), y_assign)


def setup_inputs(seed: int = 0) -> dict:
    key = jax.random.key(seed)
    ks = jax.random.split(key, 32)
    f32 = jnp.float32
    L = DEPTH

    def nrm(k, shape, s):
        return jax.random.normal(k, shape, f32) * s

    return {
        'x': nrm(ks[0], (BATCH, SEQ, D_MODEL), 1.0),
        'c': nrm(ks[1], (BATCH, D_MODEL), 1.0),
        'ctx': nrm(ks[2], (BATCH, CTX_LEN, D_MODEL), 1.0),
        'c_ctx': nrm(ks[3], (D_MODEL,), 1.0),
        'ada_w': nrm(ks[4], (L, D_MODEL, 6 * D_MODEL), 0.5 * D_MODEL ** -0.5),
        'ada_b': nrm(ks[5], (L, 6 * D_MODEL), 0.02),
        'w_in': nrm(ks[6], (L, D_MODEL, D_IN), D_MODEL ** -0.5),
        'na_rpb': nrm(ks[7], (L, N_HEADS_NA, 2 * WIN_ROWS - 1, 2 * WIN_COLS - 1), 0.1),
        'rw_mu_prev': jax.random.uniform(ks[8], (L, D_RW_IN), f32, 0.0, 0.5),
        'rw_mu_next': jax.random.uniform(ks[9], (L, D_RW_IN), f32, 0.0, 0.5),
        'rw_w0': jax.random.uniform(ks[10], (L, N_DIR, D_RW), f32, -5.0, 0.0),
        'rw_w2': nrm(ks[11], (L, N_DIR, DECAY_LORA, D_RW), 0.1 * DECAY_LORA ** -0.5),
        'rw_a0': nrm(ks[12], (L, N_DIR, D_RW), 0.1),
        'rw_a2': nrm(ks[13], (L, N_DIR, ICLR_LORA, D_RW), 0.5 * ICLR_LORA ** -0.5),
        'rw_g2': nrm(ks[14], (L, GATE_LORA, D_RW), GATE_LORA ** -0.5),
        'rw_k_k': 0.85 + nrm(ks[15], (L, D_RW), 0.05),
        'rw_k_a': 1.0 + nrm(ks[16], (L, D_RW), 0.05),
        'rw_r_k': nrm(ks[17], (L, N_HEADS_RW, HEAD_DIM), 0.1),
        'rw_gn_g': 1.0 + nrm(ks[18], (L, D_RW), 0.05),
        'rw_gn_b': nrm(ks[19], (L, D_RW), 0.02),
        'w_out': nrm(ks[20], (L, D_MIX, D_MODEL), BETA * D_MIX ** -0.5),
        'ln1_g': 1.0 + nrm(ks[21], (L, D_MODEL), 0.05),
        'ln1_b': nrm(ks[22], (L, D_MODEL), 0.02),
        'ln2_g': 1.0 + nrm(ks[23], (L, D_MODEL), 0.05),
        'ln2_b': nrm(ks[24], (L, D_MODEL), 0.02),
        'router_w': nrm(ks[25], (D_MODEL, N_EXPERTS), D_MODEL ** -0.5),
        'router_bias': nrm(ks[26], (N_EXPERTS,), 0.01),
        'exp_w1': nrm(ks[27], (L, N_EXPERTS, D_MODEL, D_EXPERT), D_MODEL ** -0.5),
        'exp_w3': nrm(ks[28], (L, N_EXPERTS, D_MODEL, D_EXPERT), D_MODEL ** -0.5),
        'exp_w2': nrm(ks[29], (L, N_EXPERTS, D_EXPERT, D_MODEL), BETA * D_EXPERT ** -0.5),
    }


def reference(x, c, ctx, c_ctx, ada_w, ada_b, w_in, na_rpb, rw_mu_prev, rw_mu_next, rw_w0, rw_w2, rw_a0, rw_a2, rw_g2, rw_k_k, rw_k_a, rw_r_k, rw_gn_g, rw_gn_b, w_out, ln1_g, ln1_b, ln2_g, ln2_b, router_w, router_bias, exp_w1, exp_w3, exp_w2):
    B, S, D = x.shape
    C = ctx.shape[1]
    rope = axial_rope_tables(S)
    q_scale = HEAD_DIM ** -0.5
    for l in range(DEPTH):
        last = l == DEPTH - 1
        mod_x = jnp.split((jax.nn.silu(c) @ ada_w[l] + ada_b[l])[:, None, :], 6, axis=-1)
        mod_c = jnp.split((jax.nn.silu(c_ctx) @ ada_w[l] + ada_b[l])[None, None, :], 6, axis=-1)
        px = modulate(x, mod_x[0], mod_x[1]) @ w_in[l]
        pc = modulate(ctx, mod_c[0], mod_c[1]) @ w_in[l]
        qx, kx, vx = (px[..., i * D_NA:(i + 1) * D_NA].reshape(B, S, N_HEADS_NA, HEAD_DIM) for i in range(3))
        qc, kc, vc = (pc[..., i * D_NA:(i + 1) * D_NA].reshape(B, C, N_HEADS_NA, HEAD_DIM) for i in range(3))
        rw_args = dict(mu_prev=rw_mu_prev[l], mu_next=rw_mu_next[l], w0=rw_w0[l], w2=rw_w2[l], a0=rw_a0[l], a2=rw_a2[l], g2=rw_g2[l], k_k=rw_k_k[l], k_a=rw_k_a[l], r_k=rw_r_k[l], gn_g=rw_gn_g[l], gn_b=rw_gn_b[l])
        rw_c, ctx_states = rwkv7_bidir(pc[..., 3 * D_NA:], None, None, **rw_args)
        rw_x, _ = rwkv7_bidir(px[..., 3 * D_NA:], rope, ctx_states, **rw_args)
        na_x = neighbourhood_attention(qx * q_scale, kx, vx, kc, vc, na_rpb[l])
        o_x = jnp.concatenate([na_x, rw_x], -1) @ w_out[l]
        x = layer_norm(ALPHA * x + mod_x[2] * o_x, ln1_g[l], ln1_b[l])
        hx = modulate(x, mod_x[3], mod_x[4])
        if last:
            y_x = moe_grouped_top2(hx.reshape(B * S, D), router_w, router_bias, exp_w1[l], exp_w3[l], exp_w2[l]).reshape(B, S, D)
        else:
            na_c = context_attention(qc * q_scale, kc, vc)
            o_c = jnp.concatenate([na_c, rw_c], -1) @ w_out[l]
            ctx = layer_norm(ALPHA * ctx + mod_c[2] * o_c, ln1_g[l], ln1_b[l])
            hc = modulate(ctx, mod_c[3], mod_c[4])
            y = moe_grouped_top2(jnp.concatenate([hx.reshape(B * S, D), hc.reshape(B * C, D)], 0), router_w, router_bias, exp_w1[l], exp_w3[l], exp_w2[l])
            y_x = y[:B * S].reshape(B, S, D)
            y_c = y[B * S:].reshape(B, C, D)
            ctx = layer_norm(ALPHA * ctx + mod_c[5] * y_c, ln2_g[l], ln2_b[l])
        x = layer_norm(ALPHA * x + mod_x[5] * y_x, ln2_g[l], ln2_b[l])
    return x
```

```python
import functools
import math

import jax
import jax.numpy as jnp
import numpy as np
from jax import lax
from jax.experimental import pallas as pl
from jax.experimental.pallas import tpu as pltpu

F32 = jnp.float32
BF16 = jnp.bfloat16

HEAD_DIM = 64
N_HEADS = 8
D_HEADS = N_HEADS * HEAD_DIM
N_PAIRS = N_HEADS // 2
GRID_W = 64
WIN_ROWS = 8
WIN_COLS = 16
LORA = 32
GATE_LORA = 96
N_EXPERTS = 32
N_GROUPS = 8
EXPERTS_PER_GROUP = 4
N_CLASSES = N_GROUPS * 6
ROPE_BASE = 10000.0
LN_EPS = 1e-6
GN_EPS = 64e-5
CHUNK = 64
ROW_TILE = 256
MOE_TILE = 256
D_QKV = 3 * D_HEADS
D_RWP = 3 * D_HEADS + 256
D_INP = D_QKV + D_RWP
XS_W = 1024 + 128
NEG = -1e30
VMEM_LIMIT = 56 * 1024 * 1024


def _cparams(sem):
    return pltpu.CompilerParams(dimension_semantics=sem, vmem_limit_bytes=VMEM_LIMIT)


def _dot(a, b):
    return jnp.dot(a, b, preferred_element_type=F32)


def _dot_nt(a, b):
    return lax.dot_general(a, b, (((1,), (1,)), ((), ())), preferred_element_type=F32)


def _split2(a):
    hi = a.astype(BF16)
    lo = (a - hi.astype(F32)).astype(BF16)
    return hi, lo


def _split3(a):
    hi = a.astype(BF16)
    r1 = a - hi.astype(F32)
    mid = r1.astype(BF16)
    lo = (r1 - mid.astype(F32)).astype(BF16)
    return hi, mid, lo


def _dot3(a, b):
    ah, al = _split2(a)
    bh, bl = _split2(b)
    return _dot(ah, bh) + _dot(al, bh) + _dot(ah, bl)


def _dot_exact_rhs(a, b_exact):
    h, m, l = _split3(a)
    return _dot(h, b_exact) + _dot(m, b_exact) + _dot(l, b_exact)


def _dot_exact_lhs(a_exact, b):
    h, m, l = _split3(b)
    return _dot(a_exact, h) + _dot(a_exact, m) + _dot(a_exact, l)


def _ada_kernel(cc_ref, w_ref, b_ref, o_ref):
    cc = cc_ref[...]
    s = cc * jax.nn.sigmoid(cc)
    o_ref[0] = _dot3(s, w_ref[0]) + b_ref[0]


def _ada(cc, ada_w, ada_b):
    L, D, D6 = ada_w.shape
    R = cc.shape[0]
    tn = 1536
    return pl.pallas_call(
        _ada_kernel,
        grid=(L, D6 // tn),
        in_specs=[
            pl.BlockSpec((R, D), lambda l, n: (0, 0)),
            pl.BlockSpec((1, D, tn), lambda l, n: (l, 0, n)),
            pl.BlockSpec((1, 1, tn), lambda l, n: (l, 0, n)),
        ],
        out_specs=pl.BlockSpec((1, R, tn), lambda l, n: (l, 0, n)),
        out_shape=jax.ShapeDtypeStruct((L, R, D6), F32),
        compiler_params=_cparams(("arbitrary", "arbitrary")),
    )(cc, ada_w, ada_b.reshape(L, 1, D6))


def _inproj_kernel(z_ref, mod_ref, w_ref, qkv_ref, rw_ref, *, d):
    z = z_ref[0]
    mod = mod_ref[0, 0]
    shift = mod[:, 0:d]
    scale = mod[:, d:2 * d]
    h = (z * (1.0 + scale) + shift).astype(BF16)
    q = _dot(h, w_ref[:, 0:D_HEADS])
    qkv_ref[0, :, 0:D_HEADS] = (q * (HEAD_DIM ** -0.5)).astype(BF16)
    kv = _dot(h, w_ref[:, D_HEADS:D_QKV])
    qkv_ref[0, :, D_HEADS:D_QKV] = kv.astype(BF16)
    rw_ref[0] = _dot(h, w_ref[:, D_QKV:D_INP])


def _inproj(z, modsel, w_in_p, n_ctx_tiles):
    B, T, D = z.shape
    nt = T // ROW_TILE
    return pl.pallas_call(
        functools.partial(_inproj_kernel, d=D),
        grid=(B, nt),
        in_specs=[
            pl.BlockSpec((1, ROW_TILE, D), lambda b, t: (b, t, 0)),
            pl.BlockSpec((1, 1, 1, 6 * D), lambda b, t: (b, jnp.where(t < n_ctx_tiles, 0, 1), 0, 0)),
            pl.BlockSpec((D, D_INP), lambda b, t: (0, 0)),
        ],
        out_specs=[
            pl.BlockSpec((1, ROW_TILE, D_QKV), lambda b, t: (b, t, 0)),
            pl.BlockSpec((1, ROW_TILE, D_RWP), lambda b, t: (b, t, 0)),
        ],
        out_shape=[
            jax.ShapeDtypeStruct((B, T, D_QKV), BF16),
            jax.ShapeDtypeStruct((B, T, D_RWP), F32),
        ],
        compiler_params=_cparams(("arbitrary", "arbitrary")),
    )(z, modsel, w_in_p)


def _rwprep_kernel(p_ref, pp_ref, pn_ref, mup_ref, mun_ref, cos_ref, sa_ref, sb_ref,
                   kk_ref, ka_ref, rk_ref, w0_ref, a0_ref, w2_ref, a2_ref, g2_ref, ones_ref,
                   at_f, bt_f, kt_f, rt_f, bh_f, kh_f, pe_f,
                   at_b, bt_b, kt_b, rt_b, bh_b, kh_b, pe_b,
                   vt_ref, bonus_ref, g_ref, *, n_ctx_tiles, n_tiles):
    t = pl.program_id(1)
    P = p_ref[0]
    R = P.shape[0]
    prev_ok = jnp.logical_and(t != 0, t != n_ctx_tiles)
    next_ok = jnp.logical_and(t != n_ctx_tiles - 1, t != n_tiles - 1)
    prev_row = jnp.where(prev_ok, pp_ref[0, 7:8, :], 0.0)
    next_row = jnp.where(next_ok, pn_ref[0, 0:1, :], 0.0)
    row = lax.broadcasted_iota(jnp.int32, (R, 1), 0)
    prev = jnp.where(row == 0, prev_row, pltpu.roll(P, 1, axis=0))
    nxt = jnp.where(row == R - 1, next_row, pltpu.roll(P, R - 1, axis=0))
    z = P + mup_ref[...] * (prev - P) + mun_ref[...] * (nxt - P)

    cos = cos_ref[...]
    sa = sa_ref[...]
    sb = sb_ref[...]

    def rope(u):
        return (u * cos + pltpu.roll(u, D_HEADS - HEAD_DIM // 4, axis=1) * sa
                + pltpu.roll(u, HEAD_DIM // 4, axis=1) * sb)

    r = rope(z[:, 0:D_HEADS])
    k = rope(z[:, D_HEADS:2 * D_HEADS])
    v = z[:, 2 * D_HEADS:3 * D_HEADS]
    ones_bd = ones_ref[...]

    def seg_sum(u):
        return _dot_exact_rhs(u, ones_bd)

    kk = k * kk_ref[...]
    kk = kk / jnp.maximum(jnp.sqrt(seg_sum(kk * kk)), 1e-12)

    slab = z[:, 3 * D_HEADS:3 * D_HEADS + 128]
    u_w = w0_ref[...] + _dot3(jnp.tanh(slab), w2_ref[...])
    u_a = a0_ref[...] + _dot3(slab, a2_ref[...])
    g_ref[0] = _dot3(jax.nn.sigmoid(z[:, 3 * D_HEADS + 128:D_RWP]), g2_ref[...])
    e_all = math.exp(-0.5) * jax.nn.sigmoid(u_w)
    a_all = jax.nn.sigmoid(u_a)

    ci = lax.broadcasted_iota(jnp.int32, (CHUNK, CHUNK), 0)
    cj = lax.broadcasted_iota(jnp.int32, (CHUNK, CHUNK), 1)
    ka = ka_ref[...]
    outs = ((at_f, bt_f, kt_f, rt_f, bh_f, kh_f, pe_f), (at_b, bt_b, kt_b, rt_b, bh_b, kh_b, pe_b))
    kd_sum = None
    for d in range(2):
        e = e_all[:, d * D_HEADS:(d + 1) * D_HEADS]
        a = a_all[:, d * D_HEADS:(d + 1) * D_HEADS]
        tri = (cj <= ci) if d == 0 else (cj >= ci)
        tri = tri.astype(BF16)
        cs, ce = [], []
        for q in range(R // CHUNK):
            cq = -_dot_exact_lhs(tri, e[q * CHUNK:(q + 1) * CHUNK])
            end = cq[CHUNK - 1:CHUNK] if d == 0 else cq[0:1]
            cs.append(cq)
            ce.append(jnp.broadcast_to(end, cq.shape))
        c = jnp.concatenate(cs, axis=0)
        cend = jnp.concatenate(ce, axis=0)
        kd = k * (1.0 + (a - 1.0) * ka)
        kd_sum = kd if kd_sum is None else kd_sum + kd
        beta = a * kk
        en = jnp.exp(-c)
        eh = jnp.exp(cend - c)
        o_at, o_bt, o_kt, o_rt, o_bh, o_kh, o_pe = outs[d]
        o_at[0] = (-kk * jnp.exp(c + e)).astype(BF16)
        o_bt[0] = (beta * en).astype(BF16)
        o_kt[0] = (kd * en).astype(BF16)
        o_rt[0] = (r * jnp.exp(c)).astype(BF16)
        o_bh[0] = (beta * eh).astype(BF16)
        o_kh[0] = (kd * eh).astype(BF16)
        for q in range(R // CHUNK):
            o_pe[0, q] = jnp.exp(cend[q * CHUNK:q * CHUNK + 8])
    bonus_ref[0] = seg_sum(r * rk_ref[...] * kd_sum) * v
    vt_ref[0] = v.T.astype(BF16)


def _rwprep(p_rw, consts, n_ctx_tiles):
    B, T, _ = p_rw.shape
    nt = T // ROW_TILE
    nh = ROW_TILE // 8
    row = lambda w: pl.BlockSpec((1, w), lambda b, t: (0, 0))
    full = lambda a: pl.BlockSpec(a.shape, lambda b, t: (0, 0))
    tm = pl.BlockSpec((1, ROW_TILE, D_HEADS), lambda b, t: (b, t, 0))
    tab = pl.BlockSpec((ROW_TILE, D_HEADS), lambda b, t: (t, 0))
    pe = pl.BlockSpec((1, ROW_TILE // CHUNK, 8, D_HEADS), lambda b, t: (b, t, 0, 0))
    tm_shape = jax.ShapeDtypeStruct((B, T, D_HEADS), BF16)
    pe_shape = jax.ShapeDtypeStruct((B, T // CHUNK, 8, D_HEADS), F32)
    dir_specs = [tm] * 6 + [pe]
    dir_shapes = [tm_shape] * 6 + [pe_shape]
    return pl.pallas_call(
        functools.partial(_rwprep_kernel, n_ctx_tiles=n_ctx_tiles, n_tiles=nt),
        grid=(B, nt),
        in_specs=[
            pl.BlockSpec((1, ROW_TILE, D_RWP), lambda b, t: (b, t, 0)),
            pl.BlockSpec((1, 8, D_RWP), lambda b, t: (b, jnp.maximum(t * nh - 1, 0), 0)),
            pl.BlockSpec((1, 8, D_RWP), lambda b, t: (b, jnp.minimum((t + 1) * nh, T // 8 - 1), 0)),
            row(D_RWP), row(D_RWP), tab, tab, tab,
            row(D_HEADS), row(D_HEADS), row(D_HEADS), row(2 * D_HEADS), row(2 * D_HEADS),
            full(consts["w2p"]), full(consts["a2p"]), full(consts["g2p"]), full(consts["ones_bd"]),
        ],
        out_specs=dir_specs + dir_specs + [
            pl.BlockSpec((1, D_HEADS, ROW_TILE), lambda b, t: (b, 0, t)),
            tm, tm,
        ],
        out_shape=dir_shapes + dir_shapes + [
            jax.ShapeDtypeStruct((B, D_HEADS, T), BF16),
            jax.ShapeDtypeStruct((B, T, D_HEADS), F32),
            jax.ShapeDtypeStruct((B, T, D_HEADS), F32),
        ],
        compiler_params=_cparams(("arbitrary", "arbitrary")),
    )(p_rw, p_rw, p_rw, consts["mu_prev"], consts["mu_next"], consts["cos"], consts["sa"], consts["sb"],
      consts["k_k"], consts["k_a"], consts["r_k"], consts["w0"], consts["a0"],
      consts["w2p"], consts["a2p"], consts["g2p"], consts["ones_bd"])


def _bd(y, m0):
    zero = jnp.zeros_like(y)
    return jnp.concatenate([jnp.where(m0, y, zero), jnp.where(m0, zero, y)], axis=0)


def _sel(w, m0):
    return jnp.where(m0, w[0:CHUNK], w[CHUNK:2 * CHUNK])


def _wkv_chunk(St, A, R, Bt, Kt, Bh, Kh, vtm, pend, m0, strict, incl, eye):
    rhs1 = jnp.concatenate([_bd(A, m0), _bd(R, m0)], axis=0)
    G = _dot_nt(jnp.concatenate([Bt, Kt], axis=0), rhs1)
    zero = jnp.zeros((CHUNK, 2 * CHUNK), F32)
    A_ab = jnp.where(strict, G[0:CHUNK, 0:128], zero)
    N_br = jnp.where(incl, G[0:CHUNK, 128:256], zero)
    A_ak = jnp.where(strict, G[CHUNK:128, 0:128], zero)
    N_kr = jnp.where(incl, G[CHUNK:128, 128:256], zero)
    Ab = A_ab.astype(BF16)
    M = _dot(Ab, _bd(Ab, m0))
    N = A_ab
    for _ in range(4):
        Mb = M.astype(BF16)
        Rm = _dot(jnp.concatenate([N.astype(BF16), Mb], axis=0), _bd(Mb, m0))
        N = N + M + Rm[0:CHUNK]
        M = Rm[CHUNK:2 * CHUNK]
    N = N + M + _dot(N.astype(BF16), _bd(M.astype(BF16), m0))
    cat2 = lambda y: jnp.concatenate([y, y], axis=0)
    VG = _dot(vtm, jnp.concatenate([cat2(A_ak.astype(BF16)), cat2(N_kr.astype(BF16)), cat2(Kh)], axis=1))
    VA = _sel(VG[:, 0:128], m0)
    VN = _sel(VG[:, 128:256], m0)
    VK = _sel(VG[:, 256:384], m0)
    SG = _dot_nt(St.astype(BF16), rhs1)
    X = SG[:, 0:128] + VA
    U = X + _dot(X.astype(BF16), _bd(N.astype(BF16), m0))
    UG = _dot(U.astype(BF16), jnp.concatenate([_bd(N_br.astype(BF16), m0), _bd(Bh, m0)], axis=1))
    Yt = SG[:, 128:256] + UG[:, 0:128] + VN
    S_new = St * pend + UG[:, 128:256] + VK
    del eye
    return S_new, Yt


def _wkv_kernel(*refs):
    (at_f, bt_f, kt_f, rt_f, bh_f, kh_f, pe_f, vt_f,
     at_b, bt_b, kt_b, rt_b, bh_b, kh_b, pe_b, vt_b,
     yf_ref, yb_ref, s_ref) = refs
    s = pl.program_id(1)

    @pl.when(s == 0)
    def _():
        s_ref[...] = jnp.zeros_like(s_ref)

    lane = lax.broadcasted_iota(jnp.int32, (CHUNK, 2 * CHUNK), 1)
    rowi = lax.broadcasted_iota(jnp.int32, (CHUNK, 2 * CHUNK), 0)
    lm = jnp.bitwise_and(lane, CHUNK - 1)
    m0 = lane < CHUNK
    lane2 = lax.broadcasted_iota(jnp.int32, (2 * CHUNK, 2 * CHUNK), 1)
    dirs = ((at_f, bt_f, kt_f, rt_f, bh_f, kh_f, pe_f, vt_f, yf_ref),
            (at_b, bt_b, kt_b, rt_b, bh_b, kh_b, pe_b, vt_b, yb_ref))
    for d in range(2):
        at, bt, kt, rt, bh, kh, pe, vt, y_ref = dirs[d]
        strict = (rowi < lm) if d == 0 else (rowi > lm)
        incl = (rowi <= lm) if d == 0 else (rowi >= lm)
        for p in range(N_PAIRS):
            ls = slice(p * 128, (p + 1) * 128)
            St = s_ref[d, p]
            vt_p = vt[0, ls, :]
            ys = [None, None]
            for step in range(2):
                half = step if d == 0 else 1 - step
                rs = slice(half * CHUNK, (half + 1) * CHUNK)
                in_half = (lane2 < CHUNK) if half == 0 else (lane2 >= CHUNK)
                vtm = jnp.where(in_half, vt_p, jnp.zeros_like(vt_p))
                St, Yt = _wkv_chunk(St, at[0, rs, ls], rt[0, rs, ls], bt[0, rs, ls], kt[0, rs, ls],
                                    bh[0, rs, ls], kh[0, rs, ls], vtm, pe[0, half, 0:1, ls],
                                    m0, strict, incl, None)
                ys[half] = Yt
            s_ref[d, p] = St
            y0r = pltpu.roll(ys[0], CHUNK, axis=1)
            y1r = pltpu.roll(ys[1], CHUNK, axis=1)
            y_ref[0, p * 128:p * 128 + CHUNK, :] = jnp.where(m0, ys[0], y1r)
            y_ref[0, p * 128 + CHUNK:(p + 1) * 128, :] = jnp.where(m0, y0r, ys[1])


def _wkv(prep, n_ctx):
    (at_f, bt_f, kt_f, rt_f, bh_f, kh_f, pe_f, at_b, bt_b, kt_b, rt_b, bh_b, kh_b, pe_b, vt) = prep
    B, T, _ = at_f.shape
    ns = T // (2 * CHUNK)
    nc2 = n_ctx // (2 * CHUNK)

    def mrev(s):
        return jnp.where(s < nc2, nc2 - 1 - s, ns - 1 - (s - nc2))

    def specs(idx):
        tm = pl.BlockSpec((1, 2 * CHUNK, D_HEADS), lambda b, s: (b, idx(s), 0))
        pe = pl.BlockSpec((1, 2, 8, D_HEADS), lambda b, s: (b, idx(s), 0, 0))
        vts = pl.BlockSpec((1, D_HEADS, 2 * CHUNK), lambda b, s: (b, 0, idx(s)))
        return [tm] * 6 + [pe, vts]

    fwd = lambda s: s
    yt = lambda idx: pl.BlockSpec((1, D_HEADS, 2 * CHUNK), lambda b, s: (b, 0, idx(s)))
    return pl.pallas_call(
        _wkv_kernel,
        grid=(B, ns),
        in_specs=specs(fwd) + specs(mrev),
        out_specs=[yt(fwd), yt(mrev)],
        out_shape=[jax.ShapeDtypeStruct((B, D_HEADS, T), F32)] * 2,
        scratch_shapes=[pltpu.VMEM((2, N_PAIRS, CHUNK, 2 * CHUNK), F32)],
        compiler_params=_cparams(("arbitrary", "arbitrary")),
    )(at_f, bt_f, kt_f, rt_f, bh_f, kh_f, pe_f, vt, at_b, bt_b, kt_b, rt_b, bh_b, kh_b, pe_b, vt)


def _rwpost_kernel(yf_ref, yb_ref, bonus_ref, g_ref, gg_ref, gb_ref, o_ref):
    y = yf_ref[0] + yb_ref[0]
    R = y.shape[1]
    y3 = y.reshape(N_HEADS, HEAD_DIM, R)
    mu = jnp.mean(y3, axis=1, keepdims=True)
    var = jnp.mean(jnp.square(y3 - mu), axis=1, keepdims=True)
    yn = ((y3 - mu) * lax.rsqrt(var + GN_EPS)).reshape(D_HEADS, R)
    out = (yn.T * gg_ref[...] + gb_ref[...] + bonus_ref[0]) * g_ref[0]
    o_ref[0] = out.astype(BF16)


def _rwpost(yf, yb, bonus, g, gn_g, gn_b):
    B, _, T = yf.shape
    nt = T // ROW_TILE
    ytile = pl.BlockSpec((1, D_HEADS, ROW_TILE), lambda b, t: (b, 0, t))
    tm = pl.BlockSpec((1, ROW_TILE, D_HEADS), lambda b, t: (b, t, 0))
    row = pl.BlockSpec((1, D_HEADS), lambda b, t: (0, 0))
    return pl.pallas_call(
        _rwpost_kernel,
        grid=(B, nt),
        in_specs=[ytile, ytile, tm, tm, row, row],
        out_specs=tm,
        out_shape=jax.ShapeDtypeStruct((B, T, D_HEADS), BF16),
        compiler_params=_cparams(("arbitrary", "arbitrary")),
    )(yf, yb, bonus, g, gn_g, gn_b)


def _attn_kernel(q_ref, k_ref, v_ref, bias_ref, o_ref, *, n_ctx, n_rows):
    j = pl.program_id(1)
    n_cblk = n_ctx // GRID_W
    lane = lax.broadcasted_iota(jnp.int32, (GRID_W, 128), 1)
    m0 = lane < HEAD_DIM
    win = WIN_ROWS * GRID_W

    def stacked_q(p):
        qp = q_ref[0, :, p * 128:(p + 1) * 128]
        return _bd(qp, m0)

    @pl.when(j < n_cblk)
    def _():
        for p in range(N_PAIRS):
            ls = slice(p * 128, (p + 1) * 128)
            sc = _dot_nt(stacked_q(p), k_ref[0, 0:n_ctx, ls])
            m = jnp.max(sc, axis=-1, keepdims=True)
            e = jnp.exp(sc - m)
            den = jnp.sum(e, axis=-1, keepdims=True)
            o = _dot(e.astype(BF16), v_ref[0, 0:n_ctx, ls]) / den
            o_ref[0, :, ls] = _sel(o, m0).astype(BF16)

    @pl.when(j >= n_cblk)
    def _():
        i = j - n_cblk
        rs = jnp.clip(i - WIN_ROWS // 2, 0, n_rows - WIN_ROWS)
        start = pl.multiple_of(n_ctx + rs * GRID_W, GRID_W)
        for p in range(N_PAIRS):
            ls = slice(p * 128, (p + 1) * 128)
            qs = stacked_q(p)
            s_loc = _dot_nt(qs, k_ref[0, pl.ds(start, win), ls]) + bias_ref[0, p]
            s_ctx = _dot_nt(qs, k_ref[0, 0:n_ctx, ls])
            m = jnp.maximum(jnp.max(s_loc, axis=-1, keepdims=True), jnp.max(s_ctx, axis=-1, keepdims=True))
            e_loc = jnp.exp(s_loc - m)
            e_ctx = jnp.exp(s_ctx - m)
            den = jnp.sum(e_loc, axis=-1, keepdims=True) + jnp.sum(e_ctx, axis=-1, keepdims=True)
            o = _dot(e_loc.astype(BF16), v_ref[0, pl.ds(start, win), ls])
            o = (o + _dot(e_ctx.astype(BF16), v_ref[0, 0:n_ctx, ls])) / den
            o_ref[0, :, ls] = _sel(o, m0).astype(BF16)


def _attention(qkv, bias_tab, n_ctx):
    B, T, _ = qkv.shape
    n_rows = (T - n_ctx) // GRID_W
    n_cblk = n_ctx // GRID_W
    half = WIN_ROWS // 2

    def delta(j):
        i = jnp.maximum(j - n_cblk, 0)
        return jnp.minimum(i, half) + jnp.maximum(i - (n_rows - half), 0)

    return pl.pallas_call(
        functools.partial(_attn_kernel, n_ctx=n_ctx, n_rows=n_rows),
        grid=(B, T // GRID_W),
        in_specs=[
            pl.BlockSpec((1, GRID_W, D_HEADS), lambda b, j: (b, j, 0)),
            pl.BlockSpec((1, T, D_HEADS), lambda b, j: (b, 0, 1)),
            pl.BlockSpec((1, T, D_HEADS), lambda b, j: (b, 0, 2)),
            pl.BlockSpec((1, N_PAIRS, 128, WIN_ROWS * GRID_W), lambda b, j: (delta(j), 0, 0, 0)),
        ],
        out_specs=pl.BlockSpec((1, GRID_W, D_HEADS), lambda b, j: (b, j, 0)),
        out_shape=jax.ShapeDtypeStruct((B, T, D_HEADS), BF16),
        compiler_params=_cparams(("arbitrary", "arbitrary")),
    )(qkv, qkv, qkv, bias_tab)


def _na_bias_table(rpb):
    H = rpb.shape[0]
    dl = np.arange(WIN_ROWS)[:, None]
    r = np.arange(WIN_ROWS)[None, :]
    ridx = r - dl + (WIN_ROWS - 1)
    c = np.arange(GRID_W)[:, None]
    kc = np.arange(GRID_W)[None, :]
    cs = np.clip(c - WIN_COLS // 2, 0, GRID_W - WIN_COLS)
    valid = (kc >= cs) & (kc < cs + WIN_COLS)
    cidx = np.clip(kc - c + (WIN_COLS - 1), 0, 2 * WIN_COLS - 2)
    tab = rpb[:, ridx]
    tab = tab[:, :, :, cidx]
    tab = jnp.where(valid[None, None, None], tab, NEG)
    tab = tab.transpose(1, 0, 3, 2, 4).reshape(WIN_ROWS, H, GRID_W, WIN_ROWS * GRID_W)
    return tab.reshape(WIN_ROWS, H // 2, 2 * GRID_W, WIN_ROWS * GRID_W).astype(F32)


def _layer_norm(h, g, b):
    mu = jnp.mean(h, axis=-1, keepdims=True)
    var = jnp.mean(jnp.square(h - mu), axis=-1, keepdims=True)
    return (h - mu) * lax.rsqrt(var + LN_EPS) * g + b


def _outproj_kernel(na_ref, rw_ref, w_ref, z_ref, mod_ref, g_ref, b_ref, rw_w_ref,
                    z1_ref, hx_ref, lg_ref, *, d, alpha):
    o = _dot(na_ref[0], w_ref[0:D_HEADS, :]) + _dot(rw_ref[0], w_ref[D_HEADS:2 * D_HEADS, :])
    mod = mod_ref[0, 0]
    gate = mod[:, 2 * d:3 * d]
    z1 = _layer_norm(alpha * z_ref[0] + gate * o, g_ref[...], b_ref[...])
    z1_ref[0] = z1
    hx = z1 * (1.0 + mod[:, 4 * d:5 * d]) + mod[:, 3 * d:4 * d]
    hx_ref[0] = hx
    lg_ref[0] = _dot3(hx, rw_w_ref[...])


def _outproj(na, rw, w_out_b, z, modsel, ln_g, ln_b, router_wp, n_ctx_tiles, alpha):
    B, T, D = z.shape
    nt = T // ROW_TILE
    half = pl.BlockSpec((1, ROW_TILE, D_HEADS), lambda b, t: (b, t, 0))
    tile = pl.BlockSpec((1, ROW_TILE, D), lambda b, t: (b, t, 0))
    row = pl.BlockSpec((1, D), lambda b, t: (0, 0))
    return pl.pallas_call(
        functools.partial(_outproj_kernel, d=D, alpha=alpha),
        grid=(B, nt),
        in_specs=[
            half, half,
            pl.BlockSpec((2 * D_HEADS, D), lambda b, t: (0, 0)),
            tile,
            pl.BlockSpec((1, 1, 1, 6 * D), lambda b, t: (b, jnp.where(t < n_ctx_tiles, 0, 1), 0, 0)),
            row, row,
            pl.BlockSpec((D, 128), lambda b, t: (0, 0)),
        ],
        out_specs=[tile, tile, pl.BlockSpec((1, ROW_TILE, 128), lambda b, t: (b, t, 0))],
        out_shape=[
            jax.ShapeDtypeStruct((B, T, D), F32),
            jax.ShapeDtypeStruct((B, T, D), F32),
            jax.ShapeDtypeStruct((B, T, 128), F32),
        ],
        compiler_params=_cparams(("arbitrary", "arbitrary")),
    )(na, rw, w_out_b, z, modsel, ln_g, ln_b, router_wp)


def _dispatch_kernel(dest_ref, hx_ref, gate_ref, xs_in, xs_out, buf, sem):
    del xs_in
    n = buf.shape[0]
    buf[:, 0:1024] = hx_ref[...]
    buf[:, 1024:XS_W] = gate_ref[...]

    def start(r, c):
        pltpu.make_async_copy(buf.at[pl.ds(r, 1)], xs_out.at[pl.ds(dest_ref[r], 1)], sem).start()
        return c

    lax.fori_loop(0, n, start, 0)

    def wait(r, c):
        pltpu.make_async_copy(buf.at[pl.ds(0, 1)], xs_out.at[pl.ds(0, 1)], sem).wait()
        return c

    lax.fori_loop(0, n, wait, 0)


def _dispatch(dest, hx, gates, n_rows_pad):
    N, D = hx.shape
    xs0 = jnp.zeros((n_rows_pad, XS_W), F32)
    return pl.pallas_call(
        _dispatch_kernel,
        grid=(N // ROW_TILE,),
        in_specs=[
            pl.BlockSpec((ROW_TILE,), lambda i: (i,), memory_space=pltpu.SMEM),
            pl.BlockSpec((ROW_TILE, D), lambda i: (i, 0)),
            pl.BlockSpec((ROW_TILE, 128), lambda i: (i, 0)),
            pl.BlockSpec(memory_space=pl.ANY),
        ],
        out_specs=pl.BlockSpec(memory_space=pl.ANY),
        out_shape=jax.ShapeDtypeStruct((n_rows_pad, XS_W), F32),
        scratch_shapes=[pltpu.VMEM((ROW_TILE, XS_W), F32), pltpu.SemaphoreType.DMA(())],
        input_output_aliases={3: 0},
        compiler_params=_cparams(("arbitrary",)),
    )(dest, hx, gates, xs0)


def _expert_kernel(ea_ref, eb_ref, valid_ref, xs_ref, w1a, w3a, w2a, w1b, w3b, w2b, ys_ref,
                   c1a, c3a, c2a, c1b, c3b, c2b):
    i = pl.program_id(0)
    prev = jnp.maximum(i - 1, 0)
    changed = jnp.logical_or(i == 0, jnp.logical_or(ea_ref[i] != ea_ref[prev], eb_ref[i] != eb_ref[prev]))

    @pl.when(changed)
    def _():
        c1a[...] = w1a[0].astype(BF16)
        c3a[...] = w3a[0].astype(BF16)
        c2a[...] = w2a[0].astype(BF16)
        c1b[...] = w1b[0].astype(BF16)
        c3b[...] = w3b[0].astype(BF16)
        c2b[...] = w2b[0].astype(BF16)

    @pl.when(valid_ref[i] != 0)
    def _():
        x = xs_ref[:, 0:1024].astype(BF16)
        ga = xs_ref[:, 1024:1025]
        gb = xs_ref[:, 1025:1026]

        def ffn(c1, c3, c2):
            h1 = _dot(x, c1[...])
            h3 = _dot(x, c3[...])
            h = (h1 * jax.nn.sigmoid(h1)) * h3
            return _dot(h.astype(BF16), c2[...])

        ys_ref[...] = ga * ffn(c1a, c3a, c2a) + gb * ffn(c1b, c3b, c2b)

    @pl.when(valid_ref[i] == 0)
    def _():
        ys_ref[...] = jnp.zeros_like(ys_ref)


def _experts(blk_ea, blk_eb, blk_valid, xs, w1, w3, w2):
    n_rows_pad = xs.shape[0]
    nb = n_rows_pad // MOE_TILE
    _, D, DE = w1.shape
    wa = lambda shape: pl.BlockSpec(shape, lambda i, ea, eb, va: (ea[i], 0, 0))
    wb = lambda shape: pl.BlockSpec(shape, lambda i, ea, eb, va: (eb[i], 0, 0))
    grid_spec = pltpu.PrefetchScalarGridSpec(
        num_scalar_prefetch=3,
        grid=(nb,),
        in_specs=[
            pl.BlockSpec((MOE_TILE, XS_W), lambda i, ea, eb, va: (i, 0)),
            wa((1, D, DE)), wa((1, D, DE)), wa((1, DE, D)),
            wb((1, D, DE)), wb((1, D, DE)), wb((1, DE, D)),
        ],
        out_specs=pl.BlockSpec((MOE_TILE, D), lambda i, ea, eb, va: (i, 0)),
        scratch_shapes=[pltpu.VMEM((D, DE), BF16), pltpu.VMEM((D, DE), BF16), pltpu.VMEM((DE, D), BF16),
                        pltpu.VMEM((D, DE), BF16), pltpu.VMEM((D, DE), BF16), pltpu.VMEM((DE, D), BF16)],
    )
    return pl.pallas_call(
        _expert_kernel,
        grid_spec=grid_spec,
        out_shape=jax.ShapeDtypeStruct((n_rows_pad, D), F32),
        compiler_params=_cparams(("arbitrary",)),
    )(blk_ea, blk_eb, blk_valid, xs, w1, w3, w2, w1, w3, w2)


def _combine_kernel(dest_ref, ys_ref, z1_ref, mod_ref, g_ref, b_ref, o_ref, buf, sem, *, d, alpha):
    n = buf.shape[0]

    def start(r, c):
        pltpu.make_async_copy(ys_ref.at[pl.ds(dest_ref[r], 1)], buf.at[pl.ds(r, 1)], sem).start()
        return c

    lax.fori_loop(0, n, start, 0)

    def wait(r, c):
        pltpu.make_async_copy(ys_ref.at[pl.ds(0, 1)], buf.at[pl.ds(0, 1)], sem).wait()
        return c

    lax.fori_loop(0, n, wait, 0)
    gate = mod_ref[0, 0][:, 5 * d:6 * d]
    o_ref[0] = _layer_norm(alpha * z1_ref[0] + gate * buf[...], g_ref[...], b_ref[...])


def _combine(dest, ys, z1, modsel, ln_g, ln_b, n_ctx_tiles, alpha):
    B, T, D = z1.shape
    nt = T // ROW_TILE
    tile = pl.BlockSpec((1, ROW_TILE, D), lambda b, t: (b, t, 0))
    row = pl.BlockSpec((1, D), lambda b, t: (0, 0))
    return pl.pallas_call(
        functools.partial(_combine_kernel, d=D, alpha=alpha),
        grid=(B, nt),
        in_specs=[
            pl.BlockSpec((ROW_TILE,), lambda b, t: (b * nt + t,), memory_space=pltpu.SMEM),
            pl.BlockSpec(memory_space=pl.ANY),
            tile,
            pl.BlockSpec((1, 1, 1, 6 * D), lambda b, t: (b, jnp.where(t < n_ctx_tiles, 0, 1), 0, 0)),
            row, row,
        ],
        out_specs=tile,
        out_shape=jax.ShapeDtypeStruct((B, T, D), F32),
        scratch_shapes=[pltpu.VMEM((ROW_TILE, D), F32), pltpu.SemaphoreType.DMA(())],
        compiler_params=_cparams(("arbitrary", "arbitrary")),
    )(dest, ys, z1, modsel, ln_g, ln_b)


_PAIR_LO = np.array([0, 0, 0, 1, 1, 2], np.int32)
_PAIR_HI = np.array([1, 2, 3, 2, 3, 3], np.int32)


def _route(logits, router_bias):
    N = logits.shape[0]
    scores = jax.nn.sigmoid(logits[:, :N_EXPERTS])
    sel = (scores + router_bias.astype(F32)).reshape(N, N_GROUPS, EXPERTS_PER_GROUP)
    pos = jnp.arange(EXPERTS_PER_GROUP, dtype=jnp.int32)
    a1 = jnp.argmax(sel, axis=-1)
    m1 = jnp.max(sel, axis=-1)
    rest = jnp.where(pos == a1[..., None], -jnp.inf, sel)
    m2 = jnp.max(rest, axis=-1)
    g_idx = jnp.argmax(m1 + m2, axis=-1).astype(jnp.int32)
    sel_g = jnp.take_along_axis(sel, g_idx[:, None, None], axis=1)[:, 0]
    i1 = jnp.argmax(sel_g, axis=-1).astype(jnp.int32)
    i2 = jnp.argmax(jnp.where(pos == i1[:, None], -jnp.inf, sel_g), axis=-1).astype(jnp.int32)
    sc_g = jnp.take_along_axis(scores.reshape(N, N_GROUPS, EXPERTS_PER_GROUP), g_idx[:, None, None], axis=1)[:, 0]
    lo = jnp.minimum(i1, i2)
    hi = jnp.maximum(i1, i2)
    g_lo = jnp.take_along_axis(sc_g, lo[:, None], axis=1)[:, 0]
    g_hi = jnp.take_along_axis(sc_g, hi[:, None], axis=1)[:, 0]
    tot = g_lo + g_hi
    g_lo = g_lo / tot
    g_hi = g_hi / tot
    pair = lo * 3 - (lo * (lo - 1)) // 2 + (hi - lo - 1)
    cls = g_idx * 6 + pair

    onehot = (cls[:, None] == jnp.arange(N_CLASSES, dtype=jnp.int32)[None, :])
    oh = onehot.astype(BF16).reshape(N // ROW_TILE, ROW_TILE, N_CLASSES)
    tri = jnp.tril(jnp.ones((ROW_TILE, ROW_TILE), BF16), -1)
    within = jnp.einsum("rs,tsc->trc", tri, oh, preferred_element_type=F32)
    tile_cnt = jnp.sum(oh.astype(F32), axis=1)
    tile_off = jnp.cumsum(tile_cnt, axis=0) - tile_cnt
    rank_all = (within + tile_off[:, None, :]).reshape(N, N_CLASSES)
    rank = jnp.sum(jnp.where(onehot, rank_all, 0.0), axis=1).astype(jnp.int32)
    counts = jnp.sum(tile_cnt, axis=0).astype(jnp.int32)
    padded = (counts + MOE_TILE - 1) // MOE_TILE * MOE_TILE
    cls_end = jnp.cumsum(padded)
    cls_start = cls_end - padded
    dest = cls_start[cls] + rank

    nb = N // MOE_TILE + N_CLASSES
    blk_row = jnp.arange(nb, dtype=jnp.int32) * MOE_TILE
    blk_valid = (blk_row < cls_end[-1]).astype(jnp.int32)
    blk_cls = jnp.minimum(jnp.searchsorted(cls_end, blk_row, side="right"), N_CLASSES - 1).astype(jnp.int32)
    n_used = cls_end[-1] // MOE_TILE
    last_used_cls = blk_cls[jnp.maximum(n_used - 1, 0)]
    blk_cls = jnp.where(blk_valid != 0, blk_cls, last_used_cls)
    blk_grp = blk_cls // 6
    blk_pair = blk_cls % 6
    blk_ea = blk_grp * EXPERTS_PER_GROUP + jnp.asarray(_PAIR_LO)[blk_pair]
    blk_eb = blk_grp * EXPERTS_PER_GROUP + jnp.asarray(_PAIR_HI)[blk_pair]
    gates = jnp.zeros((N, 128), F32).at[:, 0].set(g_lo).at[:, 1].set(g_hi)
    return dest.astype(jnp.int32), gates, blk_ea.astype(jnp.int32), blk_eb.astype(jnp.int32), blk_valid, nb * MOE_TILE


def _rope_tables(n_ctx, seq):
    t = np.arange(seq)
    row = (t // GRID_W).astype(np.float32)
    col = (t % GRID_W).astype(np.float32)
    n_freq = HEAD_DIM // 4
    inv = jnp.asarray(ROPE_BASE, F32) ** (-jnp.arange(n_freq, dtype=F32) / n_freq)
    ar = jnp.asarray(row)[:, None] * inv
    ac = jnp.asarray(col)[:, None] * inv
    ang = jnp.concatenate([ar, ar, ac, ac], -1)
    cos = jnp.cos(ang)
    sin = jnp.sin(ang)
    quarter = (np.arange(HEAD_DIM) // n_freq) % 2
    sa = jnp.where(quarter == 0, -sin, 0.0)
    sb = jnp.where(quarter == 1, sin, 0.0)
    pad = lambda a, fill: jnp.concatenate([jnp.full((n_ctx, HEAD_DIM), fill, F32), a], 0)
    tile = lambda a: jnp.tile(a, (1, N_HEADS))
    return tile(pad(cos, 1.0)), tile(pad(sa, 0.0)), tile(pad(sb, 0.0))


def _rw_consts(l, n_ctx, seq, rw_mu_prev, rw_mu_next, rw_w0, rw_w2, rw_a0, rw_a2, rw_g2, rw_k_k, rw_k_a, rw_r_k):
    d_rw_in = rw_mu_prev.shape[1]
    padw = lambda a: jnp.pad(a[l], (0, D_RWP - d_rw_in)).reshape(1, D_RWP)
    cos, sa, sb = _rope_tables(n_ctx, seq)
    w2p = jnp.zeros((128, 2 * D_HEADS), F32)
    w2p = w2p.at[0:LORA, 0:D_HEADS].set(rw_w2[l, 0]).at[LORA:2 * LORA, D_HEADS:].set(rw_w2[l, 1])
    a2p = jnp.zeros((128, 2 * D_HEADS), F32)
    a2p = a2p.at[2 * LORA:3 * LORA, 0:D_HEADS].set(rw_a2[l, 0]).at[3 * LORA:4 * LORA, D_HEADS:].set(rw_a2[l, 1])
    g2p = jnp.zeros((128, D_HEADS), F32).at[0:GATE_LORA].set(rw_g2[l])
    head = np.arange(D_HEADS) // HEAD_DIM
    ones_bd = jnp.asarray(head[:, None] == head[None, :], BF16)
    return dict(
        mu_prev=padw(rw_mu_prev), mu_next=padw(rw_mu_next), cos=cos, sa=sa, sb=sb,
        k_k=rw_k_k[l].reshape(1, D_HEADS), k_a=rw_k_a[l].reshape(1, D_HEADS), r_k=rw_r_k[l].reshape(1, D_HEADS),
        w0=rw_w0[l].reshape(1, 2 * D_HEADS), a0=rw_a0[l].reshape(1, 2 * D_HEADS),
        w2p=w2p, a2p=a2p, g2p=g2p, ones_bd=ones_bd,
    )


def kernel(x, c, ctx, c_ctx, ada_w, ada_b, w_in, na_rpb, rw_mu_prev, rw_mu_next, rw_w0, rw_w2, rw_a0, rw_a2, rw_g2, rw_k_k, rw_k_a, rw_r_k, rw_gn_g, rw_gn_b, w_out, ln1_g, ln1_b, ln2_g, ln2_b, router_w, router_bias, exp_w1, exp_w3, exp_w2):
    B, S, D = x.shape
    C = ctx.shape[1]
    L = ada_w.shape[0]
    T = C + S
    assert D == 1024 and C % ROW_TILE == 0 and S % ROW_TILE == 0 and C % (2 * CHUNK) == 0
    assert S % GRID_W == 0 and S // GRID_W >= WIN_ROWS and w_in.shape[2] == D_INP - 32
    n_ctx_tiles = C // ROW_TILE
    alpha = float((2 * L) ** 0.25)

    z = jnp.concatenate([ctx, x], axis=1)
    n_mod = (B + 1 + 7) // 8 * 8
    cc = jnp.zeros((n_mod, D), F32).at[0:B].set(c).at[B].set(c_ctx)
    mod_all = _ada(cc, ada_w, ada_b)
    router_wp = jnp.pad(router_w, ((0, 0), (0, 128 - N_EXPERTS)))

    for l in range(L):
        mod_c = jnp.broadcast_to(mod_all[l, B][None], (B, 6 * D))
        modsel = jnp.stack([mod_c, mod_all[l, 0:B]], axis=1).reshape(B, 2, 1, 6 * D)
        w_in_p = jnp.pad(w_in[l], ((0, 0), (0, D_INP - w_in.shape[2]))).astype(BF16)
        qkv, p_rw = _inproj(z, modsel, w_in_p, n_ctx_tiles)

        consts = _rw_consts(l, C, S, rw_mu_prev, rw_mu_next, rw_w0, rw_w2, rw_a0, rw_a2, rw_g2,
                            rw_k_k, rw_k_a, rw_r_k)
        prep = _rwprep(p_rw, consts, n_ctx_tiles)
        yf, yb = _wkv(prep[0:15], C)
        rw = _rwpost(yf, yb, prep[15], prep[16], rw_gn_g[l].reshape(1, D_HEADS), rw_gn_b[l].reshape(1, D_HEADS))

        na = _attention(qkv, _na_bias_table(na_rpb[l]), C)

        z1, hx, logits = _outproj(na, rw, w_out[l].astype(BF16), z, modsel,
                                  ln1_g[l].reshape(1, D), ln1_b[l].reshape(1, D), router_wp, n_ctx_tiles, alpha)

        dest, gates, blk_ea, blk_eb, blk_valid, n_rows_pad = _route(logits.reshape(B * T, 128), router_bias)
        xs = _dispatch(dest, hx.reshape(B * T, D), gates, n_rows_pad)
        ys = _experts(blk_ea, blk_eb, blk_valid, xs, exp_w1[l], exp_w3[l], exp_w2[l])
        z = _combine(dest, ys, z1, modsel, ln2_g[l].reshape(1, D), ln2_b[l].reshape(1, D), n_ctx_tiles, alpha)

    return z[:, C:, :]
```

```python
import functools
import math

import jax
import jax.numpy as jnp
import numpy as np
from jax import lax
from jax.experimental import pallas as pl
from jax.experimental.pallas import tpu as pltpu

F32 = jnp.float32
BF16 = jnp.bfloat16

HEAD_DIM = 64
N_HEADS = 8
D_HEADS = N_HEADS * HEAD_DIM
N_PAIRS = N_HEADS // 2
GRID_W = 64
WIN_ROWS = 8
WIN_COLS = 16
LORA = 32
GATE_LORA = 96
N_EXPERTS = 32
N_GROUPS = 8
EXPERTS_PER_GROUP = 4
N_CLASSES = N_GROUPS * 6
ROPE_BASE = 10000.0
LN_EPS = 1e-6
GN_EPS = 64e-5
CHUNK = 64
ROW_TILE = 256
MOE_TILE = 256
D_QKV = 3 * D_HEADS
D_RWP = 3 * D_HEADS + 256
D_INP = D_QKV + D_RWP
TOK_SUB = 8
DISP_TILE = ROW_TILE
NEG = -1e30
VMEM_LIMIT = 56 * 1024 * 1024


def _cparams(sem):
    return pltpu.CompilerParams(dimension_semantics=sem, vmem_limit_bytes=VMEM_LIMIT)


def _dot(a, b):
    return jnp.dot(a, b, preferred_element_type=F32)


def _dot_nt(a, b):
    return lax.dot_general(a, b, (((1,), (1,)), ((), ())), preferred_element_type=F32)


def _split2(a):
    hi = a.astype(BF16)
    lo = (a - hi.astype(F32)).astype(BF16)
    return hi, lo


def _split3(a):
    hi = a.astype(BF16)
    r1 = a - hi.astype(F32)
    mid = r1.astype(BF16)
    lo = (r1 - mid.astype(F32)).astype(BF16)
    return hi, mid, lo


def _dot3(a, b):
    ah, al = _split2(a)
    bh, bl = _split2(b)
    return _dot(ah, bh) + _dot(al, bh) + _dot(ah, bl)


def _dot_exact_rhs(a, b_exact):
    h, m, l = _split3(a)
    return _dot(h, b_exact) + _dot(m, b_exact) + _dot(l, b_exact)


def _dot_exact_lhs(a_exact, b):
    h, m, l = _split3(b)
    return _dot(a_exact, h) + _dot(a_exact, m) + _dot(a_exact, l)


def _ada_kernel(cc_ref, w_ref, b_ref, o_ref):
    cc = cc_ref[...]
    s = cc * jax.nn.sigmoid(cc)
    o_ref[0] = _dot3(s, w_ref[0]) + b_ref[0]


def _ada(cc, ada_w, ada_b):
    L, D, D6 = ada_w.shape
    R = cc.shape[0]
    tn = 1536
    return pl.pallas_call(
        _ada_kernel,
        grid=(L, D6 // tn),
        in_specs=[
            pl.BlockSpec((R, D), lambda l, n: (0, 0)),
            pl.BlockSpec((1, D, tn), lambda l, n: (l, 0, n)),
            pl.BlockSpec((1, 1, tn), lambda l, n: (l, 0, n)),
        ],
        out_specs=pl.BlockSpec((1, R, tn), lambda l, n: (l, 0, n)),
        out_shape=jax.ShapeDtypeStruct((L, R, D6), F32),
        compiler_params=_cparams(("arbitrary", "arbitrary")),
    )(cc, ada_w, ada_b.reshape(L, 1, D6))


def _z_specs(D, n_ctx_tiles, sub, t0=0):
    cspec = pl.BlockSpec((1, ROW_TILE, D), lambda b, t: (b, jnp.minimum(t + t0, n_ctx_tiles - 1), 0))
    xspec = pl.BlockSpec((1, ROW_TILE, D), lambda b, t: (b, jnp.maximum(t + t0, n_ctx_tiles) - sub, 0))
    return cspec, xspec


def _z_tile(zc_ref, zx_ref, n_ctx_tiles, t0=0):
    return jnp.where(pl.program_id(1) + t0 < n_ctx_tiles, zc_ref[0], zx_ref[0])


def _mod_spec(D, n_ctx_tiles, t0=0):
    return pl.BlockSpec((1, 1, 1, 6 * D), lambda b, t: (b, jnp.where(t + t0 < n_ctx_tiles, 0, 1), 0, 0))


def _inproj_kernel(zc_ref, zx_ref, mod_ref, w_ref, qkv_ref, rw_ref, *, d, n_ctx_tiles):
    z = _z_tile(zc_ref, zx_ref, n_ctx_tiles)
    mod = mod_ref[0, 0]
    shift = mod[:, 0:d]
    scale = mod[:, d:2 * d]
    h = (z * (1.0 + scale) + shift).astype(BF16)
    q = _dot(h, w_ref[:, 0:D_HEADS])
    qkv_ref[0, :, 0:D_HEADS] = (q * (HEAD_DIM ** -0.5)).astype(BF16)
    kv = _dot(h, w_ref[:, D_HEADS:D_QKV])
    qkv_ref[0, :, D_HEADS:D_QKV] = kv.astype(BF16)
    rw_ref[0] = _dot(h, w_ref[:, D_QKV:D_INP])


def _inproj(zc, zx, sub, T, modsel, w_in_p, n_ctx_tiles):
    B, _, D = zc.shape
    nt = T // ROW_TILE
    cspec, xspec = _z_specs(D, n_ctx_tiles, sub)
    return pl.pallas_call(
        functools.partial(_inproj_kernel, d=D, n_ctx_tiles=n_ctx_tiles),
        grid=(B, nt),
        in_specs=[
            cspec, xspec,
            pl.BlockSpec((1, 1, 1, 6 * D), lambda b, t: (b, jnp.where(t < n_ctx_tiles, 0, 1), 0, 0)),
            pl.BlockSpec((D, D_INP), lambda b, t: (0, 0)),
        ],
        out_specs=[
            pl.BlockSpec((1, ROW_TILE, D_QKV), lambda b, t: (b, t, 0)),
            pl.BlockSpec((1, ROW_TILE, D_RWP), lambda b, t: (b, t, 0)),
        ],
        out_shape=[
            jax.ShapeDtypeStruct((B, T, D_QKV), BF16),
            jax.ShapeDtypeStruct((B, T, D_RWP), F32),
        ],
        compiler_params=_cparams(("arbitrary", "arbitrary")),
    )(zc, zx, modsel, w_in_p)


def _rwprep_kernel(p_ref, pp_ref, pn_ref, mup_ref, mun_ref, cos_ref, sa_ref, sb_ref,
                   kk_ref, ka_ref, rk_ref, w0_ref, a0_ref, w2_ref, a2_ref, g2_ref, ones_ref,
                   at_f, bt_f, kt_f, rt_f, bh_f, kh_f, pe_f,
                   at_b, bt_b, kt_b, rt_b, bh_b, kh_b, pe_b,
                   vt_ref, bonus_ref, g_ref, *, n_ctx_tiles, n_tiles):
    t = pl.program_id(1)
    P = p_ref[0]
    R = P.shape[0]
    prev_ok = jnp.logical_and(t != 0, t != n_ctx_tiles)
    next_ok = jnp.logical_and(t != n_ctx_tiles - 1, t != n_tiles - 1)
    prev_row = jnp.where(prev_ok, pp_ref[0, 7:8, :], 0.0)
    next_row = jnp.where(next_ok, pn_ref[0, 0:1, :], 0.0)
    row = lax.broadcasted_iota(jnp.int32, (R, 1), 0)
    prev = jnp.where(row == 0, prev_row, pltpu.roll(P, 1, axis=0))
    nxt = jnp.where(row == R - 1, next_row, pltpu.roll(P, R - 1, axis=0))
    z = P + mup_ref[...] * (prev - P) + mun_ref[...] * (nxt - P)

    cos = cos_ref[...]
    sa = sa_ref[...]
    sb = sb_ref[...]

    def rope(u):
        return (u * cos + pltpu.roll(u, D_HEADS - HEAD_DIM // 4, axis=1) * sa
                + pltpu.roll(u, HEAD_DIM // 4, axis=1) * sb)

    r = rope(z[:, 0:D_HEADS])
    k = rope(z[:, D_HEADS:2 * D_HEADS])
    v = z[:, 2 * D_HEADS:3 * D_HEADS]
    ones_bd = ones_ref[...]

    def seg_sum(u):
        return _dot_exact_rhs(u, ones_bd)

    kk = k * kk_ref[...]
    kk = kk / jnp.maximum(jnp.sqrt(seg_sum(kk * kk)), 1e-12)

    slab = z[:, 3 * D_HEADS:3 * D_HEADS + 128]
    u_w = w0_ref[...] + _dot3(jnp.tanh(slab), w2_ref[...])
    u_a = a0_ref[...] + _dot3(slab, a2_ref[...])
    g_ref[0] = _dot3(jax.nn.sigmoid(z[:, 3 * D_HEADS + 128:D_RWP]), g2_ref[...])
    e_all = math.exp(-0.5) * jax.nn.sigmoid(u_w)
    a_all = jax.nn.sigmoid(u_a)

    ci = lax.broadcasted_iota(jnp.int32, (CHUNK, CHUNK), 0)
    cj = lax.broadcasted_iota(jnp.int32, (CHUNK, CHUNK), 1)
    ka = ka_ref[...]
    outs = ((at_f, bt_f, kt_f, rt_f, bh_f, kh_f, pe_f), (at_b, bt_b, kt_b, rt_b, bh_b, kh_b, pe_b))
    kd_sum = None
    for d in range(2):
        e = e_all[:, d * D_HEADS:(d + 1) * D_HEADS]
        a = a_all[:, d * D_HEADS:(d + 1) * D_HEADS]
        tri = (cj <= ci) if d == 0 else (cj >= ci)
        tri = tri.astype(BF16)
        cs, ce = [], []
        for q in range(R // CHUNK):
            cq = -_dot_exact_lhs(tri, e[q * CHUNK:(q + 1) * CHUNK])
            end = cq[CHUNK - 1:CHUNK] if d == 0 else cq[0:1]
            cs.append(cq)
            ce.append(jnp.broadcast_to(end, cq.shape))
        c = jnp.concatenate(cs, axis=0)
        cend = jnp.concatenate(ce, axis=0)
        kd = k * (1.0 + (a - 1.0) * ka)
        kd_sum = kd if kd_sum is None else kd_sum + kd
        beta = a * kk
        en = jnp.exp(-c)
        eh = jnp.exp(cend - c)
        o_at, o_bt, o_kt, o_rt, o_bh, o_kh, o_pe = outs[d]
        o_at[0] = (-kk * jnp.exp(c + e)).astype(BF16)
        o_bt[0] = (beta * en).astype(BF16)
        o_kt[0] = (kd * en).astype(BF16)
        o_rt[0] = (r * jnp.exp(c)).astype(BF16)
        o_bh[0] = (beta * eh).astype(BF16)
        o_kh[0] = (kd * eh).astype(BF16)
        for q in range(R // CHUNK):
            o_pe[0, q] = jnp.exp(cend[q * CHUNK:q * CHUNK + 8])
    bonus_ref[0] = seg_sum(r * rk_ref[...] * kd_sum) * v
    vt_ref[0] = v.T.astype(BF16)


def _rwprep(p_rw, consts, n_ctx_tiles):
    B, T, _ = p_rw.shape
    nt = T // ROW_TILE
    nh = ROW_TILE // 8
    row = lambda w: pl.BlockSpec((1, w), lambda b, t: (0, 0))
    full = lambda a: pl.BlockSpec(a.shape, lambda b, t: (0, 0))
    tm = pl.BlockSpec((1, ROW_TILE, D_HEADS), lambda b, t: (b, t, 0))
    tab = pl.BlockSpec((ROW_TILE, D_HEADS), lambda b, t: (t, 0))
    pe = pl.BlockSpec((1, ROW_TILE // CHUNK, 8, D_HEADS), lambda b, t: (b, t, 0, 0))
    tm_shape = jax.ShapeDtypeStruct((B, T, D_HEADS), BF16)
    pe_shape = jax.ShapeDtypeStruct((B, T // CHUNK, 8, D_HEADS), F32)
    dir_specs = [tm] * 6 + [pe]
    dir_shapes = [tm_shape] * 6 + [pe_shape]
    return pl.pallas_call(
        functools.partial(_rwprep_kernel, n_ctx_tiles=n_ctx_tiles, n_tiles=nt),
        grid=(B, nt),
        in_specs=[
            pl.BlockSpec((1, ROW_TILE, D_RWP), lambda b, t: (b, t, 0)),
            pl.BlockSpec((1, 8, D_RWP), lambda b, t: (b, jnp.maximum(t * nh - 1, 0), 0)),
            pl.BlockSpec((1, 8, D_RWP), lambda b, t: (b, jnp.minimum((t + 1) * nh, T // 8 - 1), 0)),
            row(D_RWP), row(D_RWP), tab, tab, tab,
            row(D_HEADS), row(D_HEADS), row(D_HEADS), row(2 * D_HEADS), row(2 * D_HEADS),
            full(consts["w2p"]), full(consts["a2p"]), full(consts["g2p"]), full(consts["ones_bd"]),
        ],
        out_specs=dir_specs + dir_specs + [
            pl.BlockSpec((1, D_HEADS, ROW_TILE), lambda b, t: (b, 0, t)),
            tm, tm,
        ],
        out_shape=dir_shapes + dir_shapes + [
            jax.ShapeDtypeStruct((B, D_HEADS, T), BF16),
            jax.ShapeDtypeStruct((B, T, D_HEADS), F32),
            jax.ShapeDtypeStruct((B, T, D_HEADS), F32),
        ],
        compiler_params=_cparams(("arbitrary", "arbitrary")),
    )(p_rw, p_rw, p_rw, consts["mu_prev"], consts["mu_next"], consts["cos"], consts["sa"], consts["sb"],
      consts["k_k"], consts["k_a"], consts["r_k"], consts["w0"], consts["a0"],
      consts["w2p"], consts["a2p"], consts["g2p"], consts["ones_bd"])


def _bd(y, m0):
    zero = jnp.zeros_like(y)
    return jnp.concatenate([jnp.where(m0, y, zero), jnp.where(m0, zero, y)], axis=0)


def _sel(w, m0):
    return jnp.where(m0, w[0:CHUNK], w[CHUNK:2 * CHUNK])


def _wkv_kernel(*refs):
    (at_f, bt_f, kt_f, rt_f, bh_f, kh_f, pe_f, vt_f,
     at_b, bt_b, kt_b, rt_b, bh_b, kh_b, pe_b, vt_b,
     yf_ref, yb_ref, s_ref) = refs
    s = pl.program_id(1)

    @pl.when(s == 0)
    def _():
        s_ref[...] = jnp.zeros_like(s_ref)

    lane = lax.broadcasted_iota(jnp.int32, (CHUNK, 2 * CHUNK), 1)
    rowi = lax.broadcasted_iota(jnp.int32, (CHUNK, 2 * CHUNK), 0)
    lm = jnp.bitwise_and(lane, CHUNK - 1)
    m0 = lane < CHUNK
    lane2 = lax.broadcasted_iota(jnp.int32, (2 * CHUNK, 2 * CHUNK), 1)
    dirs = ((at_f, bt_f, kt_f, rt_f, bh_f, kh_f, pe_f, vt_f, yf_ref),
            (at_b, bt_b, kt_b, rt_b, bh_b, kh_b, pe_b, vt_b, yb_ref))
    masks = (((rowi < lm), (rowi <= lm)), ((rowi > lm), (rowi >= lm)))
    zero = jnp.zeros((CHUNK, 2 * CHUNK), F32)
    cat2 = lambda y: jnp.concatenate([y, y], axis=0)
    bd = lambda y: _bd(y, m0)
    bf = lambda y: y.astype(BF16)

    probs = []
    for rnd in range(2):
        for d in range(2):
            for p in range(N_PAIRS):
                half = rnd if d == 0 else 1 - rnd
                probs.append(dict(d=d, p=p, half=half, rnd=rnd,
                                  rs=slice(half * CHUNK, (half + 1) * CHUNK), ls=slice(p * 128, (p + 1) * 128)))

    def ld(pr, i):
        return dirs[pr["d"]][i][0, pr["rs"], pr["ls"]]

    def rhs1(pr):
        return jnp.concatenate([bd(ld(pr, 0)), bd(ld(pr, 3))], axis=0)

    for pr in probs:
        G = _dot_nt(jnp.concatenate([ld(pr, 1), ld(pr, 2)], axis=0), rhs1(pr))
        strict, incl = masks[pr["d"]]
        pr["N"] = jnp.where(strict, G[0:CHUNK, 0:128], zero)
        pr["N_br"] = bf(jnp.where(incl, G[0:CHUNK, 128:256], zero))
        pr["A_ak"] = bf(jnp.where(strict, G[CHUNK:128, 0:128], zero))
        pr["N_kr"] = bf(jnp.where(incl, G[CHUNK:128, 128:256], zero))
    for pr in probs:
        Ab = bf(pr["N"])
        pr["M"] = _dot(Ab, bd(Ab))
    for _ in range(4):
        for pr in probs:
            Mb = bf(pr["M"])
            Rm = _dot(jnp.concatenate([bf(pr["N"]), Mb], axis=0), bd(Mb))
            pr["N"] = pr["N"] + pr["M"] + Rm[0:CHUNK]
            pr["M"] = Rm[CHUNK:2 * CHUNK]
    for pr in probs:
        pr["N"] = bf(pr["N"] + pr["M"] + _dot(bf(pr["N"]), bd(bf(pr["M"]))))
        del pr["M"]
    for pr in probs:
        vt_p = dirs[pr["d"]][7][0, pr["ls"], :]
        in_half = (lane2 < CHUNK) if pr["half"] == 0 else (lane2 >= CHUNK)
        vtm = jnp.where(in_half, vt_p, jnp.zeros_like(vt_p))
        VG = _dot(vtm, jnp.concatenate([cat2(pr["A_ak"]), cat2(pr["N_kr"]), cat2(ld(pr, 5))], axis=1))
        pr["VA"] = _sel(VG[:, 0:128], m0)
        pr["VN"] = _sel(VG[:, 128:256], m0)
        pr["VK"] = _sel(VG[:, 256:384], m0)

    S = {(d, p): s_ref[d, p] for d in range(2) for p in range(N_PAIRS)}
    ys = {}
    for rnd in range(2):
        cur = [pr for pr in probs if pr["rnd"] == rnd]
        for pr in cur:
            St = S[(pr["d"], pr["p"])]
            SG = _dot_nt(bf(St), rhs1(pr))
            pr["X"] = SG[:, 0:128] + pr["VA"]
            pr["Y"] = SG[:, 128:256] + pr["VN"]
        for pr in cur:
            pr["U"] = bf(pr["X"] + _dot(bf(pr["X"]), bd(pr["N"])))
        for pr in cur:
            UG = _dot(pr["U"], jnp.concatenate([bd(pr["N_br"]), bd(ld(pr, 4))], axis=1))
            key = (pr["d"], pr["p"])
            pend = dirs[pr["d"]][6][0, pr["half"], 0:1, pr["ls"]]
            S[key] = S[key] * pend + UG[:, 128:256] + pr["VK"]
            ys[(pr["d"], pr["p"], pr["half"])] = pr["Y"] + UG[:, 0:128]
    for d in range(2):
        y_ref = dirs[d][8]
        for p in range(N_PAIRS):
            s_ref[d, p] = S[(d, p)]
            y0, y1 = ys[(d, p, 0)], ys[(d, p, 1)]
            y_ref[0, p * 128:p * 128 + CHUNK, :] = jnp.where(m0, y0, pltpu.roll(y1, CHUNK, axis=1))
            y_ref[0, p * 128 + CHUNK:(p + 1) * 128, :] = jnp.where(m0, pltpu.roll(y0, CHUNK, axis=1), y1)


def _wkv(prep, n_ctx):
    (at_f, bt_f, kt_f, rt_f, bh_f, kh_f, pe_f, at_b, bt_b, kt_b, rt_b, bh_b, kh_b, pe_b, vt) = prep
    B, T, _ = at_f.shape
    ns = T // (2 * CHUNK)
    nc2 = n_ctx // (2 * CHUNK)

    def mrev(s):
        return jnp.where(s < nc2, nc2 - 1 - s, ns - 1 - (s - nc2))

    def specs(idx):
        tm = pl.BlockSpec((1, 2 * CHUNK, D_HEADS), lambda b, s: (b, idx(s), 0))
        pe = pl.BlockSpec((1, 2, 8, D_HEADS), lambda b, s: (b, idx(s), 0, 0))
        vts = pl.BlockSpec((1, D_HEADS, 2 * CHUNK), lambda b, s: (b, 0, idx(s)))
        return [tm] * 6 + [pe, vts]

    fwd = lambda s: s
    yt = lambda idx: pl.BlockSpec((1, D_HEADS, 2 * CHUNK), lambda b, s: (b, 0, idx(s)))
    return pl.pallas_call(
        _wkv_kernel,
        grid=(B, ns),
        in_specs=specs(fwd) + specs(mrev),
        out_specs=[yt(fwd), yt(mrev)],
        out_shape=[jax.ShapeDtypeStruct((B, D_HEADS, T), F32)] * 2,
        scratch_shapes=[pltpu.VMEM((2, N_PAIRS, CHUNK, 2 * CHUNK), F32)],
        compiler_params=_cparams(("arbitrary", "arbitrary")),
    )(at_f, bt_f, kt_f, rt_f, bh_f, kh_f, pe_f, vt, at_b, bt_b, kt_b, rt_b, bh_b, kh_b, pe_b, vt)


def _rwpost_kernel(yf_ref, yb_ref, bonus_ref, g_ref, gg_ref, gb_ref, o_ref):
    y = yf_ref[0] + yb_ref[0]
    R = y.shape[1]
    y3 = y.reshape(N_HEADS, HEAD_DIM, R)
    mu = jnp.mean(y3, axis=1, keepdims=True)
    var = jnp.mean(jnp.square(y3 - mu), axis=1, keepdims=True)
    yn = ((y3 - mu) * lax.rsqrt(var + GN_EPS)).reshape(D_HEADS, R)
    out = (yn.T * gg_ref[...] + gb_ref[...] + bonus_ref[0]) * g_ref[0]
    o_ref[0] = out.astype(BF16)


def _rwpost(yf, yb, bonus, g, gn_g, gn_b):
    B, _, T = yf.shape
    nt = T // ROW_TILE
    ytile = pl.BlockSpec((1, D_HEADS, ROW_TILE), lambda b, t: (b, 0, t))
    tm = pl.BlockSpec((1, ROW_TILE, D_HEADS), lambda b, t: (b, t, 0))
    row = pl.BlockSpec((1, D_HEADS), lambda b, t: (0, 0))
    return pl.pallas_call(
        _rwpost_kernel,
        grid=(B, nt),
        in_specs=[ytile, ytile, tm, tm, row, row],
        out_specs=tm,
        out_shape=jax.ShapeDtypeStruct((B, T, D_HEADS), BF16),
        compiler_params=_cparams(("arbitrary", "arbitrary")),
    )(yf, yb, bonus, g, gn_g, gn_b)


def _attn_kernel(q_ref, k_ref, v_ref, bias_ref, o_ref, *, n_ctx, n_rows):
    j = pl.program_id(1)
    n_cblk = n_ctx // GRID_W
    lane = lax.broadcasted_iota(jnp.int32, (GRID_W, 128), 1)
    m0 = lane < HEAD_DIM
    win = WIN_ROWS * GRID_W

    def stacked_q(p):
        qp = q_ref[0, :, p * 128:(p + 1) * 128]
        return _bd(qp, m0)

    pairs = [slice(p * 128, (p + 1) * 128) for p in range(N_PAIRS)]
    rmax = lambda a: jnp.max(a, axis=-1, keepdims=True)
    rsum = lambda a: jnp.sum(a, axis=-1, keepdims=True)

    @pl.when(j < n_cblk)
    def _():
        sc = [_dot_nt(stacked_q(p), k_ref[0, 0:n_ctx, ls]) for p, ls in enumerate(pairs)]
        mx = [rmax(a) for a in sc]
        ex = [jnp.exp(a - m) for a, m in zip(sc, mx)]
        den = [rsum(e) for e in ex]
        for p, ls in enumerate(pairs):
            o = _dot(ex[p].astype(BF16), v_ref[0, 0:n_ctx, ls]) / den[p]
            o_ref[0, :, ls] = _sel(o, m0).astype(BF16)

    @pl.when(j >= n_cblk)
    def _():
        i = j - n_cblk
        rs = jnp.clip(i - WIN_ROWS // 2, 0, n_rows - WIN_ROWS)
        start = pl.multiple_of(n_ctx + rs * GRID_W, GRID_W)
        qs = [stacked_q(p) for p in range(N_PAIRS)]
        s_loc = [_dot_nt(qs[p], k_ref[0, pl.ds(start, win), ls]) + bias_ref[0, p] for p, ls in enumerate(pairs)]
        s_ctx = [_dot_nt(qs[p], k_ref[0, 0:n_ctx, ls]) for p, ls in enumerate(pairs)]
        mx = [jnp.maximum(rmax(a), rmax(b)) for a, b in zip(s_loc, s_ctx)]
        e_loc = [jnp.exp(a - m) for a, m in zip(s_loc, mx)]
        e_ctx = [jnp.exp(a - m) for a, m in zip(s_ctx, mx)]
        den = [rsum(a) + rsum(b) for a, b in zip(e_loc, e_ctx)]
        for p, ls in enumerate(pairs):
            o = _dot(e_loc[p].astype(BF16), v_ref[0, pl.ds(start, win), ls])
            o = (o + _dot(e_ctx[p].astype(BF16), v_ref[0, 0:n_ctx, ls])) / den[p]
            o_ref[0, :, ls] = _sel(o, m0).astype(BF16)


def _attention(qkv, bias_tab, n_ctx):
    B, T, _ = qkv.shape
    n_rows = (T - n_ctx) // GRID_W
    n_cblk = n_ctx // GRID_W
    half = WIN_ROWS // 2

    def delta(j):
        i = jnp.maximum(j - n_cblk, 0)
        return jnp.minimum(i, half) + jnp.maximum(i - (n_rows - half), 0)

    return pl.pallas_call(
        functools.partial(_attn_kernel, n_ctx=n_ctx, n_rows=n_rows),
        grid=(B, T // GRID_W),
        in_specs=[
            pl.BlockSpec((1, GRID_W, D_HEADS), lambda b, j: (b, j, 0)),
            pl.BlockSpec((1, T, D_HEADS), lambda b, j: (b, 0, 1)),
            pl.BlockSpec((1, T, D_HEADS), lambda b, j: (b, 0, 2)),
            pl.BlockSpec((1, N_PAIRS, 128, WIN_ROWS * GRID_W), lambda b, j: (delta(j), 0, 0, 0)),
        ],
        out_specs=pl.BlockSpec((1, GRID_W, D_HEADS), lambda b, j: (b, j, 0)),
        out_shape=jax.ShapeDtypeStruct((B, T, D_HEADS), BF16),
        compiler_params=_cparams(("arbitrary", "arbitrary")),
    )(qkv, qkv, qkv, bias_tab)


def _na_bias_table(rpb):
    H = rpb.shape[0]
    dl = np.arange(WIN_ROWS)[:, None]
    r = np.arange(WIN_ROWS)[None, :]
    ridx = r - dl + (WIN_ROWS - 1)
    c = np.arange(GRID_W)[:, None]
    kc = np.arange(GRID_W)[None, :]
    cs = np.clip(c - WIN_COLS // 2, 0, GRID_W - WIN_COLS)
    valid = (kc >= cs) & (kc < cs + WIN_COLS)
    cidx = np.clip(kc - c + (WIN_COLS - 1), 0, 2 * WIN_COLS - 2)
    tab = rpb[:, ridx]
    tab = tab[:, :, :, cidx]
    tab = jnp.where(valid[None, None, None], tab, NEG)
    tab = tab.transpose(1, 0, 3, 2, 4).reshape(WIN_ROWS, H, GRID_W, WIN_ROWS * GRID_W)
    return tab.reshape(WIN_ROWS, H // 2, 2 * GRID_W, WIN_ROWS * GRID_W).astype(F32)


def _layer_norm(h, g, b):
    mu = jnp.mean(h, axis=-1, keepdims=True)
    var = jnp.mean(jnp.square(h - mu), axis=-1, keepdims=True)
    return (h - mu) * lax.rsqrt(var + LN_EPS) * g + b


def _to_token_tiles(ref, val):
    n = val.shape[0]
    for j in range(TOK_SUB):
        ref[pl.ds(j, n, stride=TOK_SUB), :] = val[:, j * 128:(j + 1) * 128]


def _from_token_tiles(ref, n):
    return [ref[pl.ds(j, n, stride=TOK_SUB), :] for j in range(TOK_SUB)]


def _outproj_kernel(na_ref, rw_ref, w_ref, zc_ref, zx_ref, mod_ref, g_ref, b_ref, rw_w_ref,
                    z1_ref, hx_ref, lg_ref, *, d, alpha, n_ctx_tiles, t0):
    o = _dot(na_ref[0], w_ref[0:D_HEADS, :]) + _dot(rw_ref[0], w_ref[D_HEADS:2 * D_HEADS, :])
    mod = mod_ref[0, 0]
    gate = mod[:, 2 * d:3 * d]
    z = _z_tile(zc_ref, zx_ref, n_ctx_tiles, t0)
    z1 = _layer_norm(alpha * z + gate * o, g_ref[...], b_ref[...])
    z1_ref[0] = z1
    hx = z1 * (1.0 + mod[:, 4 * d:5 * d]) + mod[:, 3 * d:4 * d]
    _to_token_tiles(hx_ref, hx)
    lg_ref[0] = _dot3(hx, rw_w_ref[...])


def _outproj(na, rw, w_out_b, zc, zx, sub, modsel, ln_g, ln_b, router_wp, n_ctx_tiles, t0, alpha):
    B, T, _ = na.shape
    D = zc.shape[2]
    nt = T // ROW_TILE - t0
    half = pl.BlockSpec((1, ROW_TILE, D_HEADS), lambda b, t: (b, t + t0, 0))
    tile = pl.BlockSpec((1, ROW_TILE, D), lambda b, t: (b, t, 0))
    row = pl.BlockSpec((1, D), lambda b, t: (0, 0))
    cspec, xspec = _z_specs(D, n_ctx_tiles, sub, t0)
    return pl.pallas_call(
        functools.partial(_outproj_kernel, d=D, alpha=alpha, n_ctx_tiles=n_ctx_tiles, t0=t0),
        grid=(B, nt),
        in_specs=[
            half, half,
            pl.BlockSpec((2 * D_HEADS, D), lambda b, t: (0, 0)),
            cspec, xspec,
            _mod_spec(D, n_ctx_tiles, t0),
            row, row,
            pl.BlockSpec((D, 128), lambda b, t: (0, 0)),
        ],
        out_specs=[
            tile,
            pl.BlockSpec((ROW_TILE * TOK_SUB, 128), lambda b, t: (b * nt + t, 0)),
            pl.BlockSpec((1, ROW_TILE, 128), lambda b, t: (b, t, 0)),
        ],
        out_shape=[
            jax.ShapeDtypeStruct((B, nt * ROW_TILE, D), F32),
            jax.ShapeDtypeStruct((B * nt * ROW_TILE * TOK_SUB, 128), F32),
            jax.ShapeDtypeStruct((B, nt * ROW_TILE, 128), F32),
        ],
        compiler_params=_cparams(("arbitrary", "arbitrary")),
    )(na, rw, w_out_b, zc, zx, modsel, ln_g, ln_b, router_wp)


def _tok(ref, i):
    return ref.at[pl.ds(pl.multiple_of(i * TOK_SUB, TOK_SUB), TOK_SUB)]


def _dispatch_kernel(dest_ref, hx_ref, xs_in, xs_out, sem):
    del xs_in
    n = dest_ref.shape[0]
    base = pl.program_id(0) * n

    def start(r, c):
        pltpu.make_async_copy(_tok(hx_ref, base + r), _tok(xs_out, dest_ref[r]), sem).start()
        return c

    lax.fori_loop(0, n, start, 0, unroll=8)

    def wait(r, c):
        pltpu.make_async_copy(_tok(hx_ref, 0), _tok(xs_out, 0), sem).wait()
        return c

    lax.fori_loop(0, n, wait, 0, unroll=8)


def _dispatch(dest, hx_tiles, n_rows_pad):
    N = dest.shape[0]
    xs0 = jnp.zeros((n_rows_pad * TOK_SUB, 128), F32)
    return pl.pallas_call(
        _dispatch_kernel,
        grid=(N // DISP_TILE,),
        in_specs=[
            pl.BlockSpec((DISP_TILE,), lambda i: (i,), memory_space=pltpu.SMEM),
            pl.BlockSpec(memory_space=pl.ANY),
            pl.BlockSpec(memory_space=pl.ANY),
        ],
        out_specs=pl.BlockSpec(memory_space=pl.ANY),
        out_shape=jax.ShapeDtypeStruct((n_rows_pad * TOK_SUB, 128), F32),
        scratch_shapes=[pltpu.SemaphoreType.DMA(())],
        input_output_aliases={2: 0},
        compiler_params=_cparams(("arbitrary",)),
    )(dest, hx_tiles, xs0)


def _pack_bf16_pair(a, b):
    ha = lax.bitcast_convert_type(a.astype(BF16).astype(F32), jnp.uint32)
    hb = lax.bitcast_convert_type(b.astype(BF16).astype(F32), jnp.uint32)
    return jnp.bitwise_or(ha, jnp.right_shift(hb, jnp.uint32(16)))


def _unpack_bf16_pair(w):
    a = lax.bitcast_convert_type(jnp.bitwise_and(w, jnp.uint32(0xFFFF0000)), F32)
    b = lax.bitcast_convert_type(jnp.left_shift(w, jnp.uint32(16)), F32)
    return a, b


def _expert_kernel(ea_ref, eb_ref, valid_ref, xs_ref, w1a, w3a, w2a, w1b, w3b, w2b, ys_ref,
                   c1a, c3a, c2a, c1b, c3b, c2b):
    i = pl.program_id(0)
    prev = jnp.maximum(i - 1, 0)
    changed = jnp.logical_or(i == 0, jnp.logical_or(ea_ref[i] != ea_ref[prev], eb_ref[i] != eb_ref[prev]))

    @pl.when(changed)
    def _():
        c1a[...] = w1a[0].astype(BF16)
        c3a[...] = w3a[0].astype(BF16)
        c2a[...] = w2a[0].astype(BF16)
        c1b[...] = w1b[0].astype(BF16)
        c3b[...] = w3b[0].astype(BF16)
        c2b[...] = w2b[0].astype(BF16)

    @pl.when(valid_ref[i] != 0)
    def _():
        x = jnp.concatenate(_from_token_tiles(xs_ref, MOE_TILE), axis=1).astype(BF16)

        def ffn(c1, c3, c2):
            h1 = _dot(x, c1[...])
            h3 = _dot(x, c3[...])
            h = (h1 * jax.nn.sigmoid(h1)) * h3
            return _dot(h.astype(BF16), c2[...])

        _to_token_tiles(ys_ref, _pack_bf16_pair(ffn(c1a, c3a, c2a), ffn(c1b, c3b, c2b)))

    @pl.when(valid_ref[i] == 0)
    def _():
        ys_ref[...] = jnp.zeros_like(ys_ref)


def _experts(blk_ea, blk_eb, blk_valid, xs, w1, w3, w2):
    nb = xs.shape[0] // (MOE_TILE * TOK_SUB)
    _, D, DE = w1.shape
    wa = lambda shape: pl.BlockSpec(shape, lambda i, ea, eb, va: (ea[i], 0, 0))
    wb = lambda shape: pl.BlockSpec(shape, lambda i, ea, eb, va: (eb[i], 0, 0))
    tok = pl.BlockSpec((MOE_TILE * TOK_SUB, 128), lambda i, ea, eb, va: (i, 0))
    grid_spec = pltpu.PrefetchScalarGridSpec(
        num_scalar_prefetch=3,
        grid=(nb,),
        in_specs=[
            tok,
            wa((1, D, DE)), wa((1, D, DE)), wa((1, DE, D)),
            wb((1, D, DE)), wb((1, D, DE)), wb((1, DE, D)),
        ],
        out_specs=tok,
        scratch_shapes=[pltpu.VMEM((D, DE), BF16), pltpu.VMEM((D, DE), BF16), pltpu.VMEM((DE, D), BF16),
                        pltpu.VMEM((D, DE), BF16), pltpu.VMEM((D, DE), BF16), pltpu.VMEM((DE, D), BF16)],
    )
    return pl.pallas_call(
        _expert_kernel,
        grid_spec=grid_spec,
        out_shape=jax.ShapeDtypeStruct(xs.shape, jnp.uint32),
        compiler_params=_cparams(("arbitrary",)),
    )(blk_ea, blk_eb, blk_valid, xs, w1, w3, w2, w1, w3, w2)


def _combine_kernel(dest_ref, ys_ref, gate_ref, z1_ref, mod_ref, g_ref, b_ref, o_ref, buf, sem, *, d, alpha):
    n = dest_ref.shape[0]

    def start(r, c):
        pltpu.make_async_copy(_tok(ys_ref, dest_ref[r]), _tok(buf, r), sem).start()
        return c

    lax.fori_loop(0, n, start, 0, unroll=8)

    def wait(r, c):
        pltpu.make_async_copy(_tok(ys_ref, 0), _tok(buf, 0), sem).wait()
        return c

    lax.fori_loop(0, n, wait, 0, unroll=8)
    ga = gate_ref[:, 0:1]
    gb = gate_ref[:, 1:2]
    parts = []
    for w in _from_token_tiles(buf, n):
        fa, fb = _unpack_bf16_pair(w)
        parts.append(ga * fa + gb * fb)
    y = jnp.concatenate(parts, axis=1)
    gate = mod_ref[0, 0][:, 5 * d:6 * d]
    o_ref[0] = _layer_norm(alpha * z1_ref[0] + gate * y, g_ref[...], b_ref[...])


def _combine(dest, ys, gates, z1, modsel, ln_g, ln_b, n_ctx_tiles, t0, t_out, alpha):
    B, T1, D = z1.shape
    nt1 = T1 // ROW_TILE
    skip = t_out - t0
    nt = nt1 - skip
    tile = lambda off: pl.BlockSpec((1, ROW_TILE, D), lambda b, t: (b, t + off, 0))
    row = pl.BlockSpec((1, D), lambda b, t: (0, 0))
    return pl.pallas_call(
        functools.partial(_combine_kernel, d=D, alpha=alpha),
        grid=(B, nt),
        in_specs=[
            pl.BlockSpec((ROW_TILE,), lambda b, t: (b * nt1 + t + skip,), memory_space=pltpu.SMEM),
            pl.BlockSpec(memory_space=pl.ANY),
            pl.BlockSpec((ROW_TILE, 128), lambda b, t: (b * nt1 + t + skip, 0)),
            tile(skip),
            _mod_spec(D, n_ctx_tiles, t_out),
            row, row,
        ],
        out_specs=tile(0),
        out_shape=jax.ShapeDtypeStruct((B, nt * ROW_TILE, D), F32),
        scratch_shapes=[pltpu.VMEM((ROW_TILE * TOK_SUB, 128), jnp.uint32), pltpu.SemaphoreType.DMA(())],
        compiler_params=_cparams(("arbitrary", "arbitrary")),
    )(dest, ys, gates, z1, modsel, ln_g, ln_b)


_PAIR_LO = np.array([0, 0, 0, 1, 1, 2], np.int32)
_PAIR_HI = np.array([1, 2, 3, 2, 3, 3], np.int32)


def _route(logits, router_bias):
    N = logits.shape[0]
    scores = jax.nn.sigmoid(logits[:, :N_EXPERTS])
    sel = (scores + router_bias.astype(F32)).reshape(N, N_GROUPS, EXPERTS_PER_GROUP)
    pos = jnp.arange(EXPERTS_PER_GROUP, dtype=jnp.int32)
    a1 = jnp.argmax(sel, axis=-1)
    m1 = jnp.max(sel, axis=-1)
    rest = jnp.where(pos == a1[..., None], -jnp.inf, sel)
    m2 = jnp.max(rest, axis=-1)
    g_idx = jnp.argmax(m1 + m2, axis=-1).astype(jnp.int32)
    sel_g = jnp.take_along_axis(sel, g_idx[:, None, None], axis=1)[:, 0]
    i1 = jnp.argmax(sel_g, axis=-1).astype(jnp.int32)
    i2 = jnp.argmax(jnp.where(pos == i1[:, None], -jnp.inf, sel_g), axis=-1).astype(jnp.int32)
    sc_g = jnp.take_along_axis(scores.reshape(N, N_GROUPS, EXPERTS_PER_GROUP), g_idx[:, None, None], axis=1)[:, 0]
    lo = jnp.minimum(i1, i2)
    hi = jnp.maximum(i1, i2)
    g_lo = jnp.take_along_axis(sc_g, lo[:, None], axis=1)[:, 0]
    g_hi = jnp.take_along_axis(sc_g, hi[:, None], axis=1)[:, 0]
    tot = g_lo + g_hi
    g_lo = g_lo / tot
    g_hi = g_hi / tot
    pair = lo * 3 - (lo * (lo - 1)) // 2 + (hi - lo - 1)
    cls = g_idx * 6 + pair

    onehot = (cls[:, None] == jnp.arange(N_CLASSES, dtype=jnp.int32)[None, :])
    oh = onehot.astype(BF16).reshape(N // ROW_TILE, ROW_TILE, N_CLASSES)
    tri = jnp.tril(jnp.ones((ROW_TILE, ROW_TILE), BF16), -1)
    within = jnp.einsum("rs,tsc->trc", tri, oh, preferred_element_type=F32)
    tile_cnt = jnp.sum(oh.astype(F32), axis=1)
    tile_off = jnp.cumsum(tile_cnt, axis=0) - tile_cnt
    rank_all = (within + tile_off[:, None, :]).reshape(N, N_CLASSES)
    rank = jnp.sum(jnp.where(onehot, rank_all, 0.0), axis=1).astype(jnp.int32)
    counts = jnp.sum(tile_cnt, axis=0).astype(jnp.int32)
    padded = (counts + MOE_TILE - 1) // MOE_TILE * MOE_TILE
    cls_end = jnp.cumsum(padded)
    cls_start = cls_end - padded
    dest = cls_start[cls] + rank

    nb = N // MOE_TILE + N_CLASSES
    blk_row = jnp.arange(nb, dtype=jnp.int32) * MOE_TILE
    blk_valid = (blk_row < cls_end[-1]).astype(jnp.int32)
    blk_cls = jnp.minimum(jnp.searchsorted(cls_end, blk_row, side="right"), N_CLASSES - 1).astype(jnp.int32)
    n_used = cls_end[-1] // MOE_TILE
    last_used_cls = blk_cls[jnp.maximum(n_used - 1, 0)]
    blk_cls = jnp.where(blk_valid != 0, blk_cls, last_used_cls)
    blk_grp = blk_cls // 6
    blk_pair = blk_cls % 6
    blk_ea = blk_grp * EXPERTS_PER_GROUP + jnp.asarray(_PAIR_LO)[blk_pair]
    blk_eb = blk_grp * EXPERTS_PER_GROUP + jnp.asarray(_PAIR_HI)[blk_pair]
    gates = jnp.zeros((N, 128), F32).at[:, 0].set(g_lo).at[:, 1].set(g_hi)
    return dest.astype(jnp.int32), gates, blk_ea.astype(jnp.int32), blk_eb.astype(jnp.int32), blk_valid, nb * MOE_TILE


def _rope_tables(n_ctx, seq):
    t = np.arange(seq)
    row = (t // GRID_W).astype(np.float32)
    col = (t % GRID_W).astype(np.float32)
    n_freq = HEAD_DIM // 4
    inv = jnp.asarray(ROPE_BASE, F32) ** (-jnp.arange(n_freq, dtype=F32) / n_freq)
    ar = jnp.asarray(row)[:, None] * inv
    ac = jnp.asarray(col)[:, None] * inv
    ang = jnp.concatenate([ar, ar, ac, ac], -1)
    cos = jnp.cos(ang)
    sin = jnp.sin(ang)
    quarter = (np.arange(HEAD_DIM) // n_freq) % 2
    sa = jnp.where(quarter == 0, -sin, 0.0)
    sb = jnp.where(quarter == 1, sin, 0.0)
    pad = lambda a, fill: jnp.concatenate([jnp.full((n_ctx, HEAD_DIM), fill, F32), a], 0)
    tile = lambda a: jnp.tile(a, (1, N_HEADS))
    return tile(pad(cos, 1.0)), tile(pad(sa, 0.0)), tile(pad(sb, 0.0))


def _rw_consts(l, n_ctx, seq, rw_mu_prev, rw_mu_next, rw_w0, rw_w2, rw_a0, rw_a2, rw_g2, rw_k_k, rw_k_a, rw_r_k):
    d_rw_in = rw_mu_prev.shape[1]
    padw = lambda a: jnp.pad(a[l], (0, D_RWP - d_rw_in)).reshape(1, D_RWP)
    cos, sa, sb = _rope_tables(n_ctx, seq)
    w2p = jnp.zeros((128, 2 * D_HEADS), F32)
    w2p = w2p.at[0:LORA, 0:D_HEADS].set(rw_w2[l, 0]).at[LORA:2 * LORA, D_HEADS:].set(rw_w2[l, 1])
    a2p = jnp.zeros((128, 2 * D_HEADS), F32)
    a2p = a2p.at[2 * LORA:3 * LORA, 0:D_HEADS].set(rw_a2[l, 0]).at[3 * LORA:4 * LORA, D_HEADS:].set(rw_a2[l, 1])
    g2p = jnp.zeros((128, D_HEADS), F32).at[0:GATE_LORA].set(rw_g2[l])
    head = np.arange(D_HEADS) // HEAD_DIM
    ones_bd = jnp.asarray(head[:, None] == head[None, :], BF16)
    return dict(
        mu_prev=padw(rw_mu_prev), mu_next=padw(rw_mu_next), cos=cos, sa=sa, sb=sb,
        k_k=rw_k_k[l].reshape(1, D_HEADS), k_a=rw_k_a[l].reshape(1, D_HEADS), r_k=rw_r_k[l].reshape(1, D_HEADS),
        w0=rw_w0[l].reshape(1, 2 * D_HEADS), a0=rw_a0[l].reshape(1, 2 * D_HEADS),
        w2p=w2p, a2p=a2p, g2p=g2p, ones_bd=ones_bd,
    )


def kernel(x, c, ctx, c_ctx, ada_w, ada_b, w_in, na_rpb, rw_mu_prev, rw_mu_next, rw_w0, rw_w2, rw_a0, rw_a2, rw_g2, rw_k_k, rw_k_a, rw_r_k, rw_gn_g, rw_gn_b, w_out, ln1_g, ln1_b, ln2_g, ln2_b, router_w, router_bias, exp_w1, exp_w3, exp_w2):
    B, S, D = x.shape
    C = ctx.shape[1]
    L = ada_w.shape[0]
    T = C + S
    assert D == 1024 and C % ROW_TILE == 0 and S % ROW_TILE == 0 and C % (2 * CHUNK) == 0
    assert S % GRID_W == 0 and S // GRID_W >= WIN_ROWS and w_in.shape[2] == D_INP - 32
    n_ctx_tiles = C // ROW_TILE
    alpha = float((2 * L) ** 0.25)

    zc, zx, sub = ctx, x, n_ctx_tiles
    n_mod = (B + 1 + 7) // 8 * 8
    cc = jnp.zeros((n_mod, D), F32).at[0:B].set(c).at[B].set(c_ctx)
    mod_all = _ada(cc, ada_w, ada_b)
    router_wp = jnp.pad(router_w, ((0, 0), (0, 128 - N_EXPERTS)))

    for l in range(L):
        mod_c = jnp.broadcast_to(mod_all[l, B][None], (B, 6 * D))
        modsel = jnp.stack([mod_c, mod_all[l, 0:B]], axis=1).reshape(B, 2, 1, 6 * D)
        w_in_p = jnp.pad(w_in[l], ((0, 0), (0, D_INP - w_in.shape[2]))).astype(BF16)
        qkv, p_rw = _inproj(zc, zx, sub, T, modsel, w_in_p, n_ctx_tiles)

        consts = _rw_consts(l, C, S, rw_mu_prev, rw_mu_next, rw_w0, rw_w2, rw_a0, rw_a2, rw_g2,
                            rw_k_k, rw_k_a, rw_r_k)
        prep = _rwprep(p_rw, consts, n_ctx_tiles)
        yf, yb = _wkv(prep[0:15], C)
        rw = _rwpost(yf, yb, prep[15], prep[16], rw_gn_g[l].reshape(1, D_HEADS), rw_gn_b[l].reshape(1, D_HEADS))

        na = _attention(qkv, _na_bias_table(na_rpb[l]), C)

        t0 = n_ctx_tiles if l == L - 1 else 0
        z1, hx_tiles, logits = _outproj(na, rw, w_out[l].astype(BF16), zc, zx, sub, modsel,
                                        ln1_g[l].reshape(1, D), ln1_b[l].reshape(1, D), router_wp,
                                        n_ctx_tiles, t0, alpha)

        dest, gates, blk_ea, blk_eb, blk_valid, n_rows_pad = _route(logits.reshape(-1, 128), router_bias)
        xs = _dispatch(dest, hx_tiles, n_rows_pad)
        ys = _experts(blk_ea, blk_eb, blk_valid, xs, exp_w1[l], exp_w3[l], exp_w2[l])
        z = _combine(dest, ys, gates, z1, modsel, ln2_g[l].reshape(1, D), ln2_b[l].reshape(1, D),
                     n_ctx_tiles, t0, t0, alpha)
        zc, zx, sub = z, z, 0

    return z
```

```python
import functools
import math

import jax
import jax.numpy as jnp
import numpy as np
from jax import lax
from jax.experimental import pallas as pl
from jax.experimental.pallas import tpu as pltpu

F32 = jnp.float32
BF16 = jnp.bfloat16

HEAD_DIM = 64
N_HEADS = 8
D_HEADS = N_HEADS * HEAD_DIM
N_PAIRS = N_HEADS // 2
GRID_W = 64
WIN_ROWS = 8
WIN_COLS = 16
LORA = 32
GATE_LORA = 96
N_EXPERTS = 32
N_GROUPS = 8
EXPERTS_PER_GROUP = 4
N_CLASSES = N_GROUPS * 6
ROPE_BASE = 10000.0
LN_EPS = 1e-6
GN_EPS = 64e-5
CHUNK = 64
ROW_TILE = 256
MOE_TILE = 256
D_QKV = 3 * D_HEADS
D_RWP = 3 * D_HEADS + 256
D_INP = D_QKV + D_RWP
TOK_SUB = 8
NEG = -1e30
VMEM_LIMIT = 56 * 1024 * 1024


def _cparams(sem):
    return pltpu.CompilerParams(dimension_semantics=sem, vmem_limit_bytes=VMEM_LIMIT)


def _dot(a, b):
    return jnp.dot(a, b, preferred_element_type=F32)


def _dot_nt(a, b):
    return lax.dot_general(a, b, (((1,), (1,)), ((), ())), preferred_element_type=F32)


def _split2(a):
    hi = a.astype(BF16)
    lo = (a - hi.astype(F32)).astype(BF16)
    return hi, lo


def _split3(a):
    hi = a.astype(BF16)
    r1 = a - hi.astype(F32)
    mid = r1.astype(BF16)
    lo = (r1 - mid.astype(F32)).astype(BF16)
    return hi, mid, lo


def _dot3(a, b):
    ah, al = _split2(a)
    bh, bl = _split2(b)
    return _dot(ah, bh) + _dot(al, bh) + _dot(ah, bl)


def _dot_exact_rhs(a, b_exact):
    h, m, l = _split3(a)
    return _dot(h, b_exact) + _dot(m, b_exact) + _dot(l, b_exact)


def _dot_exact_lhs(a_exact, b):
    h, m, l = _split3(b)
    return _dot(a_exact, h) + _dot(a_exact, m) + _dot(a_exact, l)


def _ada_kernel(cc_ref, w_ref, b_ref, o_ref):
    cc = cc_ref[...]
    s = cc * jax.nn.sigmoid(cc)
    o_ref[0] = _dot3(s, w_ref[0]) + b_ref[0]


def _ada(cc, ada_w, ada_b):
    L, D, D6 = ada_w.shape
    R = cc.shape[0]
    tn = 1536
    return pl.pallas_call(
        _ada_kernel,
        grid=(L, D6 // tn),
        in_specs=[
            pl.BlockSpec((R, D), lambda l, n: (0, 0)),
            pl.BlockSpec((1, D, tn), lambda l, n: (l, 0, n)),
            pl.BlockSpec((1, 1, tn), lambda l, n: (l, 0, n)),
        ],
        out_specs=pl.BlockSpec((1, R, tn), lambda l, n: (l, 0, n)),
        out_shape=jax.ShapeDtypeStruct((L, R, D6), F32),
        compiler_params=_cparams(("arbitrary", "arbitrary")),
    )(cc, ada_w, ada_b.reshape(L, 1, D6))


def _z_specs(D, n_ctx_tiles, sub, t0=0):
    cspec = pl.BlockSpec((1, ROW_TILE, D), lambda b, t: (b, jnp.minimum(t + t0, n_ctx_tiles - 1), 0))
    xspec = pl.BlockSpec((1, ROW_TILE, D), lambda b, t: (b, jnp.maximum(t + t0, n_ctx_tiles) - sub, 0))
    return cspec, xspec


def _z_tile(zc_ref, zx_ref, n_ctx_tiles, t0=0):
    return jnp.where(pl.program_id(1) + t0 < n_ctx_tiles, zc_ref[0], zx_ref[0])


def _mod_spec(D, n_ctx_tiles, t0=0):
    return pl.BlockSpec((1, 1, 1, 6 * D), lambda b, t: (b, jnp.where(t + t0 < n_ctx_tiles, 0, 1), 0, 0))


def _inproj_kernel(zc_ref, zx_ref, mod_ref, w_ref, qkv_ref, rw_ref, *, d, n_ctx_tiles):
    z = _z_tile(zc_ref, zx_ref, n_ctx_tiles)
    mod = mod_ref[0, 0]
    shift = mod[:, 0:d]
    scale = mod[:, d:2 * d]
    h = (z * (1.0 + scale) + shift).astype(BF16)
    q = _dot(h, w_ref[:, 0:D_HEADS])
    qkv_ref[0, :, 0:D_HEADS] = (q * (HEAD_DIM ** -0.5)).astype(BF16)
    kv = _dot(h, w_ref[:, D_HEADS:D_QKV])
    qkv_ref[0, :, D_HEADS:D_QKV] = kv.astype(BF16)
    rw_ref[0] = _dot(h, w_ref[:, D_QKV:D_INP])


def _inproj(zc, zx, sub, T, modsel, w_in_p, n_ctx_tiles):
    B, _, D = zc.shape
    nt = T // ROW_TILE
    cspec, xspec = _z_specs(D, n_ctx_tiles, sub)
    return pl.pallas_call(
        functools.partial(_inproj_kernel, d=D, n_ctx_tiles=n_ctx_tiles),
        grid=(B, nt),
        in_specs=[
            cspec, xspec,
            pl.BlockSpec((1, 1, 1, 6 * D), lambda b, t: (b, jnp.where(t < n_ctx_tiles, 0, 1), 0, 0)),
            pl.BlockSpec((D, D_INP), lambda b, t: (0, 0)),
        ],
        out_specs=[
            pl.BlockSpec((1, ROW_TILE, D_QKV), lambda b, t: (b, t, 0)),
            pl.BlockSpec((1, ROW_TILE, D_RWP), lambda b, t: (b, t, 0)),
        ],
        out_shape=[
            jax.ShapeDtypeStruct((B, T, D_QKV), BF16),
            jax.ShapeDtypeStruct((B, T, D_RWP), F32),
        ],
        compiler_params=_cparams(("arbitrary", "arbitrary")),
    )(zc, zx, modsel, w_in_p)


def _rwprep_kernel(p_ref, pp_ref, pn_ref, mup_ref, mun_ref, cos_ref, sa_ref, sb_ref,
                   kk_ref, ka_ref, rk_ref, w0_ref, a0_ref, w2_ref, a2_ref, g2_ref, ones_ref,
                   at_f, bt_f, kt_f, rt_f, bh_f, kh_f, pe_f,
                   at_b, bt_b, kt_b, rt_b, bh_b, kh_b, pe_b,
                   vt_ref, bonus_ref, g_ref, *, n_ctx_tiles, n_tiles):
    t = pl.program_id(1)
    P = p_ref[0]
    R = P.shape[0]
    prev_ok = jnp.logical_and(t != 0, t != n_ctx_tiles)
    next_ok = jnp.logical_and(t != n_ctx_tiles - 1, t != n_tiles - 1)
    prev_row = jnp.where(prev_ok, pp_ref[0, 7:8, :], 0.0)
    next_row = jnp.where(next_ok, pn_ref[0, 0:1, :], 0.0)
    row = lax.broadcasted_iota(jnp.int32, (R, 1), 0)
    prev = jnp.where(row == 0, prev_row, pltpu.roll(P, 1, axis=0))
    nxt = jnp.where(row == R - 1, next_row, pltpu.roll(P, R - 1, axis=0))
    z = P + mup_ref[...] * (prev - P) + mun_ref[...] * (nxt - P)

    cos = cos_ref[...]
    sa = sa_ref[...]
    sb = sb_ref[...]

    def rope(u):
        return (u * cos + pltpu.roll(u, D_HEADS - HEAD_DIM // 4, axis=1) * sa
                + pltpu.roll(u, HEAD_DIM // 4, axis=1) * sb)

    r = rope(z[:, 0:D_HEADS])
    k = rope(z[:, D_HEADS:2 * D_HEADS])
    v = z[:, 2 * D_HEADS:3 * D_HEADS]
    ones_bd = ones_ref[...]

    def seg_sum(u):
        return _dot_exact_rhs(u, ones_bd)

    kk = k * kk_ref[...]
    kk = kk / jnp.maximum(jnp.sqrt(seg_sum(kk * kk)), 1e-12)

    slab = z[:, 3 * D_HEADS:3 * D_HEADS + 128]
    u_w = w0_ref[...] + _dot3(jnp.tanh(slab), w2_ref[...])
    u_a = a0_ref[...] + _dot3(slab, a2_ref[...])
    g_ref[0] = _dot3(jax.nn.sigmoid(z[:, 3 * D_HEADS + 128:D_RWP]), g2_ref[...]).astype(BF16)
    e_all = math.exp(-0.5) * jax.nn.sigmoid(u_w)
    a_all = jax.nn.sigmoid(u_a)

    ci = lax.broadcasted_iota(jnp.int32, (CHUNK, CHUNK), 0)
    cj = lax.broadcasted_iota(jnp.int32, (CHUNK, CHUNK), 1)
    ka = ka_ref[...]
    outs = ((at_f, bt_f, kt_f, rt_f, bh_f, kh_f, pe_f), (at_b, bt_b, kt_b, rt_b, bh_b, kh_b, pe_b))
    kd_sum = None
    for d in range(2):
        e = e_all[:, d * D_HEADS:(d + 1) * D_HEADS]
        a = a_all[:, d * D_HEADS:(d + 1) * D_HEADS]
        tri = (cj <= ci) if d == 0 else (cj >= ci)
        tri = tri.astype(BF16)
        cs, ce = [], []
        for q in range(R // CHUNK):
            cq = -_dot_exact_lhs(tri, e[q * CHUNK:(q + 1) * CHUNK])
            end = cq[CHUNK - 1:CHUNK] if d == 0 else cq[0:1]
            cs.append(cq)
            ce.append(jnp.broadcast_to(end, cq.shape))
        c = jnp.concatenate(cs, axis=0)
        cend = jnp.concatenate(ce, axis=0)
        kd = k * (1.0 + (a - 1.0) * ka)
        kd_sum = kd if kd_sum is None else kd_sum + kd
        beta = a * kk
        en = jnp.exp(-c)
        eh = jnp.exp(cend - c)
        o_at, o_bt, o_kt, o_rt, o_bh, o_kh, o_pe = outs[d]
        o_at[0] = (-kk * jnp.exp(c + e)).astype(BF16)
        o_bt[0] = (beta * en).astype(BF16)
        o_kt[0] = (kd * en).astype(BF16)
        o_rt[0] = (r * jnp.exp(c)).astype(BF16)
        o_bh[0] = (beta * eh).astype(BF16)
        o_kh[0] = (kd * eh).astype(BF16)
        for q in range(R // CHUNK):
            o_pe[0, q] = jnp.exp(cend[q * CHUNK:q * CHUNK + 8])
    bonus_ref[0] = (seg_sum(r * rk_ref[...] * kd_sum) * v).astype(BF16)
    vt_ref[0] = v.T.astype(BF16)


def _rwprep(p_rw, consts, n_ctx_tiles):
    B, T, _ = p_rw.shape
    nt = T // ROW_TILE
    nh = ROW_TILE // 8
    row = lambda w: pl.BlockSpec((1, w), lambda b, t: (0, 0))
    full = lambda a: pl.BlockSpec(a.shape, lambda b, t: (0, 0))
    tm = pl.BlockSpec((1, ROW_TILE, D_HEADS), lambda b, t: (b, t, 0))
    tab = pl.BlockSpec((ROW_TILE, D_HEADS), lambda b, t: (t, 0))
    pe = pl.BlockSpec((1, ROW_TILE // CHUNK, 8, D_HEADS), lambda b, t: (b, t, 0, 0))
    tm_shape = jax.ShapeDtypeStruct((B, T, D_HEADS), BF16)
    pe_shape = jax.ShapeDtypeStruct((B, T // CHUNK, 8, D_HEADS), F32)
    dir_specs = [tm] * 6 + [pe]
    dir_shapes = [tm_shape] * 6 + [pe_shape]
    return pl.pallas_call(
        functools.partial(_rwprep_kernel, n_ctx_tiles=n_ctx_tiles, n_tiles=nt),
        grid=(B, nt),
        in_specs=[
            pl.BlockSpec((1, ROW_TILE, D_RWP), lambda b, t: (b, t, 0)),
            pl.BlockSpec((1, 8, D_RWP), lambda b, t: (b, jnp.maximum(t * nh - 1, 0), 0)),
            pl.BlockSpec((1, 8, D_RWP), lambda b, t: (b, jnp.minimum((t + 1) * nh, T // 8 - 1), 0)),
            row(D_RWP), row(D_RWP), tab, tab, tab,
            row(D_HEADS), row(D_HEADS), row(D_HEADS), row(2 * D_HEADS), row(2 * D_HEADS),
            full(consts["w2p"]), full(consts["a2p"]), full(consts["g2p"]), full(consts["ones_bd"]),
        ],
        out_specs=dir_specs + dir_specs + [
            pl.BlockSpec((1, D_HEADS, ROW_TILE), lambda b, t: (b, 0, t)),
            tm, tm,
        ],
        out_shape=dir_shapes + dir_shapes + [
            jax.ShapeDtypeStruct((B, D_HEADS, T), BF16),
            jax.ShapeDtypeStruct((B, T, D_HEADS), BF16),
            jax.ShapeDtypeStruct((B, T, D_HEADS), BF16),
        ],
        compiler_params=_cparams(("arbitrary", "arbitrary")),
    )(p_rw, p_rw, p_rw, consts["mu_prev"], consts["mu_next"], consts["cos"], consts["sa"], consts["sb"],
      consts["k_k"], consts["k_a"], consts["r_k"], consts["w0"], consts["a0"],
      consts["w2p"], consts["a2p"], consts["g2p"], consts["ones_bd"])


def _bd(y, m0):
    zero = jnp.zeros_like(y)
    return jnp.concatenate([jnp.where(m0, y, zero), jnp.where(m0, zero, y)], axis=0)


def _sel(w, m0):
    return jnp.where(m0, w[0:CHUNK], w[CHUNK:2 * CHUNK])


def _wkv_kernel(*refs):
    (at_f, bt_f, kt_f, rt_f, bh_f, kh_f, pe_f, vt_f,
     at_b, bt_b, kt_b, rt_b, bh_b, kh_b, pe_b, vt_b,
     yf_ref, yb_ref, s_ref) = refs
    s = pl.program_id(1)

    @pl.when(s == 0)
    def _():
        s_ref[...] = jnp.zeros_like(s_ref)

    lane = lax.broadcasted_iota(jnp.int32, (CHUNK, 2 * CHUNK), 1)
    rowi = lax.broadcasted_iota(jnp.int32, (CHUNK, 2 * CHUNK), 0)
    lm = jnp.bitwise_and(lane, CHUNK - 1)
    m0 = lane < CHUNK
    lane2 = lax.broadcasted_iota(jnp.int32, (2 * CHUNK, 2 * CHUNK), 1)
    dirs = ((at_f, bt_f, kt_f, rt_f, bh_f, kh_f, pe_f, vt_f, yf_ref),
            (at_b, bt_b, kt_b, rt_b, bh_b, kh_b, pe_b, vt_b, yb_ref))
    masks = (((rowi < lm), (rowi <= lm)), ((rowi > lm), (rowi >= lm)))
    zero = jnp.zeros((CHUNK, 2 * CHUNK), F32)
    cat2 = lambda y: jnp.concatenate([y, y], axis=0)
    bd = lambda y: _bd(y, m0)
    bf = lambda y: y.astype(BF16)

    probs = []
    for rnd in range(2):
        for d in range(2):
            for p in range(N_PAIRS):
                half = rnd if d == 0 else 1 - rnd
                probs.append(dict(d=d, p=p, half=half, rnd=rnd,
                                  rs=slice(half * CHUNK, (half + 1) * CHUNK), ls=slice(p * 128, (p + 1) * 128)))

    def ld(pr, i):
        return dirs[pr["d"]][i][0, pr["rs"], pr["ls"]]

    def rhs1(pr):
        return jnp.concatenate([bd(ld(pr, 0)), bd(ld(pr, 3))], axis=0)

    for pr in probs:
        G = _dot_nt(jnp.concatenate([ld(pr, 1), ld(pr, 2)], axis=0), rhs1(pr))
        strict, incl = masks[pr["d"]]
        pr["N"] = jnp.where(strict, G[0:CHUNK, 0:128], zero)
        pr["N_br"] = bf(jnp.where(incl, G[0:CHUNK, 128:256], zero))
        pr["A_ak"] = bf(jnp.where(strict, G[CHUNK:128, 0:128], zero))
        pr["N_kr"] = bf(jnp.where(incl, G[CHUNK:128, 128:256], zero))
    for pr in probs:
        Ab = bf(pr["N"])
        pr["M"] = _dot(Ab, bd(Ab))
    for _ in range(4):
        for pr in probs:
            Mb = bf(pr["M"])
            Rm = _dot(jnp.concatenate([bf(pr["N"]), Mb], axis=0), bd(Mb))
            pr["N"] = pr["N"] + pr["M"] + Rm[0:CHUNK]
            pr["M"] = Rm[CHUNK:2 * CHUNK]
    for pr in probs:
        pr["N"] = bf(pr["N"] + pr["M"] + _dot(bf(pr["N"]), bd(bf(pr["M"]))))
        del pr["M"]
    for pr in probs:
        vt_p = dirs[pr["d"]][7][0, pr["ls"], :]
        in_half = (lane2 < CHUNK) if pr["half"] == 0 else (lane2 >= CHUNK)
        vtm = jnp.where(in_half, vt_p, jnp.zeros_like(vt_p))
        VG = _dot(vtm, jnp.concatenate([cat2(pr["A_ak"]), cat2(pr["N_kr"]), cat2(ld(pr, 5))], axis=1))
        pr["VA"] = _sel(VG[:, 0:128], m0)
        pr["VN"] = _sel(VG[:, 128:256], m0)
        pr["VK"] = _sel(VG[:, 256:384], m0)

    S = {(d, p): s_ref[d, p] for d in range(2) for p in range(N_PAIRS)}
    ys = {}
    for rnd in range(2):
        cur = [pr for pr in probs if pr["rnd"] == rnd]
        for pr in cur:
            St = S[(pr["d"], pr["p"])]
            SG = _dot_nt(bf(St), rhs1(pr))
            pr["X"] = SG[:, 0:128] + pr["VA"]
            pr["Y"] = SG[:, 128:256] + pr["VN"]
        for pr in cur:
            pr["U"] = bf(pr["X"] + _dot(bf(pr["X"]), bd(pr["N"])))
        for pr in cur:
            UG = _dot(pr["U"], jnp.concatenate([bd(pr["N_br"]), bd(ld(pr, 4))], axis=1))
            key = (pr["d"], pr["p"])
            pend = dirs[pr["d"]][6][0, pr["half"], 0:1, pr["ls"]]
            S[key] = S[key] * pend + UG[:, 128:256] + pr["VK"]
            ys[(pr["d"], pr["p"], pr["half"])] = pr["Y"] + UG[:, 0:128]
    for d in range(2):
        y_ref = dirs[d][8]
        for p in range(N_PAIRS):
            s_ref[d, p] = S[(d, p)]
            y0, y1 = ys[(d, p, 0)], ys[(d, p, 1)]
            y_ref[0, p * 128:p * 128 + CHUNK, :] = jnp.where(m0, y0, pltpu.roll(y1, CHUNK, axis=1)).astype(BF16)
            y_ref[0, p * 128 + CHUNK:(p + 1) * 128, :] = jnp.where(m0, pltpu.roll(y0, CHUNK, axis=1), y1).astype(BF16)


def _wkv(prep, n_ctx):
    (at_f, bt_f, kt_f, rt_f, bh_f, kh_f, pe_f, at_b, bt_b, kt_b, rt_b, bh_b, kh_b, pe_b, vt) = prep
    B, T, _ = at_f.shape
    ns = T // (2 * CHUNK)
    nc2 = n_ctx // (2 * CHUNK)

    def mrev(s):
        return jnp.where(s < nc2, nc2 - 1 - s, ns - 1 - (s - nc2))

    def specs(idx):
        tm = pl.BlockSpec((1, 2 * CHUNK, D_HEADS), lambda b, s: (b, idx(s), 0))
        pe = pl.BlockSpec((1, 2, 8, D_HEADS), lambda b, s: (b, idx(s), 0, 0))
        vts = pl.BlockSpec((1, D_HEADS, 2 * CHUNK), lambda b, s: (b, 0, idx(s)))
        return [tm] * 6 + [pe, vts]

    fwd = lambda s: s
    yt = lambda idx: pl.BlockSpec((1, D_HEADS, 2 * CHUNK), lambda b, s: (b, 0, idx(s)))
    return pl.pallas_call(
        _wkv_kernel,
        grid=(B, ns),
        in_specs=specs(fwd) + specs(mrev),
        out_specs=[yt(fwd), yt(mrev)],
        out_shape=[jax.ShapeDtypeStruct((B, D_HEADS, T), BF16)] * 2,
        scratch_shapes=[pltpu.VMEM((2, N_PAIRS, CHUNK, 2 * CHUNK), F32)],
        compiler_params=_cparams(("arbitrary", "arbitrary")),
    )(at_f, bt_f, kt_f, rt_f, bh_f, kh_f, pe_f, vt, at_b, bt_b, kt_b, rt_b, bh_b, kh_b, pe_b, vt)


def _rwpost_kernel(yf_ref, yb_ref, bonus_ref, g_ref, gg_ref, gb_ref, o_ref):
    y = yf_ref[0].astype(F32) + yb_ref[0].astype(F32)
    R = y.shape[1]
    y3 = y.reshape(N_HEADS, HEAD_DIM, R)
    mu = jnp.mean(y3, axis=1, keepdims=True)
    var = jnp.mean(jnp.square(y3 - mu), axis=1, keepdims=True)
    yn = ((y3 - mu) * lax.rsqrt(var + GN_EPS)).reshape(D_HEADS, R)
    out = (yn.T * gg_ref[...] + gb_ref[...] + bonus_ref[0].astype(F32)) * g_ref[0].astype(F32)
    o_ref[0] = out.astype(BF16)


def _rwpost(yf, yb, bonus, g, gn_g, gn_b):
    B, _, T = yf.shape
    nt = T // ROW_TILE
    ytile = pl.BlockSpec((1, D_HEADS, ROW_TILE), lambda b, t: (b, 0, t))
    tm = pl.BlockSpec((1, ROW_TILE, D_HEADS), lambda b, t: (b, t, 0))
    row = pl.BlockSpec((1, D_HEADS), lambda b, t: (0, 0))
    return pl.pallas_call(
        _rwpost_kernel,
        grid=(B, nt),
        in_specs=[ytile, ytile, tm, tm, row, row],
        out_specs=tm,
        out_shape=jax.ShapeDtypeStruct((B, T, D_HEADS), BF16),
        compiler_params=_cparams(("arbitrary", "arbitrary")),
    )(yf, yb, bonus, g, gn_g, gn_b)


def _attn_kernel(q_ref, k_ref, v_ref, bias_ref, o_ref, *, n_ctx, n_rows):
    j = pl.program_id(1)
    n_cblk = n_ctx // GRID_W
    lane = lax.broadcasted_iota(jnp.int32, (GRID_W, 128), 1)
    m0 = lane < HEAD_DIM
    win = WIN_ROWS * GRID_W

    def stacked_q(p):
        qp = q_ref[0, :, p * 128:(p + 1) * 128]
        return _bd(qp, m0)

    pairs = [slice(p * 128, (p + 1) * 128) for p in range(N_PAIRS)]
    rmax = lambda a: jnp.max(a, axis=-1, keepdims=True)
    rsum = lambda a: jnp.sum(a, axis=-1, keepdims=True)

    @pl.when(j < n_cblk)
    def _():
        sc = [_dot_nt(stacked_q(p), k_ref[0, 0:n_ctx, ls]) for p, ls in enumerate(pairs)]
        mx = [rmax(a) for a in sc]
        ex = [jnp.exp(a - m) for a, m in zip(sc, mx)]
        den = [rsum(e) for e in ex]
        for p, ls in enumerate(pairs):
            o = _dot(ex[p].astype(BF16), v_ref[0, 0:n_ctx, ls]) / den[p]
            o_ref[0, :, ls] = _sel(o, m0).astype(BF16)

    @pl.when(j >= n_cblk)
    def _():
        i = j - n_cblk
        rs = jnp.clip(i - WIN_ROWS // 2, 0, n_rows - WIN_ROWS)
        start = pl.multiple_of(n_ctx + rs * GRID_W, GRID_W)
        qs = [stacked_q(p) for p in range(N_PAIRS)]
        s_loc = [_dot_nt(qs[p], k_ref[0, pl.ds(start, win), ls]) + bias_ref[0, p] for p, ls in enumerate(pairs)]
        s_ctx = [_dot_nt(qs[p], k_ref[0, 0:n_ctx, ls]) for p, ls in enumerate(pairs)]
        mx = [jnp.maximum(rmax(a), rmax(b)) for a, b in zip(s_loc, s_ctx)]
        e_loc = [jnp.exp(a - m) for a, m in zip(s_loc, mx)]
        e_ctx = [jnp.exp(a - m) for a, m in zip(s_ctx, mx)]
        den = [rsum(a) + rsum(b) for a, b in zip(e_loc, e_ctx)]
        for p, ls in enumerate(pairs):
            o = _dot(e_loc[p].astype(BF16), v_ref[0, pl.ds(start, win), ls])
            o = (o + _dot(e_ctx[p].astype(BF16), v_ref[0, 0:n_ctx, ls])) / den[p]
            o_ref[0, :, ls] = _sel(o, m0).astype(BF16)


def _attention(qkv, bias_tab, n_ctx):
    B, T, _ = qkv.shape
    n_rows = (T - n_ctx) // GRID_W
    n_cblk = n_ctx // GRID_W
    half = WIN_ROWS // 2

    def delta(j):
        i = jnp.maximum(j - n_cblk, 0)
        return jnp.minimum(i, half) + jnp.maximum(i - (n_rows - half), 0)

    return pl.pallas_call(
        functools.partial(_attn_kernel, n_ctx=n_ctx, n_rows=n_rows),
        grid=(B, T // GRID_W),
        in_specs=[
            pl.BlockSpec((1, GRID_W, D_HEADS), lambda b, j: (b, j, 0)),
            pl.BlockSpec((1, T, D_HEADS), lambda b, j: (b, 0, 1)),
            pl.BlockSpec((1, T, D_HEADS), lambda b, j: (b, 0, 2)),
            pl.BlockSpec((1, N_PAIRS, 128, WIN_ROWS * GRID_W), lambda b, j: (delta(j), 0, 0, 0)),
        ],
        out_specs=pl.BlockSpec((1, GRID_W, D_HEADS), lambda b, j: (b, j, 0)),
        out_shape=jax.ShapeDtypeStruct((B, T, D_HEADS), BF16),
        compiler_params=_cparams(("arbitrary", "arbitrary")),
    )(qkv, qkv, qkv, bias_tab)


def _na_bias_table(rpb):
    H = rpb.shape[0]
    c = np.arange(GRID_W)[:, None]
    kc = np.arange(GRID_W)[None, :]
    cs = np.clip(c - WIN_COLS // 2, 0, GRID_W - WIN_COLS)
    valid = (kc >= cs) & (kc < cs + WIN_COLS)
    cidx = np.clip(kc - c + (WIN_COLS - 1), 0, 2 * WIN_COLS - 2)
    t = jnp.where(valid[None, None], rpb[:, :, cidx], NEG)
    t = t.transpose(0, 2, 1, 3).astype(F32)
    tabs = [t[:, :, WIN_ROWS - 1 - dl:2 * WIN_ROWS - 1 - dl, :].reshape(H, GRID_W, WIN_ROWS * GRID_W)
            for dl in range(WIN_ROWS)]
    return jnp.stack(tabs, 0).reshape(WIN_ROWS, H // 2, 2 * GRID_W, WIN_ROWS * GRID_W)


def _layer_norm(h, g, b):
    mu = jnp.mean(h, axis=-1, keepdims=True)
    var = jnp.mean(jnp.square(h - mu), axis=-1, keepdims=True)
    return (h - mu) * lax.rsqrt(var + LN_EPS) * g + b


def _to_token_tiles(ref, val):
    n = val.shape[0]
    for j in range(TOK_SUB):
        ref[pl.ds(j, n, stride=TOK_SUB), :] = val[:, j * 128:(j + 1) * 128]


def _from_token_tiles(ref, n):
    return [ref[pl.ds(j, n, stride=TOK_SUB), :] for j in range(TOK_SUB)]


def _outproj_kernel(na_ref, rw_ref, w_ref, zc_ref, zx_ref, mod_ref, g_ref, b_ref, rw_w_ref,
                    z1_ref, hx_ref, lg_ref, *, d, alpha, n_ctx_tiles, t0):
    o = _dot(na_ref[0], w_ref[0:D_HEADS, :]) + _dot(rw_ref[0], w_ref[D_HEADS:2 * D_HEADS, :])
    mod = mod_ref[0, 0]
    gate = mod[:, 2 * d:3 * d]
    z = _z_tile(zc_ref, zx_ref, n_ctx_tiles, t0)
    z1 = _layer_norm(alpha * z + gate * o, g_ref[...], b_ref[...])
    z1_ref[0] = z1
    hx = z1 * (1.0 + mod[:, 4 * d:5 * d]) + mod[:, 3 * d:4 * d]
    _to_token_tiles(hx_ref, hx)
    lg_ref[0] = _dot3(hx, rw_w_ref[...])


def _outproj(na, rw, w_out_b, zc, zx, sub, modsel, ln_g, ln_b, router_wp, n_ctx_tiles, t0, alpha):
    B, T, _ = na.shape
    D = zc.shape[2]
    nt = T // ROW_TILE - t0
    half = pl.BlockSpec((1, ROW_TILE, D_HEADS), lambda b, t: (b, t + t0, 0))
    tile = pl.BlockSpec((1, ROW_TILE, D), lambda b, t: (b, t, 0))
    row = pl.BlockSpec((1, D), lambda b, t: (0, 0))
    cspec, xspec = _z_specs(D, n_ctx_tiles, sub, t0)
    return pl.pallas_call(
        functools.partial(_outproj_kernel, d=D, alpha=alpha, n_ctx_tiles=n_ctx_tiles, t0=t0),
        grid=(B, nt),
        in_specs=[
            half, half,
            pl.BlockSpec((2 * D_HEADS, D), lambda b, t: (0, 0)),
            cspec, xspec,
            _mod_spec(D, n_ctx_tiles, t0),
            row, row,
            pl.BlockSpec((D, 128), lambda b, t: (0, 0)),
        ],
        out_specs=[
            tile,
            pl.BlockSpec((ROW_TILE * TOK_SUB, 128), lambda b, t: (b * nt + t, 0)),
            pl.BlockSpec((1, ROW_TILE, 128), lambda b, t: (b, t, 0)),
        ],
        out_shape=[
            jax.ShapeDtypeStruct((B, nt * ROW_TILE, D), F32),
            jax.ShapeDtypeStruct((B * nt * ROW_TILE * TOK_SUB, 128), F32),
            jax.ShapeDtypeStruct((B, nt * ROW_TILE, 128), F32),
        ],
        compiler_params=_cparams(("arbitrary", "arbitrary")),
    )(na, rw, w_out_b, zc, zx, modsel, ln_g, ln_b, router_wp)


def _tok(ref, i):
    return ref.at[pl.ds(pl.multiple_of(i * TOK_SUB, TOK_SUB), TOK_SUB)]


def _dispatch_kernel(dest_ref, hx_ref, xs_in, xs_out, sem):
    del xs_in
    n = dest_ref.shape[0]

    def start(i, c):
        for par in range(2):
            r = 2 * i + par
            pltpu.make_async_copy(_tok(hx_ref, r), _tok(xs_out, dest_ref[r]), sem).start(priority=par)
        return c

    lax.fori_loop(0, n // 2, start, 0, unroll=4)

    def wait(r, c):
        pltpu.make_async_copy(_tok(hx_ref, 0), _tok(xs_out, 0), sem).wait()
        return c

    lax.fori_loop(0, n, wait, 0, unroll=8)


def _dispatch(dest, hx_tiles, n_rows_pad):
    N = dest.shape[0]
    tile = next(t for t in (4 * ROW_TILE, 2 * ROW_TILE, ROW_TILE) if N % t == 0)
    xs0 = jnp.zeros((n_rows_pad * TOK_SUB, 128), F32)
    return pl.pallas_call(
        _dispatch_kernel,
        grid=(N // tile,),
        in_specs=[
            pl.BlockSpec((tile,), lambda i: (i,), memory_space=pltpu.SMEM),
            pl.BlockSpec((tile * TOK_SUB, 128), lambda i: (i, 0)),
            pl.BlockSpec(memory_space=pl.ANY),
        ],
        out_specs=pl.BlockSpec(memory_space=pl.ANY),
        out_shape=jax.ShapeDtypeStruct((n_rows_pad * TOK_SUB, 128), F32),
        scratch_shapes=[pltpu.SemaphoreType.DMA(())],
        input_output_aliases={2: 0},
        compiler_params=_cparams(("arbitrary",)),
    )(dest, hx_tiles, xs0)


def _pack_bf16_pair(a, b):
    ha = lax.bitcast_convert_type(a.astype(BF16).astype(F32), jnp.uint32)
    hb = lax.bitcast_convert_type(b.astype(BF16).astype(F32), jnp.uint32)
    return jnp.bitwise_or(ha, jnp.right_shift(hb, jnp.uint32(16)))


def _unpack_bf16_pair(w):
    a = lax.bitcast_convert_type(jnp.bitwise_and(w, jnp.uint32(0xFFFF0000)), F32)
    b = lax.bitcast_convert_type(jnp.left_shift(w, jnp.uint32(16)), F32)
    return a, b


def _expert_kernel(ea_ref, eb_ref, valid_ref, xs_ref, w1a, w3a, w2a, w1b, w3b, w2b, ys_ref,
                   c1a, c3a, c2a, c1b, c3b, c2b):
    i = pl.program_id(0)
    prev = jnp.maximum(i - 1, 0)
    changed = jnp.logical_or(i == 0, jnp.logical_or(ea_ref[i] != ea_ref[prev], eb_ref[i] != eb_ref[prev]))

    @pl.when(changed)
    def _():
        c1a[...] = w1a[0, 0].astype(BF16)
        c3a[...] = w3a[0, 0].astype(BF16)
        c2a[...] = w2a[0, 0].astype(BF16)
        c1b[...] = w1b[0, 0].astype(BF16)
        c3b[...] = w3b[0, 0].astype(BF16)
        c2b[...] = w2b[0, 0].astype(BF16)

    @pl.when(valid_ref[i] != 0)
    def _():
        x = jnp.concatenate(_from_token_tiles(xs_ref, MOE_TILE), axis=1).astype(BF16)

        def ffn(c1, c3, c2):
            h1 = _dot(x, c1[...])
            h3 = _dot(x, c3[...])
            h = (h1 * jax.nn.sigmoid(h1)) * h3
            return _dot(h.astype(BF16), c2[...])

        _to_token_tiles(ys_ref, _pack_bf16_pair(ffn(c1a, c3a, c2a), ffn(c1b, c3b, c2b)))

    @pl.when(valid_ref[i] == 0)
    def _():
        ys_ref[...] = jnp.zeros_like(ys_ref)


def _experts(blk_ea, blk_eb, blk_valid, xs, w1, w3, w2, l):
    nb = xs.shape[0] // (MOE_TILE * TOK_SUB)
    _, _, D, DE = w1.shape
    wa = lambda shape: pl.BlockSpec(shape, lambda i, ea, eb, va: (l, ea[i], 0, 0))
    wb = lambda shape: pl.BlockSpec(shape, lambda i, ea, eb, va: (l, eb[i], 0, 0))
    tok = pl.BlockSpec((MOE_TILE * TOK_SUB, 128), lambda i, ea, eb, va: (i, 0))
    grid_spec = pltpu.PrefetchScalarGridSpec(
        num_scalar_prefetch=3,
        grid=(nb,),
        in_specs=[
            tok,
            wa((1, 1, D, DE)), wa((1, 1, D, DE)), wa((1, 1, DE, D)),
            wb((1, 1, D, DE)), wb((1, 1, D, DE)), wb((1, 1, DE, D)),
        ],
        out_specs=tok,
        scratch_shapes=[pltpu.VMEM((D, DE), BF16), pltpu.VMEM((D, DE), BF16), pltpu.VMEM((DE, D), BF16),
                        pltpu.VMEM((D, DE), BF16), pltpu.VMEM((D, DE), BF16), pltpu.VMEM((DE, D), BF16)],
    )
    return pl.pallas_call(
        _expert_kernel,
        grid_spec=grid_spec,
        out_shape=jax.ShapeDtypeStruct(xs.shape, jnp.uint32),
        compiler_params=_cparams(("arbitrary",)),
    )(blk_ea, blk_eb, blk_valid, xs, w1, w3, w2, w1, w3, w2)


def _combine_kernel(dest_ref, ys_ref, gate_ref, z1_ref, mod_ref, g_ref, b_ref, o_ref, buf, sem, *, d, alpha):
    n = dest_ref.shape[0]

    def start(i, c):
        for par in range(2):
            r = 2 * i + par
            pltpu.make_async_copy(_tok(ys_ref, dest_ref[r]), _tok(buf, r), sem).start(priority=par)
        return c

    lax.fori_loop(0, n // 2, start, 0, unroll=4)

    def wait(r, c):
        pltpu.make_async_copy(_tok(ys_ref, 0), _tok(buf, 0), sem).wait()
        return c

    lax.fori_loop(0, n, wait, 0, unroll=8)
    ga = gate_ref[:, 0:1]
    gb = gate_ref[:, 1:2]
    parts = []
    for w in _from_token_tiles(buf, n):
        fa, fb = _unpack_bf16_pair(w)
        parts.append(ga * fa + gb * fb)
    y = jnp.concatenate(parts, axis=1)
    gate = mod_ref[0, 0][:, 5 * d:6 * d]
    o_ref[0] = _layer_norm(alpha * z1_ref[0] + gate * y, g_ref[...], b_ref[...])


def _combine(dest, ys, gates, z1, modsel, ln_g, ln_b, n_ctx_tiles, t0, t_out, alpha):
    B, T1, D = z1.shape
    nt1 = T1 // ROW_TILE
    skip = t_out - t0
    nt = nt1 - skip
    tile = lambda off: pl.BlockSpec((1, ROW_TILE, D), lambda b, t: (b, t + off, 0))
    row = pl.BlockSpec((1, D), lambda b, t: (0, 0))
    return pl.pallas_call(
        functools.partial(_combine_kernel, d=D, alpha=alpha),
        grid=(B, nt),
        in_specs=[
            pl.BlockSpec((ROW_TILE,), lambda b, t: (b * nt1 + t + skip,), memory_space=pltpu.SMEM),
            pl.BlockSpec(memory_space=pl.ANY),
            pl.BlockSpec((ROW_TILE, 128), lambda b, t: (b * nt1 + t + skip, 0)),
            tile(skip),
            _mod_spec(D, n_ctx_tiles, t_out),
            row, row,
        ],
        out_specs=tile(0),
        out_shape=jax.ShapeDtypeStruct((B, nt * ROW_TILE, D), F32),
        scratch_shapes=[pltpu.VMEM((ROW_TILE * TOK_SUB, 128), jnp.uint32), pltpu.SemaphoreType.DMA(())],
        compiler_params=_cparams(("arbitrary", "arbitrary")),
    )(dest, ys, gates, z1, modsel, ln_g, ln_b)


_PAIR_LO = np.array([0, 0, 0, 1, 1, 2], np.int32)
_PAIR_HI = np.array([1, 2, 3, 2, 3, 3], np.int32)


def _route(logits, router_bias):
    N = logits.shape[0]
    scores = jax.nn.sigmoid(logits[:, :N_EXPERTS])
    sel = (scores + router_bias.astype(F32)).reshape(N, N_GROUPS, EXPERTS_PER_GROUP)
    pos = jnp.arange(EXPERTS_PER_GROUP, dtype=jnp.int32)
    a1 = jnp.argmax(sel, axis=-1)
    m1 = jnp.max(sel, axis=-1)
    rest = jnp.where(pos == a1[..., None], -jnp.inf, sel)
    m2 = jnp.max(rest, axis=-1)
    g_idx = jnp.argmax(m1 + m2, axis=-1).astype(jnp.int32)
    in_g = (jnp.arange(N_GROUPS, dtype=jnp.int32)[None, :] == g_idx[:, None])[..., None]
    sel_g = jnp.sum(jnp.where(in_g, sel, 0.0), axis=1)
    i1 = jnp.argmax(sel_g, axis=-1).astype(jnp.int32)
    i2 = jnp.argmax(jnp.where(pos == i1[:, None], -jnp.inf, sel_g), axis=-1).astype(jnp.int32)
    sc_g = jnp.sum(jnp.where(in_g, scores.reshape(N, N_GROUPS, EXPERTS_PER_GROUP), 0.0), axis=1)
    lo = jnp.minimum(i1, i2)
    hi = jnp.maximum(i1, i2)
    g_lo = jnp.sum(jnp.where(pos == lo[:, None], sc_g, 0.0), axis=1)
    g_hi = jnp.sum(jnp.where(pos == hi[:, None], sc_g, 0.0), axis=1)
    tot = g_lo + g_hi
    g_lo = g_lo / tot
    g_hi = g_hi / tot
    pair = lo * 3 - (lo * (lo - 1)) // 2 + (hi - lo - 1)
    cls = g_idx * 6 + pair

    onehot = (cls[:, None] == jnp.arange(N_CLASSES, dtype=jnp.int32)[None, :])
    oh = onehot.astype(BF16).reshape(N // ROW_TILE, ROW_TILE, N_CLASSES)
    tri = jnp.tril(jnp.ones((ROW_TILE, ROW_TILE), BF16), -1)
    within = jnp.einsum("rs,tsc->trc", tri, oh, preferred_element_type=F32)
    tile_cnt = jnp.sum(oh.astype(F32), axis=1)
    tile_off = jnp.cumsum(tile_cnt, axis=0) - tile_cnt
    rank_all = (within + tile_off[:, None, :]).reshape(N, N_CLASSES)
    rank = jnp.sum(jnp.where(onehot, rank_all, 0.0), axis=1).astype(jnp.int32)
    counts = jnp.sum(tile_cnt, axis=0).astype(jnp.int32)
    padded = (counts + MOE_TILE - 1) // MOE_TILE * MOE_TILE
    cls_end = jnp.cumsum(padded)
    cls_start = cls_end - padded
    dest = jnp.sum(jnp.where(onehot, cls_start[None, :], 0), axis=1) + rank

    nb = N // MOE_TILE + N_CLASSES
    blk_row = jnp.arange(nb, dtype=jnp.int32) * MOE_TILE
    total = jnp.sum(padded)
    blk_valid = (blk_row < total).astype(jnp.int32)
    row_c = jnp.minimum(blk_row, jnp.maximum(total - MOE_TILE, 0))
    blk_cls = jnp.sum((row_c[:, None] >= cls_end[None, :]).astype(jnp.int32), axis=1)
    blk_cls = jnp.minimum(blk_cls, N_CLASSES - 1)
    blk_grp = blk_cls // 6
    pair_hot = (blk_cls % 6)[:, None] == jnp.arange(6, dtype=jnp.int32)[None, :]
    blk_ea = blk_grp * EXPERTS_PER_GROUP + jnp.sum(jnp.where(pair_hot, jnp.asarray(_PAIR_LO)[None, :], 0), axis=1)
    blk_eb = blk_grp * EXPERTS_PER_GROUP + jnp.sum(jnp.where(pair_hot, jnp.asarray(_PAIR_HI)[None, :], 0), axis=1)
    gates = jnp.pad(jnp.stack([g_lo, g_hi], axis=1), ((0, 0), (0, 126)))
    return dest.astype(jnp.int32), gates, blk_ea.astype(jnp.int32), blk_eb.astype(jnp.int32), blk_valid, nb * MOE_TILE


def _rope_tables(n_ctx, seq):
    t = np.arange(seq)
    row = (t // GRID_W).astype(np.float32)
    col = (t % GRID_W).astype(np.float32)
    n_freq = HEAD_DIM // 4
    inv = jnp.asarray(ROPE_BASE, F32) ** (-jnp.arange(n_freq, dtype=F32) / n_freq)
    ar = jnp.asarray(row)[:, None] * inv
    ac = jnp.asarray(col)[:, None] * inv
    ang = jnp.concatenate([ar, ar, ac, ac], -1)
    cos = jnp.cos(ang)
    sin = jnp.sin(ang)
    quarter = (np.arange(HEAD_DIM) // n_freq) % 2
    sa = jnp.where(quarter == 0, -sin, 0.0)
    sb = jnp.where(quarter == 1, sin, 0.0)
    pad = lambda a, fill: jnp.concatenate([jnp.full((n_ctx, HEAD_DIM), fill, F32), a], 0)
    tile = lambda a: jnp.tile(a, (1, N_HEADS))
    return tile(pad(cos, 1.0)), tile(pad(sa, 0.0)), tile(pad(sb, 0.0))


def _rw_consts(l, n_ctx, seq, rw_mu_prev, rw_mu_next, rw_w0, rw_w2, rw_a0, rw_a2, rw_g2, rw_k_k, rw_k_a, rw_r_k):
    d_rw_in = rw_mu_prev.shape[1]
    padw = lambda a: jnp.pad(a[l], (0, D_RWP - d_rw_in)).reshape(1, D_RWP)
    cos, sa, sb = _rope_tables(n_ctx, seq)
    w2p = jnp.zeros((128, 2 * D_HEADS), F32)
    w2p = w2p.at[0:LORA, 0:D_HEADS].set(rw_w2[l, 0]).at[LORA:2 * LORA, D_HEADS:].set(rw_w2[l, 1])
    a2p = jnp.zeros((128, 2 * D_HEADS), F32)
    a2p = a2p.at[2 * LORA:3 * LORA, 0:D_HEADS].set(rw_a2[l, 0]).at[3 * LORA:4 * LORA, D_HEADS:].set(rw_a2[l, 1])
    g2p = jnp.zeros((128, D_HEADS), F32).at[0:GATE_LORA].set(rw_g2[l])
    head = np.arange(D_HEADS) // HEAD_DIM
    ones_bd = jnp.asarray(head[:, None] == head[None, :], BF16)
    return dict(
        mu_prev=padw(rw_mu_prev), mu_next=padw(rw_mu_next), cos=cos, sa=sa, sb=sb,
        k_k=rw_k_k[l].reshape(1, D_HEADS), k_a=rw_k_a[l].reshape(1, D_HEADS), r_k=rw_r_k[l].reshape(1, D_HEADS),
        w0=rw_w0[l].reshape(1, 2 * D_HEADS), a0=rw_a0[l].reshape(1, 2 * D_HEADS),
        w2p=w2p, a2p=a2p, g2p=g2p, ones_bd=ones_bd,
    )


def kernel(x, c, ctx, c_ctx, ada_w, ada_b, w_in, na_rpb, rw_mu_prev, rw_mu_next, rw_w0, rw_w2, rw_a0, rw_a2, rw_g2, rw_k_k, rw_k_a, rw_r_k, rw_gn_g, rw_gn_b, w_out, ln1_g, ln1_b, ln2_g, ln2_b, router_w, router_bias, exp_w1, exp_w3, exp_w2):
    B, S, D = x.shape
    C = ctx.shape[1]
    L = ada_w.shape[0]
    T = C + S
    assert D == 1024 and C % ROW_TILE == 0 and S % ROW_TILE == 0 and C % (2 * CHUNK) == 0
    assert S % GRID_W == 0 and S // GRID_W >= WIN_ROWS and w_in.shape[2] == D_INP - 32
    n_ctx_tiles = C // ROW_TILE
    alpha = float((2 * L) ** 0.25)

    zc, zx, sub = ctx, x, n_ctx_tiles
    n_mod = (B + 1 + 7) // 8 * 8
    cc = jnp.zeros((n_mod, D), F32).at[0:B].set(c).at[B].set(c_ctx)
    mod_all = _ada(cc, ada_w, ada_b)
    router_wp = jnp.pad(router_w, ((0, 0), (0, 128 - N_EXPERTS)))

    for l in range(L):
        mod_c = jnp.broadcast_to(mod_all[l, B][None], (B, 6 * D))
        modsel = jnp.stack([mod_c, mod_all[l, 0:B]], axis=1).reshape(B, 2, 1, 6 * D)
        w_in_p = jnp.pad(w_in[l], ((0, 0), (0, D_INP - w_in.shape[2]))).astype(BF16)
        qkv, p_rw = _inproj(zc, zx, sub, T, modsel, w_in_p, n_ctx_tiles)

        consts = _rw_consts(l, C, S, rw_mu_prev, rw_mu_next, rw_w0, rw_w2, rw_a0, rw_a2, rw_g2,
                            rw_k_k, rw_k_a, rw_r_k)
        prep = _rwprep(p_rw, consts, n_ctx_tiles)
        yf, yb = _wkv(prep[0:15], C)
        rw = _rwpost(yf, yb, prep[15], prep[16], rw_gn_g[l].reshape(1, D_HEADS), rw_gn_b[l].reshape(1, D_HEADS))

        na = _attention(qkv, _na_bias_table(na_rpb[l]), C)

        t0 = n_ctx_tiles if l == L - 1 else 0
        z1, hx_tiles, logits = _outproj(na, rw, w_out[l].astype(BF16), zc, zx, sub, modsel,
                                        ln1_g[l].reshape(1, D), ln1_b[l].reshape(1, D), router_wp,
                                        n_ctx_tiles, t0, alpha)

        dest, gates, blk_ea, blk_eb, blk_valid, n_rows_pad = _route(logits.reshape(-1, 128), router_bias)
        xs = _dispatch(dest, hx_tiles, n_rows_pad)
        ys = _experts(blk_ea, blk_eb, blk_valid, xs, exp_w1, exp_w3, exp_w2, l)
        z = _combine(dest, ys, gates, z1, modsel, ln2_g[l].reshape(1, D), ln2_b[l].reshape(1, D),
                     n_ctx_tiles, t0, t0, alpha)
        zc, zx, sub = z, z, 0

    return z
```

```python
import functools
import math

import jax
import jax.numpy as jnp
import numpy as np
from jax import lax
from jax.experimental import pallas as pl
from jax.experimental.pallas import tpu as pltpu

F32 = jnp.float32
BF16 = jnp.bfloat16

HEAD_DIM = 64
N_HEADS = 8
D_HEADS = N_HEADS * HEAD_DIM
N_PAIRS = N_HEADS // 2
GRID_W = 64
WIN_ROWS = 8
WIN_COLS = 16
LORA = 32
GATE_LORA = 96
N_EXPERTS = 32
N_GROUPS = 8
EXPERTS_PER_GROUP = 4
N_CLASSES = N_GROUPS * 6
ROPE_BASE = 10000.0
LN_EPS = 1e-6
GN_EPS = 64e-5
CHUNK = 64
ROW_TILE = 256
MOE_TILE = 256
D_QKV = 3 * D_HEADS
D_RWP = 3 * D_HEADS + 256
D_INP = D_QKV + D_RWP
TOK_SUB = 8
NEG = -1e30
VMEM_LIMIT = 56 * 1024 * 1024


def _cparams(sem):
    return pltpu.CompilerParams(dimension_semantics=sem, vmem_limit_bytes=VMEM_LIMIT)


def _dot(a, b):
    return jnp.dot(a, b, preferred_element_type=F32)


def _dot_nt(a, b):
    return lax.dot_general(a, b, (((1,), (1,)), ((), ())), preferred_element_type=F32)


def _split2(a):
    hi = a.astype(BF16)
    lo = (a - hi.astype(F32)).astype(BF16)
    return hi, lo


def _split3(a):
    hi = a.astype(BF16)
    r1 = a - hi.astype(F32)
    mid = r1.astype(BF16)
    lo = (r1 - mid.astype(F32)).astype(BF16)
    return hi, mid, lo


def _dot3(a, b):
    ah, al = _split2(a)
    bh, bl = _split2(b)
    return _dot(ah, bh) + _dot(al, bh) + _dot(ah, bl)


def _dot_exact_rhs(a, b_exact):
    h, m, l = _split3(a)
    return _dot(h, b_exact) + _dot(m, b_exact) + _dot(l, b_exact)


def _dot_exact_lhs(a_exact, b):
    h, m, l = _split3(b)
    return _dot(a_exact, h) + _dot(a_exact, m) + _dot(a_exact, l)


def _ada_kernel(cc_ref, w_ref, b_ref, o_ref):
    cc = cc_ref[...]
    s = cc * jax.nn.sigmoid(cc)
    o_ref[0] = _dot3(s, w_ref[0]) + b_ref[0]


def _ada(cc, ada_w, ada_b):
    L, D, D6 = ada_w.shape
    R = cc.shape[0]
    tn = 1536
    return pl.pallas_call(
        _ada_kernel,
        grid=(L, D6 // tn),
        in_specs=[
            pl.BlockSpec((R, D), lambda l, n: (0, 0)),
            pl.BlockSpec((1, D, tn), lambda l, n: (l, 0, n)),
            pl.BlockSpec((1, 1, tn), lambda l, n: (l, 0, n)),
        ],
        out_specs=pl.BlockSpec((1, R, tn), lambda l, n: (l, 0, n)),
        out_shape=jax.ShapeDtypeStruct((L, R, D6), F32),
        compiler_params=_cparams(("arbitrary", "arbitrary")),
    )(cc, ada_w, ada_b.reshape(L, 1, D6))


def _z_specs(D, n_ctx_tiles, sub, t0=0):
    cspec = pl.BlockSpec((1, ROW_TILE, D), lambda b, t: (b, jnp.minimum(t + t0, n_ctx_tiles - 1), 0))
    xspec = pl.BlockSpec((1, ROW_TILE, D), lambda b, t: (b, jnp.maximum(t + t0, n_ctx_tiles) - sub, 0))
    return cspec, xspec


def _z_tile(zc_ref, zx_ref, n_ctx_tiles, t0=0):
    return jnp.where(pl.program_id(1) + t0 < n_ctx_tiles, zc_ref[0], zx_ref[0])


def _mod_spec(D, n_ctx_tiles, t0=0):
    return pl.BlockSpec((1, 1, 1, 6 * D), lambda b, t: (b, jnp.where(t + t0 < n_ctx_tiles, 0, 1), 0, 0))


def _inproj_kernel(zc_ref, zx_ref, mod_ref, w_ref, qkv_ref, rw_ref, *, d, n_ctx_tiles):
    z = _z_tile(zc_ref, zx_ref, n_ctx_tiles)
    mod = mod_ref[0, 0]
    shift = mod[:, 0:d]
    scale = mod[:, d:2 * d]
    h = (z * (1.0 + scale) + shift).astype(BF16)
    q = _dot(h, w_ref[:, 0:D_HEADS])
    qkv_ref[0, :, 0:D_HEADS] = (q * (HEAD_DIM ** -0.5)).astype(BF16)
    kv = _dot(h, w_ref[:, D_HEADS:D_QKV])
    qkv_ref[0, :, D_HEADS:D_QKV] = kv.astype(BF16)
    rw_ref[0] = _dot(h, w_ref[:, D_QKV:D_INP])


def _inproj(zc, zx, sub, T, modsel, w_in_p, n_ctx_tiles):
    B, _, D = zc.shape
    nt = T // ROW_TILE
    cspec, xspec = _z_specs(D, n_ctx_tiles, sub)
    return pl.pallas_call(
        functools.partial(_inproj_kernel, d=D, n_ctx_tiles=n_ctx_tiles),
        grid=(B, nt),
        in_specs=[
            cspec, xspec,
            pl.BlockSpec((1, 1, 1, 6 * D), lambda b, t: (b, jnp.where(t < n_ctx_tiles, 0, 1), 0, 0)),
            pl.BlockSpec((D, D_INP), lambda b, t: (0, 0)),
        ],
        out_specs=[
            pl.BlockSpec((1, ROW_TILE, D_QKV), lambda b, t: (b, t, 0)),
            pl.BlockSpec((1, ROW_TILE, D_RWP), lambda b, t: (b, t, 0)),
        ],
        out_shape=[
            jax.ShapeDtypeStruct((B, T, D_QKV), BF16),
            jax.ShapeDtypeStruct((B, T, D_RWP), F32),
        ],
        compiler_params=_cparams(("arbitrary", "arbitrary")),
    )(zc, zx, modsel, w_in_p)


def _rwprep_kernel(p_ref, pp_ref, pn_ref, mup_ref, mun_ref, cos_ref, sa_ref, sb_ref,
                   kk_ref, ka_ref, rk_ref, w0_ref, a0_ref, w2_ref, a2_ref, g2_ref, ones_ref,
                   at_f, bt_f, kt_f, rt_f, bh_f, kh_f, pe_f,
                   at_b, bt_b, kt_b, rt_b, bh_b, kh_b, pe_b,
                   vt_ref, bonus_ref, g_ref, *, n_ctx_tiles, n_tiles):
    t = pl.program_id(1)
    P = p_ref[0]
    R = P.shape[0]
    prev_ok = jnp.logical_and(t != 0, t != n_ctx_tiles)
    next_ok = jnp.logical_and(t != n_ctx_tiles - 1, t != n_tiles - 1)
    prev_row = jnp.where(prev_ok, pp_ref[0, 7:8, :], 0.0)
    next_row = jnp.where(next_ok, pn_ref[0, 0:1, :], 0.0)
    row = lax.broadcasted_iota(jnp.int32, (R, 1), 0)
    prev = jnp.where(row == 0, prev_row, pltpu.roll(P, 1, axis=0))
    nxt = jnp.where(row == R - 1, next_row, pltpu.roll(P, R - 1, axis=0))
    mup = mup_ref[...]
    mun = mun_ref[...]
    z = (1.0 - mup - mun) * P + mup * prev + mun * nxt

    cos = cos_ref[...]
    sa = sa_ref[...]
    sb = sb_ref[...]

    def rope(u):
        q = HEAD_DIM // 4
        parts = []
        for m in range(D_HEADS // 128):
            ls = slice(m * 128, (m + 1) * 128)
            um = u[:, ls]
            parts.append(um * cos[:, ls] + pltpu.roll(um, 128 - q, axis=1) * sa[:, ls]
                         + pltpu.roll(um, q, axis=1) * sb[:, ls])
        return jnp.concatenate(parts, axis=1)

    r = rope(z[:, 0:D_HEADS])
    k = rope(z[:, D_HEADS:2 * D_HEADS])
    v = z[:, 2 * D_HEADS:3 * D_HEADS]
    ones_bd = ones_ref[...]

    kk = k * kk_ref[...]
    sq_hi, sq_lo = _split2(kk * kk)
    kk = kk * lax.rsqrt(jnp.maximum(_dot(sq_hi, ones_bd) + _dot(sq_lo, ones_bd), 1e-24))

    sigmoid = lambda u: 0.5 * jnp.tanh(0.5 * u) + 0.5
    slab = z[:, 3 * D_HEADS:3 * D_HEADS + 128]
    u_w = w0_ref[...] + _dot(jnp.tanh(slab).astype(BF16), w2_ref[...])
    u_a = a0_ref[...] + _dot(slab.astype(BF16), a2_ref[...])
    g_ref[0] = _dot(sigmoid(z[:, 3 * D_HEADS + 128:D_RWP]).astype(BF16), g2_ref[...]).astype(BF16)
    e_all = math.exp(-0.5) * sigmoid(u_w)
    a_all = sigmoid(u_a)

    ci = lax.broadcasted_iota(jnp.int32, (CHUNK, CHUNK), 0)
    cj = lax.broadcasted_iota(jnp.int32, (CHUNK, CHUNK), 1)
    ka = ka_ref[...]
    outs = ((at_f, bt_f, kt_f, rt_f, bh_f, kh_f, pe_f), (at_b, bt_b, kt_b, rt_b, bh_b, kh_b, pe_b))
    kd_sum = None
    for d in range(2):
        e = e_all[:, d * D_HEADS:(d + 1) * D_HEADS]
        a = a_all[:, d * D_HEADS:(d + 1) * D_HEADS]
        tri = (cj <= ci) if d == 0 else (cj >= ci)
        tri = tri.astype(BF16)
        cs, ce = [], []
        for q in range(R // CHUNK):
            e_hi, e_lo = _split2(e[q * CHUNK:(q + 1) * CHUNK])
            cq = -(_dot(tri, e_hi) + _dot(tri, e_lo))
            end = cq[CHUNK - 1:CHUNK] if d == 0 else cq[0:1]
            cs.append(cq)
            ce.append(jnp.broadcast_to(end, cq.shape))
        c = jnp.concatenate(cs, axis=0)
        cend = jnp.concatenate(ce, axis=0)
        kd = k * (1.0 + (a - 1.0) * ka)
        kd_sum = kd if kd_sum is None else kd_sum + kd
        beta = a * kk
        en = jnp.exp(-c)
        eh = jnp.exp(cend - c)
        o_at, o_bt, o_kt, o_rt, o_bh, o_kh, o_pe = outs[d]
        o_at[0] = (-kk * jnp.exp(c + e)).astype(BF16)
        o_bt[0] = (beta * en).astype(BF16)
        o_kt[0] = (kd * en).astype(BF16)
        o_rt[0] = (r * jnp.exp(c)).astype(BF16)
        o_bh[0] = (beta * eh).astype(BF16)
        o_kh[0] = (kd * eh).astype(BF16)
        for q in range(R // CHUNK):
            o_pe[0, q] = jnp.exp(cend[q * CHUNK:q * CHUNK + 8])
    bonus_ref[0] = (_dot((r * rk_ref[...] * kd_sum).astype(BF16), ones_bd) * v).astype(BF16)
    vT = v.T
    m0 = lax.broadcasted_iota(jnp.int32, (HEAD_DIM, 2 * CHUNK), 1) < CHUNK
    for p in range(N_PAIRS):
        for j in range(R // (2 * CHUNK)):
            x0 = vT[p * 128:p * 128 + HEAD_DIM, j * 128:(j + 1) * 128]
            x1 = vT[p * 128 + HEAD_DIM:(p + 1) * 128, j * 128:(j + 1) * 128]
            vt_ref[0, 2 * j, p] = jnp.where(m0, x0, pltpu.roll(x1, CHUNK, axis=1)).astype(BF16)
            vt_ref[0, 2 * j + 1, p] = jnp.where(m0, pltpu.roll(x0, CHUNK, axis=1), x1).astype(BF16)


def _rwprep(p_rw, consts, n_ctx_tiles):
    B, T, _ = p_rw.shape
    nt = T // ROW_TILE
    nh = ROW_TILE // 8
    row = lambda w: pl.BlockSpec((1, w), lambda b, t: (0, 0))
    full = lambda a: pl.BlockSpec(a.shape, lambda b, t: (0, 0))
    tm = pl.BlockSpec((1, ROW_TILE, D_HEADS), lambda b, t: (b, t, 0))
    tab = pl.BlockSpec((ROW_TILE, D_HEADS), lambda b, t: (t, 0))
    pe = pl.BlockSpec((1, ROW_TILE // CHUNK, 8, D_HEADS), lambda b, t: (b, t, 0, 0))
    tm_shape = jax.ShapeDtypeStruct((B, T, D_HEADS), BF16)
    pe_shape = jax.ShapeDtypeStruct((B, T // CHUNK, 8, D_HEADS), F32)
    dir_specs = [tm] * 6 + [pe]
    dir_shapes = [tm_shape] * 6 + [pe_shape]
    return pl.pallas_call(
        functools.partial(_rwprep_kernel, n_ctx_tiles=n_ctx_tiles, n_tiles=nt),
        grid=(B, nt),
        in_specs=[
            pl.BlockSpec((1, ROW_TILE, D_RWP), lambda b, t: (b, t, 0)),
            pl.BlockSpec((1, 8, D_RWP), lambda b, t: (b, jnp.maximum(t * nh - 1, 0), 0)),
            pl.BlockSpec((1, 8, D_RWP), lambda b, t: (b, jnp.minimum((t + 1) * nh, T // 8 - 1), 0)),
            row(D_RWP), row(D_RWP), tab, tab, tab,
            row(D_HEADS), row(D_HEADS), row(D_HEADS), row(2 * D_HEADS), row(2 * D_HEADS),
            full(consts["w2p"]), full(consts["a2p"]), full(consts["g2p"]), full(consts["ones_bd"]),
        ],
        out_specs=dir_specs + dir_specs + [
            pl.BlockSpec((1, ROW_TILE // CHUNK, N_PAIRS, HEAD_DIM, 2 * CHUNK), lambda b, t: (b, t, 0, 0, 0)),
            tm, tm,
        ],
        out_shape=dir_shapes + dir_shapes + [
            jax.ShapeDtypeStruct((B, T // CHUNK, N_PAIRS, HEAD_DIM, 2 * CHUNK), BF16),
            jax.ShapeDtypeStruct((B, T, D_HEADS), BF16),
            jax.ShapeDtypeStruct((B, T, D_HEADS), BF16),
        ],
        compiler_params=_cparams(("arbitrary", "arbitrary")),
    )(p_rw, p_rw, p_rw, consts["mu_prev"], consts["mu_next"], consts["cos"], consts["sa"], consts["sb"],
      consts["k_k"], consts["k_a"], consts["r_k"], consts["w0"], consts["a0"],
      consts["w2p"], consts["a2p"], consts["g2p"], consts["ones_bd"])


def _bd(y, m0):
    zero = jnp.zeros_like(y)
    return jnp.concatenate([jnp.where(m0, y, zero), jnp.where(m0, zero, y)], axis=0)


def _sel(w, m0):
    return jnp.where(m0, w[0:CHUNK], w[CHUNK:2 * CHUNK])


def _wkv_kernel(*refs):
    (at_f, bt_f, kt_f, rt_f, bh_f, kh_f, pe_f, vt_f,
     at_b, bt_b, kt_b, rt_b, bh_b, kh_b, pe_b, vt_b,
     yf_ref, yb_ref, s_ref) = refs
    s = pl.program_id(1)

    @pl.when(s == 0)
    def _():
        s_ref[...] = jnp.zeros_like(s_ref)

    lane = lax.broadcasted_iota(jnp.int32, (CHUNK, 2 * CHUNK), 1)
    rowi = lax.broadcasted_iota(jnp.int32, (CHUNK, 2 * CHUNK), 0)
    lm = jnp.bitwise_and(lane, CHUNK - 1)
    m0 = lane < CHUNK
    dirs = ((at_f, bt_f, kt_f, rt_f, bh_f, kh_f, pe_f, vt_f, yf_ref),
            (at_b, bt_b, kt_b, rt_b, bh_b, kh_b, pe_b, vt_b, yb_ref))
    masks = (((rowi < lm), (rowi <= lm)), ((rowi > lm), (rowi >= lm)))
    zero = jnp.zeros((CHUNK, 2 * CHUNK), F32)
    bd = lambda y: _bd(y, m0)
    bf = lambda y: y.astype(BF16)

    probs = []
    for rnd in range(2):
        for d in range(2):
            for p in range(N_PAIRS):
                half = rnd if d == 0 else 1 - rnd
                probs.append(dict(d=d, p=p, half=half, rnd=rnd,
                                  rs=slice(half * CHUNK, (half + 1) * CHUNK), ls=slice(p * 128, (p + 1) * 128)))

    def ld(pr, i):
        return dirs[pr["d"]][i][0, pr["rs"], pr["ls"]]

    def rhs1(pr):
        return jnp.concatenate([bd(ld(pr, 0)), bd(ld(pr, 3))], axis=0)

    for pr in probs:
        G = _dot_nt(jnp.concatenate([ld(pr, 1), ld(pr, 2)], axis=0), rhs1(pr))
        strict, incl = masks[pr["d"]]
        pr["N"] = jnp.where(strict, G[0:CHUNK, 0:128], zero)
        pr["N_br"] = bf(jnp.where(incl, G[0:CHUNK, 128:256], zero))
        pr["A_ak"] = bf(jnp.where(strict, G[CHUNK:128, 0:128], zero))
        pr["N_kr"] = bf(jnp.where(incl, G[CHUNK:128, 128:256], zero))
    for pr in probs:
        Ab = bf(pr["N"])
        pr["M"] = _dot(Ab, bd(Ab))
    for _ in range(4):
        for pr in probs:
            Mb = bf(pr["M"])
            Rm = _dot(jnp.concatenate([bf(pr["N"]), Mb], axis=0), bd(Mb))
            pr["N"] = pr["N"] + pr["M"] + Rm[0:CHUNK]
            pr["M"] = Rm[CHUNK:2 * CHUNK]
    for pr in probs:
        pr["N"] = bf(pr["N"] + pr["M"] + _dot(bf(pr["N"]), bd(bf(pr["M"]))))
        del pr["M"]
    for pr in probs:
        vtp = dirs[pr["d"]][7][0, pr["half"], pr["p"]]
        VG = _dot(vtp, jnp.concatenate([bd(pr["A_ak"]), bd(pr["N_kr"]), bd(ld(pr, 5))], axis=1))
        pr["VA"] = VG[:, 0:128]
        pr["VN"] = VG[:, 128:256]
        pr["VK"] = VG[:, 256:384]

    S = {(d, p): s_ref[d, p] for d in range(2) for p in range(N_PAIRS)}
    ys = {}
    for rnd in range(2):
        cur = [pr for pr in probs if pr["rnd"] == rnd]
        for pr in cur:
            St = S[(pr["d"], pr["p"])]
            SG = _dot_nt(bf(St), rhs1(pr))
            pr["X"] = SG[:, 0:128] + pr["VA"]
            pr["Y"] = SG[:, 128:256] + pr["VN"]
        for pr in cur:
            pr["U"] = bf(pr["X"] + _dot(bf(pr["X"]), bd(pr["N"])))
        for pr in cur:
            UG = _dot(pr["U"], jnp.concatenate([bd(pr["N_br"]), bd(ld(pr, 4))], axis=1))
            key = (pr["d"], pr["p"])
            pend = dirs[pr["d"]][6][0, pr["half"], 0:1, pr["ls"]]
            S[key] = S[key] * pend + UG[:, 128:256] + pr["VK"]
            ys[(pr["d"], pr["p"], pr["half"])] = pr["Y"] + UG[:, 0:128]
    for d in range(2):
        y_ref = dirs[d][8]
        for p in range(N_PAIRS):
            s_ref[d, p] = S[(d, p)]
            y0, y1 = ys[(d, p, 0)], ys[(d, p, 1)]
            y_ref[0, p * 128:p * 128 + CHUNK, :] = jnp.where(m0, y0, pltpu.roll(y1, CHUNK, axis=1)).astype(BF16)
            y_ref[0, p * 128 + CHUNK:(p + 1) * 128, :] = jnp.where(m0, pltpu.roll(y0, CHUNK, axis=1), y1).astype(BF16)


def _wkv(prep, n_ctx):
    (at_f, bt_f, kt_f, rt_f, bh_f, kh_f, pe_f, at_b, bt_b, kt_b, rt_b, bh_b, kh_b, pe_b, vt) = prep
    B, T, _ = at_f.shape
    ns = T // (2 * CHUNK)
    nc2 = n_ctx // (2 * CHUNK)

    def mrev(s):
        return jnp.where(s < nc2, nc2 - 1 - s, ns - 1 - (s - nc2))

    def specs(idx):
        tm = pl.BlockSpec((1, 2 * CHUNK, D_HEADS), lambda b, s: (b, idx(s), 0))
        pe = pl.BlockSpec((1, 2, 8, D_HEADS), lambda b, s: (b, idx(s), 0, 0))
        vts = pl.BlockSpec((1, 2, N_PAIRS, HEAD_DIM, 2 * CHUNK), lambda b, s: (b, idx(s), 0, 0, 0))
        return [tm] * 6 + [pe, vts]

    fwd = lambda s: s
    yt = lambda idx: pl.BlockSpec((1, D_HEADS, 2 * CHUNK), lambda b, s: (b, 0, idx(s)))
    return pl.pallas_call(
        _wkv_kernel,
        grid=(B, ns),
        in_specs=specs(fwd) + specs(mrev),
        out_specs=[yt(fwd), yt(mrev)],
        out_shape=[jax.ShapeDtypeStruct((B, D_HEADS, T), BF16)] * 2,
        scratch_shapes=[pltpu.VMEM((2, N_PAIRS, CHUNK, 2 * CHUNK), F32)],
        compiler_params=_cparams(("arbitrary", "arbitrary")),
    )(at_f, bt_f, kt_f, rt_f, bh_f, kh_f, pe_f, vt, at_b, bt_b, kt_b, rt_b, bh_b, kh_b, pe_b, vt)


def _rwpost_kernel(yf_ref, yb_ref, bonus_ref, g_ref, gg_ref, gb_ref, o_ref):
    y = yf_ref[0].astype(F32) + yb_ref[0].astype(F32)
    R = y.shape[1]
    y3 = y.reshape(N_HEADS, HEAD_DIM, R)
    mu = jnp.mean(y3, axis=1, keepdims=True)
    var = jnp.mean(jnp.square(y3 - mu), axis=1, keepdims=True)
    yn = ((y3 - mu) * lax.rsqrt(var + GN_EPS)).reshape(D_HEADS, R)
    out = (yn.T * gg_ref[...] + gb_ref[...] + bonus_ref[0].astype(F32)) * g_ref[0].astype(F32)
    o_ref[0] = out.astype(BF16)


def _rwpost(yf, yb, bonus, g, gn_g, gn_b):
    B, _, T = yf.shape
    nt = T // ROW_TILE
    ytile = pl.BlockSpec((1, D_HEADS, ROW_TILE), lambda b, t: (b, 0, t))
    tm = pl.BlockSpec((1, ROW_TILE, D_HEADS), lambda b, t: (b, t, 0))
    row = pl.BlockSpec((1, D_HEADS), lambda b, t: (0, 0))
    return pl.pallas_call(
        _rwpost_kernel,
        grid=(B, nt),
        in_specs=[ytile, ytile, tm, tm, row, row],
        out_specs=tm,
        out_shape=jax.ShapeDtypeStruct((B, T, D_HEADS), BF16),
        compiler_params=_cparams(("arbitrary", "arbitrary")),
    )(yf, yb, bonus, g, gn_g, gn_b)


ATT_ROWS = 2


def _attn_kernel(q_ref, k_ref, v_ref, *rest, n_ctx, n_rows):
    bias_refs, o_ref = rest[:ATT_ROWS], rest[ATT_ROWS]
    j = pl.program_id(1)
    n_cstep = n_ctx // (GRID_W * ATT_ROWS)
    lane = lax.broadcasted_iota(jnp.int32, (GRID_W, 128), 1)
    m0 = lane < HEAD_DIM
    win = WIN_ROWS * GRID_W
    rmax = lambda a: jnp.max(a, axis=-1, keepdims=True)
    rsum = lambda a: jnp.sum(a, axis=-1, keepdims=True)
    probs = [(u, p, slice(u * GRID_W, (u + 1) * GRID_W), slice(p * 128, (p + 1) * 128))
             for u in range(ATT_ROWS) for p in range(N_PAIRS)]

    def stacked_q(rs, ls):
        return _bd(q_ref[0, rs, ls], m0)

    @pl.when(j < n_cstep)
    def _():
        sc = [_dot_nt(stacked_q(rs, ls), k_ref[0, 0:n_ctx, ls]) for _, _, rs, ls in probs]
        mx = [rmax(a) for a in sc]
        ex = [jnp.exp(a - m) for a, m in zip(sc, mx)]
        den = [rsum(e) for e in ex]
        for n, (_, _, rs, ls) in enumerate(probs):
            o = _dot(ex[n].astype(BF16), v_ref[0, 0:n_ctx, ls]) / den[n]
            o_ref[0, rs, ls] = _sel(o, m0).astype(BF16)

    @pl.when(j >= n_cstep)
    def _():
        starts = []
        for u in range(ATT_ROWS):
            i = (j - n_cstep) * ATT_ROWS + u
            r0 = jnp.clip(i - WIN_ROWS // 2, 0, n_rows - WIN_ROWS)
            starts.append(pl.multiple_of(n_ctx + r0 * GRID_W, GRID_W))
        qs = [stacked_q(rs, ls) for _, _, rs, ls in probs]
        s_loc = [_dot_nt(qs[n], k_ref[0, pl.ds(starts[u], win), ls]) + bias_refs[u][0, p]
                 for n, (u, p, _, ls) in enumerate(probs)]
        s_ctx = [_dot_nt(qs[n], k_ref[0, 0:n_ctx, ls]) for n, (_, _, _, ls) in enumerate(probs)]
        mx = [jnp.maximum(rmax(a), rmax(b)) for a, b in zip(s_loc, s_ctx)]
        e_loc = [jnp.exp(a - m) for a, m in zip(s_loc, mx)]
        e_ctx = [jnp.exp(a - m) for a, m in zip(s_ctx, mx)]
        den = [rsum(a) + rsum(b) for a, b in zip(e_loc, e_ctx)]
        for n, (u, _, rs, ls) in enumerate(probs):
            o = _dot(e_loc[n].astype(BF16), v_ref[0, pl.ds(starts[u], win), ls])
            o = (o + _dot(e_ctx[n].astype(BF16), v_ref[0, 0:n_ctx, ls])) / den[n]
            o_ref[0, rs, ls] = _sel(o, m0).astype(BF16)


def _attention(qkv, bias_tab, n_ctx):
    B, T, _ = qkv.shape
    n_rows = (T - n_ctx) // GRID_W
    n_cstep = n_ctx // (GRID_W * ATT_ROWS)
    half = WIN_ROWS // 2
    blk = GRID_W * ATT_ROWS

    def delta(j, u):
        i = jnp.maximum(j - n_cstep, 0) * ATT_ROWS + u
        return jnp.minimum(i, half) + jnp.maximum(i - (n_rows - half), 0)

    bias_specs = [pl.BlockSpec((1, N_PAIRS, 128, WIN_ROWS * GRID_W), functools.partial(
        lambda b, j, u: (delta(j, u), 0, 0, 0), u=u)) for u in range(ATT_ROWS)]
    return pl.pallas_call(
        functools.partial(_attn_kernel, n_ctx=n_ctx, n_rows=n_rows),
        grid=(B, T // blk),
        in_specs=[
            pl.BlockSpec((1, blk, D_HEADS), lambda b, j: (b, j, 0)),
            pl.BlockSpec((1, T, D_HEADS), lambda b, j: (b, 0, 1)),
            pl.BlockSpec((1, T, D_HEADS), lambda b, j: (b, 0, 2)),
        ] + bias_specs,
        out_specs=pl.BlockSpec((1, blk, D_HEADS), lambda b, j: (b, j, 0)),
        out_shape=jax.ShapeDtypeStruct((B, T, D_HEADS), BF16),
        compiler_params=_cparams(("arbitrary", "arbitrary")),
    )(qkv, qkv, qkv, *([bias_tab] * ATT_ROWS))


def _na_bias_table(rpb):
    H = rpb.shape[0]
    c = np.arange(GRID_W)[:, None]
    kc = np.arange(GRID_W)[None, :]
    cs = np.clip(c - WIN_COLS // 2, 0, GRID_W - WIN_COLS)
    valid = (kc >= cs) & (kc < cs + WIN_COLS)
    cidx = np.clip(kc - c + (WIN_COLS - 1), 0, 2 * WIN_COLS - 2)
    t = jnp.where(valid[None, None], rpb[:, :, cidx], NEG)
    t = t.transpose(0, 2, 1, 3).astype(F32)
    tabs = [t[:, :, WIN_ROWS - 1 - dl:2 * WIN_ROWS - 1 - dl, :].reshape(H, GRID_W, WIN_ROWS * GRID_W)
            for dl in range(WIN_ROWS)]
    return jnp.stack(tabs, 0).reshape(WIN_ROWS, H // 2, 2 * GRID_W, WIN_ROWS * GRID_W)


def _layer_norm(h, g, b):
    mu = jnp.mean(h, axis=-1, keepdims=True)
    var = jnp.mean(jnp.square(h - mu), axis=-1, keepdims=True)
    return (h - mu) * lax.rsqrt(var + LN_EPS) * g + b


def _to_token_tiles(ref, val):
    n = val.shape[0]
    for j in range(TOK_SUB):
        ref[pl.ds(j, n, stride=TOK_SUB), :] = val[:, j * 128:(j + 1) * 128]


def _from_token_tiles(ref, n):
    return [ref[pl.ds(j, n, stride=TOK_SUB), :] for j in range(TOK_SUB)]


def _outproj_kernel(na_ref, rw_ref, w_ref, zc_ref, zx_ref, mod_ref, g_ref, b_ref, rw_w_ref,
                    z1_ref, hx_ref, lg_ref, *, d, alpha, n_ctx_tiles, t0):
    o = _dot(na_ref[0], w_ref[0:D_HEADS, :]) + _dot(rw_ref[0], w_ref[D_HEADS:2 * D_HEADS, :])
    mod = mod_ref[0, 0]
    gate = mod[:, 2 * d:3 * d]
    z = _z_tile(zc_ref, zx_ref, n_ctx_tiles, t0)
    z1 = _layer_norm(alpha * z + gate * o, g_ref[...], b_ref[...])
    z1_ref[0] = z1
    hx = z1 * (1.0 + mod[:, 4 * d:5 * d]) + mod[:, 3 * d:4 * d]
    _to_token_tiles(hx_ref, hx)
    lg_ref[0] = _dot3(hx, rw_w_ref[...])


def _outproj(na, rw, w_out_b, zc, zx, sub, modsel, ln_g, ln_b, router_wp, n_ctx_tiles, t0, alpha):
    B, T, _ = na.shape
    D = zc.shape[2]
    nt = T // ROW_TILE - t0
    half = pl.BlockSpec((1, ROW_TILE, D_HEADS), lambda b, t: (b, t + t0, 0))
    tile = pl.BlockSpec((1, ROW_TILE, D), lambda b, t: (b, t, 0))
    row = pl.BlockSpec((1, D), lambda b, t: (0, 0))
    cspec, xspec = _z_specs(D, n_ctx_tiles, sub, t0)
    return pl.pallas_call(
        functools.partial(_outproj_kernel, d=D, alpha=alpha, n_ctx_tiles=n_ctx_tiles, t0=t0),
        grid=(B, nt),
        in_specs=[
            half, half,
            pl.BlockSpec((2 * D_HEADS, D), lambda b, t: (0, 0)),
            cspec, xspec,
            _mod_spec(D, n_ctx_tiles, t0),
            row, row,
            pl.BlockSpec((D, 128), lambda b, t: (0, 0)),
        ],
        out_specs=[
            tile,
            pl.BlockSpec((ROW_TILE * TOK_SUB, 128), lambda b, t: (b * nt + t, 0)),
            pl.BlockSpec((1, ROW_TILE, 128), lambda b, t: (b, t, 0)),
        ],
        out_shape=[
            jax.ShapeDtypeStruct((B, nt * ROW_TILE, D), F32),
            jax.ShapeDtypeStruct((B * nt * ROW_TILE * TOK_SUB, 128), F32),
            jax.ShapeDtypeStruct((B, nt * ROW_TILE, 128), F32),
        ],
        compiler_params=_cparams(("arbitrary", "arbitrary")),
    )(na, rw, w_out_b, zc, zx, modsel, ln_g, ln_b, router_wp)


def _tok(ref, i):
    return ref.at[pl.ds(pl.multiple_of(i * TOK_SUB, TOK_SUB), TOK_SUB)]


def _dispatch_kernel(dest_ref, hx_ref, xs_in, xs_out, sem):
    del xs_in
    n = dest_ref.shape[0]

    def start(i, c):
        for par in range(2):
            r = 2 * i + par
            pltpu.make_async_copy(_tok(hx_ref, r), _tok(xs_out, dest_ref[r]), sem).start(priority=par)
        return c

    lax.fori_loop(0, n // 2, start, 0, unroll=4)

    def wait(r, c):
        pltpu.make_async_copy(_tok(hx_ref, 0), _tok(xs_out, 0), sem).wait()
        return c

    lax.fori_loop(0, n, wait, 0, unroll=8)


def _dispatch(dest, hx_tiles, n_rows_pad):
    N = dest.shape[0]
    tile = next(t for t in (4 * ROW_TILE, 2 * ROW_TILE, ROW_TILE) if N % t == 0)
    xs0 = jnp.zeros((n_rows_pad * TOK_SUB, 128), F32)
    return pl.pallas_call(
        _dispatch_kernel,
        grid=(N // tile,),
        in_specs=[
            pl.BlockSpec((tile,), lambda i: (i,), memory_space=pltpu.SMEM),
            pl.BlockSpec((tile * TOK_SUB, 128), lambda i: (i, 0)),
            pl.BlockSpec(memory_space=pl.ANY),
        ],
        out_specs=pl.BlockSpec(memory_space=pl.ANY),
        out_shape=jax.ShapeDtypeStruct((n_rows_pad * TOK_SUB, 128), F32),
        scratch_shapes=[pltpu.SemaphoreType.DMA(())],
        input_output_aliases={2: 0},
        compiler_params=_cparams(("arbitrary",)),
    )(dest, hx_tiles, xs0)


def _pack_bf16_pair(a, b):
    ha = lax.bitcast_convert_type(a.astype(BF16).astype(F32), jnp.uint32)
    hb = lax.bitcast_convert_type(b.astype(BF16).astype(F32), jnp.uint32)
    return jnp.bitwise_or(ha, jnp.right_shift(hb, jnp.uint32(16)))


def _unpack_bf16_pair(w):
    a = lax.bitcast_convert_type(jnp.bitwise_and(w, jnp.uint32(0xFFFF0000)), F32)
    b = lax.bitcast_convert_type(jnp.left_shift(w, jnp.uint32(16)), F32)
    return a, b


def _expert_kernel(ea_ref, eb_ref, valid_ref, xs_ref, w1a, w3a, w2a, w1b, w3b, w2b, ys_ref,
                   c1a, c3a, c2a, c1b, c3b, c2b):
    i = pl.program_id(0)
    prev = jnp.maximum(i - 1, 0)
    changed = jnp.logical_or(i == 0, jnp.logical_or(ea_ref[i] != ea_ref[prev], eb_ref[i] != eb_ref[prev]))

    @pl.when(changed)
    def _():
        c1a[...] = w1a[0, 0].astype(BF16)
        c3a[...] = w3a[0, 0].astype(BF16)
        c2a[...] = w2a[0, 0].astype(BF16)
        c1b[...] = w1b[0, 0].astype(BF16)
        c3b[...] = w3b[0, 0].astype(BF16)
        c2b[...] = w2b[0, 0].astype(BF16)

    @pl.when(valid_ref[i] != 0)
    def _():
        x = jnp.concatenate(_from_token_tiles(xs_ref, MOE_TILE), axis=1).astype(BF16)

        def ffn(c1, c3, c2):
            h1 = _dot(x, c1[...])
            h3 = _dot(x, c3[...])
            h = (h1 * jax.nn.sigmoid(h1)) * h3
            return _dot(h.astype(BF16), c2[...])

        _to_token_tiles(ys_ref, _pack_bf16_pair(ffn(c1a, c3a, c2a), ffn(c1b, c3b, c2b)))

    @pl.when(valid_ref[i] == 0)
    def _():
        ys_ref[...] = jnp.zeros_like(ys_ref)


def _experts(blk_ea, blk_eb, blk_valid, xs, w1, w3, w2, l):
    nb = xs.shape[0] // (MOE_TILE * TOK_SUB)
    _, _, D, DE = w1.shape
    wa = lambda shape: pl.BlockSpec(shape, lambda i, ea, eb, va: (l, ea[i], 0, 0))
    wb = lambda shape: pl.BlockSpec(shape, lambda i, ea, eb, va: (l, eb[i], 0, 0))
    tok = pl.BlockSpec((MOE_TILE * TOK_SUB, 128), lambda i, ea, eb, va: (i, 0))
    grid_spec = pltpu.PrefetchScalarGridSpec(
        num_scalar_prefetch=3,
        grid=(nb,),
        in_specs=[
            tok,
            wa((1, 1, D, DE)), wa((1, 1, D, DE)), wa((1, 1, DE, D)),
            wb((1, 1, D, DE)), wb((1, 1, D, DE)), wb((1, 1, DE, D)),
        ],
        out_specs=tok,
        scratch_shapes=[pltpu.VMEM((D, DE), BF16), pltpu.VMEM((D, DE), BF16), pltpu.VMEM((DE, D), BF16),
                        pltpu.VMEM((D, DE), BF16), pltpu.VMEM((D, DE), BF16), pltpu.VMEM((DE, D), BF16)],
    )
    return pl.pallas_call(
        _expert_kernel,
        grid_spec=grid_spec,
        out_shape=jax.ShapeDtypeStruct(xs.shape, jnp.uint32),
        compiler_params=_cparams(("arbitrary",)),
    )(blk_ea, blk_eb, blk_valid, xs, w1, w3, w2, w1, w3, w2)


def _combine_kernel(dest_ref, dnext_ref, ys_ref, gate_ref, z1_ref, mod_ref, g_ref, b_ref, o_ref, buf, sem,
                    *, d, alpha):
    n = dest_ref.shape[0]
    step = pl.program_id(0) * pl.num_programs(1) + pl.program_id(1)
    n_steps = pl.num_programs(0) * pl.num_programs(1)
    slot = step % 2

    def gather(idx_ref, sl):
        def start(i, c):
            for par in range(2):
                r = 2 * i + par
                pltpu.make_async_copy(_tok(ys_ref, idx_ref[r]), _tok(buf.at[sl], r), sem.at[sl]).start(priority=par)
            return c

        lax.fori_loop(0, n // 2, start, 0, unroll=4)

    @pl.when(step == 0)
    def _():
        gather(dest_ref, 0)

    @pl.when(step + 1 < n_steps)
    def _():
        gather(dnext_ref, 1 - slot)

    def wait(r, c):
        pltpu.make_async_copy(_tok(ys_ref, 0), _tok(buf.at[slot], 0), sem.at[slot]).wait()
        return c

    lax.fori_loop(0, n, wait, 0, unroll=8)
    ga = gate_ref[:, 0:1]
    gb = gate_ref[:, 1:2]
    parts = []
    for w in _from_token_tiles(buf.at[slot], n):
        fa, fb = _unpack_bf16_pair(w)
        parts.append(ga * fa + gb * fb)
    y = jnp.concatenate(parts, axis=1)
    gate = mod_ref[0, 0][:, 5 * d:6 * d]
    o_ref[0] = _layer_norm(alpha * z1_ref[0] + gate * y, g_ref[...], b_ref[...])


def _combine(dest, ys, gates, z1, modsel, ln_g, ln_b, n_ctx_tiles, t0, t_out, alpha):
    B, T1, D = z1.shape
    nt1 = T1 // ROW_TILE
    skip = t_out - t0
    nt = nt1 - skip
    tile = lambda off: pl.BlockSpec((1, ROW_TILE, D), lambda b, t: (b, t + off, 0))
    row = pl.BlockSpec((1, D), lambda b, t: (0, 0))

    def nxt(b, t):
        last = jnp.logical_and(b == B - 1, t == nt - 1)
        wrap = t == nt - 1
        b2 = jnp.where(jnp.logical_and(wrap, jnp.logical_not(last)), b + 1, b)
        t2 = jnp.where(last, t, jnp.where(wrap, 0, t + 1))
        return b2 * nt1 + t2 + skip

    return pl.pallas_call(
        functools.partial(_combine_kernel, d=D, alpha=alpha),
        grid=(B, nt),
        in_specs=[
            pl.BlockSpec((ROW_TILE,), lambda b, t: (b * nt1 + t + skip,), memory_space=pltpu.SMEM),
            pl.BlockSpec((ROW_TILE,), lambda b, t: (nxt(b, t),), memory_space=pltpu.SMEM),
            pl.BlockSpec(memory_space=pl.ANY),
            pl.BlockSpec((ROW_TILE, 128), lambda b, t: (b * nt1 + t + skip, 0)),
            tile(skip),
            _mod_spec(D, n_ctx_tiles, t_out),
            row, row,
        ],
        out_specs=tile(0),
        out_shape=jax.ShapeDtypeStruct((B, nt * ROW_TILE, D), F32),
        scratch_shapes=[pltpu.VMEM((2, ROW_TILE * TOK_SUB, 128), jnp.uint32), pltpu.SemaphoreType.DMA((2,))],
        compiler_params=_cparams(("arbitrary", "arbitrary")),
    )(dest, dest, ys, gates, z1, modsel, ln_g, ln_b)


_PAIR_LO = np.array([0, 0, 0, 1, 1, 2], np.int32)
_PAIR_HI = np.array([1, 2, 3, 2, 3, 3], np.int32)


def _route(logits, router_bias):
    N = logits.shape[0]
    scores = jax.nn.sigmoid(logits[:, :N_EXPERTS])
    sel = (scores + router_bias.astype(F32)).reshape(N, N_GROUPS, EXPERTS_PER_GROUP)
    pos = jnp.arange(EXPERTS_PER_GROUP, dtype=jnp.int32)
    a1 = jnp.argmax(sel, axis=-1)
    m1 = jnp.max(sel, axis=-1)
    rest = jnp.where(pos == a1[..., None], -jnp.inf, sel)
    m2 = jnp.max(rest, axis=-1)
    g_idx = jnp.argmax(m1 + m2, axis=-1).astype(jnp.int32)
    in_g = (jnp.arange(N_GROUPS, dtype=jnp.int32)[None, :] == g_idx[:, None])[..., None]
    sel_g = jnp.sum(jnp.where(in_g, sel, 0.0), axis=1)
    i1 = jnp.argmax(sel_g, axis=-1).astype(jnp.int32)
    i2 = jnp.argmax(jnp.where(pos == i1[:, None], -jnp.inf, sel_g), axis=-1).astype(jnp.int32)
    sc_g = jnp.sum(jnp.where(in_g, scores.reshape(N, N_GROUPS, EXPERTS_PER_GROUP), 0.0), axis=1)
    lo = jnp.minimum(i1, i2)
    hi = jnp.maximum(i1, i2)
    g_lo = jnp.sum(jnp.where(pos == lo[:, None], sc_g, 0.0), axis=1)
    g_hi = jnp.sum(jnp.where(pos == hi[:, None], sc_g, 0.0), axis=1)
    tot = g_lo + g_hi
    g_lo = g_lo / tot
    g_hi = g_hi / tot
    pair = lo * 3 - (lo * (lo - 1)) // 2 + (hi - lo - 1)
    cls = g_idx * 6 + pair

    onehot = (cls[:, None] == jnp.arange(N_CLASSES, dtype=jnp.int32)[None, :])
    oh = onehot.astype(BF16).reshape(N // ROW_TILE, ROW_TILE, N_CLASSES)
    tri = jnp.tril(jnp.ones((ROW_TILE, ROW_TILE), BF16), -1)
    within = jnp.einsum("rs,tsc->trc", tri, oh, preferred_element_type=F32)
    tile_cnt = jnp.sum(oh.astype(F32), axis=1)
    tile_off = jnp.cumsum(tile_cnt, axis=0) - tile_cnt
    rank_all = (within + tile_off[:, None, :]).reshape(N, N_CLASSES)
    rank = jnp.sum(jnp.where(onehot, rank_all, 0.0), axis=1).astype(jnp.int32)
    counts = jnp.sum(tile_cnt, axis=0).astype(jnp.int32)
    padded = (counts + MOE_TILE - 1) // MOE_TILE * MOE_TILE
    cls_end = jnp.cumsum(padded)
    cls_start = cls_end - padded
    dest = jnp.sum(jnp.where(onehot, cls_start[None, :], 0), axis=1) + rank

    nb = N // MOE_TILE + N_CLASSES
    blk_row = jnp.arange(nb, dtype=jnp.int32) * MOE_TILE
    total = jnp.sum(padded)
    blk_valid = (blk_row < total).astype(jnp.int32)
    row_c = jnp.minimum(blk_row, jnp.maximum(total - MOE_TILE, 0))
    blk_cls = jnp.sum((row_c[:, None] >= cls_end[None, :]).astype(jnp.int32), axis=1)
    blk_cls = jnp.minimum(blk_cls, N_CLASSES - 1)
    blk_grp = blk_cls // 6
    pair_hot = (blk_cls % 6)[:, None] == jnp.arange(6, dtype=jnp.int32)[None, :]
    blk_ea = blk_grp * EXPERTS_PER_GROUP + jnp.sum(jnp.where(pair_hot, jnp.asarray(_PAIR_LO)[None, :], 0), axis=1)
    blk_eb = blk_grp * EXPERTS_PER_GROUP + jnp.sum(jnp.where(pair_hot, jnp.asarray(_PAIR_HI)[None, :], 0), axis=1)
    gates = jnp.pad(jnp.stack([g_lo, g_hi], axis=1), ((0, 0), (0, 126)))
    return dest.astype(jnp.int32), gates, blk_ea.astype(jnp.int32), blk_eb.astype(jnp.int32), blk_valid, nb * MOE_TILE


def _rope_tables(n_ctx, seq):
    t = np.arange(seq)
    row = (t // GRID_W).astype(np.float32)
    col = (t % GRID_W).astype(np.float32)
    n_freq = HEAD_DIM // 4
    inv = jnp.asarray(ROPE_BASE, F32) ** (-jnp.arange(n_freq, dtype=F32) / n_freq)
    ar = jnp.asarray(row)[:, None] * inv
    ac = jnp.asarray(col)[:, None] * inv
    ang = jnp.concatenate([ar, ar, ac, ac], -1)
    cos = jnp.cos(ang)
    sin = jnp.sin(ang)
    quarter = (np.arange(HEAD_DIM) // n_freq) % 2
    sa = jnp.where(quarter == 0, -sin, 0.0)
    sb = jnp.where(quarter == 1, sin, 0.0)
    pad = lambda a, fill: jnp.concatenate([jnp.full((n_ctx, HEAD_DIM), fill, F32), a], 0)
    tile = lambda a: jnp.tile(a, (1, N_HEADS))
    return tile(pad(cos, 1.0)), tile(pad(sa, 0.0)), tile(pad(sb, 0.0))


def _rw_consts(l, n_ctx, seq, rw_mu_prev, rw_mu_next, rw_w0, rw_w2, rw_a0, rw_a2, rw_g2, rw_k_k, rw_k_a, rw_r_k):
    d_rw_in = rw_mu_prev.shape[1]
    padw = lambda a: jnp.pad(a[l], (0, D_RWP - d_rw_in)).reshape(1, D_RWP)
    cos, sa, sb = _rope_tables(n_ctx, seq)
    w2p = jnp.zeros((128, 2 * D_HEADS), F32)
    w2p = w2p.at[0:LORA, 0:D_HEADS].set(rw_w2[l, 0]).at[LORA:2 * LORA, D_HEADS:].set(rw_w2[l, 1])
    a2p = jnp.zeros((128, 2 * D_HEADS), F32)
    a2p = a2p.at[2 * LORA:3 * LORA, 0:D_HEADS].set(rw_a2[l, 0]).at[3 * LORA:4 * LORA, D_HEADS:].set(rw_a2[l, 1])
    g2p = jnp.zeros((128, D_HEADS), F32).at[0:GATE_LORA].set(rw_g2[l])
    head = np.arange(D_HEADS) // HEAD_DIM
    ones_bd = jnp.asarray(head[:, None] == head[None, :], BF16)
    return dict(
        mu_prev=padw(rw_mu_prev), mu_next=padw(rw_mu_next), cos=cos, sa=sa, sb=sb,
        k_k=rw_k_k[l].reshape(1, D_HEADS), k_a=rw_k_a[l].reshape(1, D_HEADS), r_k=rw_r_k[l].reshape(1, D_HEADS),
        w0=rw_w0[l].reshape(1, 2 * D_HEADS), a0=rw_a0[l].reshape(1, 2 * D_HEADS),
        w2p=w2p.astype(BF16), a2p=a2p.astype(BF16), g2p=g2p.astype(BF16), ones_bd=ones_bd,
    )


def kernel(x, c, ctx, c_ctx, ada_w, ada_b, w_in, na_rpb, rw_mu_prev, rw_mu_next, rw_w0, rw_w2, rw_a0, rw_a2, rw_g2, rw_k_k, rw_k_a, rw_r_k, rw_gn_g, rw_gn_b, w_out, ln1_g, ln1_b, ln2_g, ln2_b, router_w, router_bias, exp_w1, exp_w3, exp_w2):
    B, S, D = x.shape
    C = ctx.shape[1]
    L = ada_w.shape[0]
    T = C + S
    assert D == 1024 and C % ROW_TILE == 0 and S % ROW_TILE == 0 and C % (2 * CHUNK) == 0
    assert S % GRID_W == 0 and S // GRID_W >= WIN_ROWS and w_in.shape[2] == D_INP - 32
    n_ctx_tiles = C // ROW_TILE
    alpha = float((2 * L) ** 0.25)

    zc, zx, sub = ctx, x, n_ctx_tiles
    n_mod = (B + 1 + 7) // 8 * 8
    cc = jnp.zeros((n_mod, D), F32).at[0:B].set(c).at[B].set(c_ctx)
    mod_all = _ada(cc, ada_w, ada_b)
    router_wp = jnp.pad(router_w, ((0, 0), (0, 128 - N_EXPERTS)))

    for l in range(L):
        mod_c = jnp.broadcast_to(mod_all[l, B][None], (B, 6 * D))
        modsel = jnp.stack([mod_c, mod_all[l, 0:B]], axis=1).reshape(B, 2, 1, 6 * D)
        w_in_p = jnp.pad(w_in[l], ((0, 0), (0, D_INP - w_in.shape[2]))).astype(BF16)
        qkv, p_rw = _inproj(zc, zx, sub, T, modsel, w_in_p, n_ctx_tiles)

        consts = _rw_consts(l, C, S, rw_mu_prev, rw_mu_next, rw_w0, rw_w2, rw_a0, rw_a2, rw_g2,
                            rw_k_k, rw_k_a, rw_r_k)
        prep = _rwprep(p_rw, consts, n_ctx_tiles)
        yf, yb = _wkv(prep[0:15], C)
        rw = _rwpost(yf, yb, prep[15], prep[16], rw_gn_g[l].reshape(1, D_HEADS), rw_gn_b[l].reshape(1, D_HEADS))

        na = _attention(qkv, _na_bias_table(na_rpb[l]), C)

        t0 = n_ctx_tiles if l == L - 1 else 0
        z1, hx_tiles, logits = _outproj(na, rw, w_out[l].astype(BF16), zc, zx, sub, modsel,
                                        ln1_g[l].reshape(1, D), ln1_b[l].reshape(1, D), router_wp,
                                        n_ctx_tiles, t0, alpha)

        dest, gates, blk_ea, blk_eb, blk_valid, n_rows_pad = _route(logits.reshape(-1, 128), router_bias)
        xs = _dispatch(dest, hx_tiles, n_rows_pad)
        ys = _experts(blk_ea, blk_eb, blk_valid, xs, exp_w1, exp_w3, exp_w2, l)
        z = _combine(dest, ys, gates, z1, modsel, ln2_g[l].reshape(1, D), ln2_b[l].reshape(1, D),
                     n_ctx_tiles, t0, t0, alpha)
        zc, zx, sub = z, z, 0

    return z
```

```python
import functools
import math

import jax
import jax.numpy as jnp
import numpy as np
from jax import lax
from jax.experimental import pallas as pl
from jax.experimental.pallas import tpu as pltpu

F32 = jnp.float32
BF16 = jnp.bfloat16

HEAD_DIM = 64
N_HEADS = 8
D_HEADS = N_HEADS * HEAD_DIM
N_PAIRS = N_HEADS // 2
GRID_W = 64
WIN_ROWS = 8
WIN_COLS = 16
LORA = 32
GATE_LORA = 96
N_EXPERTS = 32
N_GROUPS = 8
EXPERTS_PER_GROUP = 4
N_CLASSES = N_GROUPS * 6
ROPE_BASE = 10000.0
LN_EPS = 1e-6
GN_EPS = 64e-5
CHUNK = 64
ROW_TILE = 256
MOE_TILE = 256
D_QKV = 3 * D_HEADS
D_RWP = 3 * D_HEADS + 256
D_INP = D_QKV + D_RWP
TOK_SUB = 8
NEG = -1e30
VMEM_LIMIT = 56 * 1024 * 1024


def _cparams(sem):
    return pltpu.CompilerParams(dimension_semantics=sem, vmem_limit_bytes=VMEM_LIMIT)


def _dot(a, b):
    return jnp.dot(a, b, preferred_element_type=F32)


def _dot_nt(a, b):
    return lax.dot_general(a, b, (((1,), (1,)), ((), ())), preferred_element_type=F32)


def _split2(a):
    hi = a.astype(BF16)
    lo = (a - hi.astype(F32)).astype(BF16)
    return hi, lo


def _split3(a):
    hi = a.astype(BF16)
    r1 = a - hi.astype(F32)
    mid = r1.astype(BF16)
    lo = (r1 - mid.astype(F32)).astype(BF16)
    return hi, mid, lo


def _dot3(a, b):
    ah, al = _split2(a)
    bh, bl = _split2(b)
    return _dot(ah, bh) + _dot(al, bh) + _dot(ah, bl)


def _dot_exact_rhs(a, b_exact):
    h, m, l = _split3(a)
    return _dot(h, b_exact) + _dot(m, b_exact) + _dot(l, b_exact)


def _dot_exact_lhs(a_exact, b):
    h, m, l = _split3(b)
    return _dot(a_exact, h) + _dot(a_exact, m) + _dot(a_exact, l)


def _ada_kernel(cc_ref, w_ref, b_ref, o_ref):
    cc = cc_ref[...]
    s = cc * jax.nn.sigmoid(cc)
    o_ref[0] = _dot3(s, w_ref[0]) + b_ref[0]


def _ada(cc, ada_w, ada_b):
    L, D, D6 = ada_w.shape
    R = cc.shape[0]
    tn = 1536
    return pl.pallas_call(
        _ada_kernel,
        grid=(L, D6 // tn),
        in_specs=[
            pl.BlockSpec((R, D), lambda l, n: (0, 0)),
            pl.BlockSpec((1, D, tn), lambda l, n: (l, 0, n)),
            pl.BlockSpec((1, 1, tn), lambda l, n: (l, 0, n)),
        ],
        out_specs=pl.BlockSpec((1, R, tn), lambda l, n: (l, 0, n)),
        out_shape=jax.ShapeDtypeStruct((L, R, D6), F32),
        compiler_params=_cparams(("arbitrary", "arbitrary")),
    )(cc, ada_w, ada_b.reshape(L, 1, D6))


def _z_specs(D, n_ctx_tiles, sub, t0=0):
    cspec = pl.BlockSpec((1, ROW_TILE, D), lambda b, t: (b, jnp.minimum(t + t0, n_ctx_tiles - 1), 0))
    xspec = pl.BlockSpec((1, ROW_TILE, D), lambda b, t: (b, jnp.maximum(t + t0, n_ctx_tiles) - sub, 0))
    return cspec, xspec


def _z_tile(zc_ref, zx_ref, n_ctx_tiles, t0=0):
    return jnp.where(pl.program_id(1) + t0 < n_ctx_tiles, zc_ref[0], zx_ref[0])


def _mod_spec(D, n_ctx_tiles, t0=0):
    return pl.BlockSpec((1, 1, 1, 6 * D), lambda b, t: (b, jnp.where(t + t0 < n_ctx_tiles, 0, 1), 0, 0))


def _inproj_kernel(zc_ref, zx_ref, mod_ref, w_ref, qkv_ref, rw_ref, *, d, n_ctx_tiles):
    z = _z_tile(zc_ref, zx_ref, n_ctx_tiles)
    mod = mod_ref[0, 0]
    shift = mod[:, 0:d]
    scale = mod[:, d:2 * d]
    h = (z * (1.0 + scale) + shift).astype(BF16)
    q = _dot(h, w_ref[:, 0:D_HEADS])
    qkv_ref[0, :, 0:D_HEADS] = (q * (HEAD_DIM ** -0.5)).astype(BF16)
    kv = _dot(h, w_ref[:, D_HEADS:D_QKV])
    qkv_ref[0, :, D_HEADS:D_QKV] = kv.astype(BF16)
    rw_ref[0] = _dot(h, w_ref[:, D_QKV:D_INP])


def _inproj(zc, zx, sub, T, modsel, w_in_p, n_ctx_tiles):
    B, _, D = zc.shape
    nt = T // ROW_TILE
    cspec, xspec = _z_specs(D, n_ctx_tiles, sub)
    return pl.pallas_call(
        functools.partial(_inproj_kernel, d=D, n_ctx_tiles=n_ctx_tiles),
        grid=(B, nt),
        in_specs=[
            cspec, xspec,
            pl.BlockSpec((1, 1, 1, 6 * D), lambda b, t: (b, jnp.where(t < n_ctx_tiles, 0, 1), 0, 0)),
            pl.BlockSpec((D, D_INP), lambda b, t: (0, 0)),
        ],
        out_specs=[
            pl.BlockSpec((1, ROW_TILE, D_QKV), lambda b, t: (b, t, 0)),
            pl.BlockSpec((1, ROW_TILE, D_RWP), lambda b, t: (b, t, 0)),
        ],
        out_shape=[
            jax.ShapeDtypeStruct((B, T, D_QKV), BF16),
            jax.ShapeDtypeStruct((B, T, D_RWP), F32),
        ],
        compiler_params=_cparams(("arbitrary", "arbitrary")),
    )(zc, zx, modsel, w_in_p)


def _rwprep_kernel(p_ref, pp_ref, pn_ref, mup_ref, mun_ref, cos_ref, sa_ref, sb_ref,
                   kk_ref, ka_ref, rk_ref, w0_ref, a0_ref, w2_ref, a2_ref, g2_ref, ones_ref,
                   at_f, bt_f, kt_f, rt_f, bh_f, kh_f, pe_f,
                   at_b, bt_b, kt_b, rt_b, bh_b, kh_b, pe_b,
                   vt_ref, bonus_ref, g_ref, *, n_ctx_tiles, n_tiles):
    t = pl.program_id(1)
    P = p_ref[0]
    R = P.shape[0]
    prev_ok = jnp.logical_and(t != 0, t != n_ctx_tiles)
    next_ok = jnp.logical_and(t != n_ctx_tiles - 1, t != n_tiles - 1)
    prev_row = jnp.where(prev_ok, pp_ref[0, 7:8, :], 0.0)
    next_row = jnp.where(next_ok, pn_ref[0, 0:1, :], 0.0)
    row = lax.broadcasted_iota(jnp.int32, (R, 1), 0)
    prev = jnp.where(row == 0, prev_row, pltpu.roll(P, 1, axis=0))
    nxt = jnp.where(row == R - 1, next_row, pltpu.roll(P, R - 1, axis=0))
    mup = mup_ref[...]
    mun = mun_ref[...]
    z = (1.0 - mup - mun) * P + mup * prev + mun * nxt

    cos = cos_ref[...]
    sa = sa_ref[...]
    sb = sb_ref[...]

    def rope(u):
        q = HEAD_DIM // 4
        parts = []
        for m in range(D_HEADS // 128):
            ls = slice(m * 128, (m + 1) * 128)
            um = u[:, ls]
            parts.append(um * cos[:, ls] + pltpu.roll(um, 128 - q, axis=1) * sa[:, ls]
                         + pltpu.roll(um, q, axis=1) * sb[:, ls])
        return jnp.concatenate(parts, axis=1)

    r = rope(z[:, 0:D_HEADS])
    k = rope(z[:, D_HEADS:2 * D_HEADS])
    v = z[:, 2 * D_HEADS:3 * D_HEADS]
    ones_bd = ones_ref[...]

    kk = k * kk_ref[...]
    sq_hi, sq_lo = _split2(kk * kk)
    kk = kk * lax.rsqrt(jnp.maximum(_dot(sq_hi, ones_bd) + _dot(sq_lo, ones_bd), 1e-24))

    sigmoid = lambda u: 0.5 * jnp.tanh(0.5 * u) + 0.5
    slab = z[:, 3 * D_HEADS:3 * D_HEADS + 128]
    u_w = w0_ref[...] + _dot(jnp.tanh(slab).astype(BF16), w2_ref[...])
    u_a = a0_ref[...] + _dot(slab.astype(BF16), a2_ref[...])
    g_ref[0] = _dot(sigmoid(z[:, 3 * D_HEADS + 128:D_RWP]).astype(BF16), g2_ref[...]).astype(BF16)
    e_all = math.exp(-0.5) * sigmoid(u_w)
    a_all = sigmoid(u_a)

    ci = lax.broadcasted_iota(jnp.int32, (CHUNK, CHUNK), 0)
    cj = lax.broadcasted_iota(jnp.int32, (CHUNK, CHUNK), 1)
    ka = ka_ref[...]
    outs = ((at_f, bt_f, kt_f, rt_f, bh_f, kh_f, pe_f), (at_b, bt_b, kt_b, rt_b, bh_b, kh_b, pe_b))
    kd_sum = None
    for d in range(2):
        e = e_all[:, d * D_HEADS:(d + 1) * D_HEADS]
        a = a_all[:, d * D_HEADS:(d + 1) * D_HEADS]
        tri = (cj <= ci) if d == 0 else (cj >= ci)
        tri = tri.astype(BF16)
        cs, ce = [], []
        for q in range(R // CHUNK):
            e_hi, e_lo = _split2(e[q * CHUNK:(q + 1) * CHUNK])
            cq = -(_dot(tri, e_hi) + _dot(tri, e_lo))
            end = cq[CHUNK - 1:CHUNK] if d == 0 else cq[0:1]
            cs.append(cq)
            ce.append(jnp.broadcast_to(end, cq.shape))
        c = jnp.concatenate(cs, axis=0)
        cend = jnp.concatenate(ce, axis=0)
        kd = k * (1.0 + (a - 1.0) * ka)
        kd_sum = kd if kd_sum is None else kd_sum + kd
        beta = a * kk
        en = jnp.exp(-c)
        eh = jnp.exp(cend - c)
        o_at, o_bt, o_kt, o_rt, o_bh, o_kh, o_pe = outs[d]
        o_at[0] = (-kk * jnp.exp(c + e)).astype(BF16)
        o_bt[0] = (beta * en).astype(BF16)
        o_kt[0] = (kd * en).astype(BF16)
        o_rt[0] = (r * jnp.exp(c)).astype(BF16)
        o_bh[0] = (beta * eh).astype(BF16)
        o_kh[0] = (kd * eh).astype(BF16)
        for q in range(R // CHUNK):
            o_pe[0, q] = jnp.exp(cend[q * CHUNK:q * CHUNK + 8])
    bonus_ref[0] = (_dot((r * rk_ref[...] * kd_sum).astype(BF16), ones_bd) * v).astype(BF16)
    vt_ref[0] = v.T.astype(BF16)


def _rwprep(p_rw, consts, n_ctx_tiles):
    B, T, _ = p_rw.shape
    nt = T // ROW_TILE
    nh = ROW_TILE // 8
    row = lambda w: pl.BlockSpec((1, w), lambda b, t: (0, 0))
    full = lambda a: pl.BlockSpec(a.shape, lambda b, t: (0, 0))
    tm = pl.BlockSpec((1, ROW_TILE, D_HEADS), lambda b, t: (b, t, 0))
    tab = pl.BlockSpec((ROW_TILE, D_HEADS), lambda b, t: (t, 0))
    pe = pl.BlockSpec((1, ROW_TILE // CHUNK, 8, D_HEADS), lambda b, t: (b, t, 0, 0))
    tm_shape = jax.ShapeDtypeStruct((B, T, D_HEADS), BF16)
    pe_shape = jax.ShapeDtypeStruct((B, T // CHUNK, 8, D_HEADS), F32)
    dir_specs = [tm] * 6 + [pe]
    dir_shapes = [tm_shape] * 6 + [pe_shape]
    return pl.pallas_call(
        functools.partial(_rwprep_kernel, n_ctx_tiles=n_ctx_tiles, n_tiles=nt),
        grid=(B, nt),
        in_specs=[
            pl.BlockSpec((1, ROW_TILE, D_RWP), lambda b, t: (b, t, 0)),
            pl.BlockSpec((1, 8, D_RWP), lambda b, t: (b, jnp.maximum(t * nh - 1, 0), 0)),
            pl.BlockSpec((1, 8, D_RWP), lambda b, t: (b, jnp.minimum((t + 1) * nh, T // 8 - 1), 0)),
            row(D_RWP), row(D_RWP), tab, tab, tab,
            row(D_HEADS), row(D_HEADS), row(D_HEADS), row(2 * D_HEADS), row(2 * D_HEADS),
            full(consts["w2p"]), full(consts["a2p"]), full(consts["g2p"]), full(consts["ones_bd"]),
        ],
        out_specs=dir_specs + dir_specs + [
            pl.BlockSpec((1, D_HEADS, ROW_TILE), lambda b, t: (b, 0, t)),
            tm, tm,
        ],
        out_shape=dir_shapes + dir_shapes + [
            jax.ShapeDtypeStruct((B, D_HEADS, T), BF16),
            jax.ShapeDtypeStruct((B, T, D_HEADS), BF16),
            jax.ShapeDtypeStruct((B, T, D_HEADS), BF16),
        ],
        compiler_params=_cparams(("arbitrary", "arbitrary")),
    )(p_rw, p_rw, p_rw, consts["mu_prev"], consts["mu_next"], consts["cos"], consts["sa"], consts["sb"],
      consts["k_k"], consts["k_a"], consts["r_k"], consts["w0"], consts["a0"],
      consts["w2p"], consts["a2p"], consts["g2p"], consts["ones_bd"])


def _bd(y, m0):
    zero = jnp.zeros_like(y)
    return jnp.concatenate([jnp.where(m0, y, zero), jnp.where(m0, zero, y)], axis=0)


def _sel(w, m0):
    return jnp.where(m0, w[0:CHUNK], w[CHUNK:2 * CHUNK])


def _wkv_kernel(*refs):
    (at_f, bt_f, kt_f, rt_f, bh_f, kh_f, pe_f, vt_f,
     at_b, bt_b, kt_b, rt_b, bh_b, kh_b, pe_b, vt_b,
     yf_ref, yb_ref, s_ref) = refs
    s = pl.program_id(1)

    @pl.when(s == 0)
    def _():
        s_ref[...] = jnp.zeros_like(s_ref)

    lane = lax.broadcasted_iota(jnp.int32, (CHUNK, 2 * CHUNK), 1)
    rowi = lax.broadcasted_iota(jnp.int32, (CHUNK, 2 * CHUNK), 0)
    lm = jnp.bitwise_and(lane, CHUNK - 1)
    m0 = lane < CHUNK
    dirs = ((at_f, bt_f, kt_f, rt_f, bh_f, kh_f, pe_f, vt_f, yf_ref),
            (at_b, bt_b, kt_b, rt_b, bh_b, kh_b, pe_b, vt_b, yb_ref))
    masks = (((rowi < lm), (rowi <= lm)), ((rowi > lm), (rowi >= lm)))
    zero = jnp.zeros((CHUNK, 2 * CHUNK), F32)
    bd = lambda y: _bd(y, m0)
    bf = lambda y: y.astype(BF16)

    lane2 = lax.broadcasted_iota(jnp.int32, (2 * CHUNK, 2 * CHUNK), 1)
    cat2 = lambda y: jnp.concatenate([y, y], axis=0)
    probs = []
    for rnd in range(WKV_CPS):
        for d in range(2):
            for p in range(N_PAIRS):
                ck = rnd if d == 0 else WKV_CPS - 1 - rnd
                probs.append(dict(d=d, p=p, ck=ck, rnd=rnd,
                                  rs=slice(ck * CHUNK, (ck + 1) * CHUNK), ls=slice(p * 128, (p + 1) * 128)))

    def ld(pr, i):
        return dirs[pr["d"]][i][0, pr["rs"], pr["ls"]]

    def rhs1(pr):
        return jnp.concatenate([bd(ld(pr, 0)), bd(ld(pr, 3))], axis=0)

    for pr in probs:
        G = _dot_nt(jnp.concatenate([ld(pr, 1), ld(pr, 2)], axis=0), rhs1(pr))
        strict, incl = masks[pr["d"]]
        pr["N"] = jnp.where(strict, G[0:CHUNK, 0:128], zero)
        pr["N_br"] = bf(jnp.where(incl, G[0:CHUNK, 128:256], zero))
        pr["A_ak"] = bf(jnp.where(strict, G[CHUNK:128, 0:128], zero))
        pr["N_kr"] = bf(jnp.where(incl, G[CHUNK:128, 128:256], zero))
    for pr in probs:
        Ab = bf(pr["N"])
        pr["M"] = _dot(Ab, bd(Ab))
    for _ in range(4):
        for pr in probs:
            Mb = bf(pr["M"])
            Rm = _dot(jnp.concatenate([bf(pr["N"]), Mb], axis=0), bd(Mb))
            pr["N"] = pr["N"] + pr["M"] + Rm[0:CHUNK]
            pr["M"] = Rm[CHUNK:2 * CHUNK]
    for pr in probs:
        pr["N"] = bf(pr["N"] + pr["M"] + _dot(bf(pr["N"]), bd(bf(pr["M"]))))
        del pr["M"]
    for pr in probs:
        bh = ld(pr, 4)
        NZ = _dot(pr["N"], jnp.concatenate([bd(pr["N_br"]), bd(bh)], axis=1))
        z_br = bf(pr["N_br"].astype(F32) + NZ[:, 0:128])
        z_bh = bf(bh.astype(F32) + NZ[:, 128:256])
        pr["Z"] = jnp.concatenate([bd(z_br), bd(z_bh)], axis=1)
    for pr in probs:
        tile, half = pr["ck"] // 2, pr["ck"] % 2
        vt_p = dirs[pr["d"]][7][0, pr["ls"], tile * 128:(tile + 1) * 128]
        in_half = (lane2 < CHUNK) if half == 0 else (lane2 >= CHUNK)
        vtm = jnp.where(in_half, vt_p, jnp.zeros_like(vt_p))
        VG = _dot(vtm, jnp.concatenate([cat2(pr["A_ak"]), cat2(pr["N_kr"]), cat2(ld(pr, 5))], axis=1))
        pr["VA"] = _sel(VG[:, 0:128], m0)
        pr["VN"] = _sel(VG[:, 128:256], m0)
        pr["VK"] = _sel(VG[:, 256:384], m0)

    S = {(d, p): s_ref[d, p] for d in range(2) for p in range(N_PAIRS)}
    ys = {}
    for rnd in range(WKV_CPS):
        cur = [pr for pr in probs if pr["rnd"] == rnd]
        for pr in cur:
            St = S[(pr["d"], pr["p"])]
            SG = _dot_nt(bf(St), rhs1(pr))
            pr["X"] = bf(SG[:, 0:128] + pr["VA"])
            pr["Y"] = SG[:, 128:256] + pr["VN"]
        for pr in cur:
            UG = _dot(pr["X"], pr["Z"])
            key = (pr["d"], pr["p"])
            pend = dirs[pr["d"]][6][0, pr["ck"], 0:1, pr["ls"]]
            S[key] = S[key] * pend + UG[:, 128:256] + pr["VK"]
            ys[(pr["d"], pr["p"], pr["ck"])] = pr["Y"] + UG[:, 0:128]
    for d in range(2):
        y_ref = dirs[d][8]
        for p in range(N_PAIRS):
            s_ref[d, p] = S[(d, p)]
            for tile in range(WKV_CPS // 2):
                y0, y1 = ys[(d, p, 2 * tile)], ys[(d, p, 2 * tile + 1)]
                ts = slice(tile * 128, (tile + 1) * 128)
                y_ref[0, p * 128:p * 128 + CHUNK, ts] = jnp.where(m0, y0, pltpu.roll(y1, CHUNK, axis=1)).astype(BF16)
                y_ref[0, p * 128 + CHUNK:(p + 1) * 128, ts] = jnp.where(m0, pltpu.roll(y0, CHUNK, axis=1), y1).astype(BF16)


def _wkv(prep, n_ctx):
    (at_f, bt_f, kt_f, rt_f, bh_f, kh_f, pe_f, at_b, bt_b, kt_b, rt_b, bh_b, kh_b, pe_b, vt) = prep
    B, T, _ = at_f.shape
    blk = WKV_CPS * CHUNK
    assert T % blk == 0 and n_ctx % blk == 0
    ns = T // blk
    nc2 = n_ctx // blk

    def mrev(s):
        return jnp.where(s < nc2, nc2 - 1 - s, ns - 1 - (s - nc2))

    def specs(idx):
        tm = pl.BlockSpec((1, blk, D_HEADS), lambda b, s: (b, idx(s), 0))
        pe = pl.BlockSpec((1, WKV_CPS, 8, D_HEADS), lambda b, s: (b, idx(s), 0, 0))
        vts = pl.BlockSpec((1, D_HEADS, blk), lambda b, s: (b, 0, idx(s)))
        return [tm] * 6 + [pe, vts]

    fwd = lambda s: s
    yt = lambda idx: pl.BlockSpec((1, D_HEADS, blk), lambda b, s: (b, 0, idx(s)))
    return pl.pallas_call(
        _wkv_kernel,
        grid=(B, ns),
        in_specs=specs(fwd) + specs(mrev),
        out_specs=[yt(fwd), yt(mrev)],
        out_shape=[jax.ShapeDtypeStruct((B, D_HEADS, T), BF16)] * 2,
        scratch_shapes=[pltpu.VMEM((2, N_PAIRS, CHUNK, 2 * CHUNK), F32)],
        compiler_params=_cparams(("arbitrary", "arbitrary")),
    )(at_f, bt_f, kt_f, rt_f, bh_f, kh_f, pe_f, vt, at_b, bt_b, kt_b, rt_b, bh_b, kh_b, pe_b, vt)


def _rwpost_kernel(yf_ref, yb_ref, bonus_ref, g_ref, gg_ref, gb_ref, o_ref):
    y = yf_ref[0].astype(F32) + yb_ref[0].astype(F32)
    R = y.shape[1]
    y3 = y.reshape(N_HEADS, HEAD_DIM, R)
    mu = jnp.mean(y3, axis=1, keepdims=True)
    var = jnp.mean(jnp.square(y3 - mu), axis=1, keepdims=True)
    yn = ((y3 - mu) * lax.rsqrt(var + GN_EPS)).reshape(D_HEADS, R)
    out = (yn.T * gg_ref[...] + gb_ref[...] + bonus_ref[0].astype(F32)) * g_ref[0].astype(F32)
    o_ref[0] = out.astype(BF16)


def _rwpost(yf, yb, bonus, g, gn_g, gn_b):
    B, _, T = yf.shape
    nt = T // ROW_TILE
    ytile = pl.BlockSpec((1, D_HEADS, ROW_TILE), lambda b, t: (b, 0, t))
    tm = pl.BlockSpec((1, ROW_TILE, D_HEADS), lambda b, t: (b, t, 0))
    row = pl.BlockSpec((1, D_HEADS), lambda b, t: (0, 0))
    return pl.pallas_call(
        _rwpost_kernel,
        grid=(B, nt),
        in_specs=[ytile, ytile, tm, tm, row, row],
        out_specs=tm,
        out_shape=jax.ShapeDtypeStruct((B, T, D_HEADS), BF16),
        compiler_params=_cparams(("arbitrary", "arbitrary")),
    )(yf, yb, bonus, g, gn_g, gn_b)


ATT_ROWS = 2
WKV_CPS = 4


def _attn_kernel(q_ref, k_ref, v_ref, *rest, n_ctx, n_rows):
    bias_refs, o_ref = rest[:ATT_ROWS], rest[ATT_ROWS]
    j = pl.program_id(1)
    n_cstep = n_ctx // (GRID_W * ATT_ROWS)
    lane = lax.broadcasted_iota(jnp.int32, (GRID_W, 128), 1)
    m0 = lane < HEAD_DIM
    win = WIN_ROWS * GRID_W
    rmax = lambda a: jnp.max(a, axis=-1, keepdims=True)
    rsum = lambda a: jnp.sum(a, axis=-1, keepdims=True)
    probs = [(u, p, slice(u * GRID_W, (u + 1) * GRID_W), slice(p * 128, (p + 1) * 128))
             for u in range(ATT_ROWS) for p in range(N_PAIRS)]

    def stacked_q(rs, ls):
        return _bd(q_ref[0, rs, ls], m0)

    @pl.when(j < n_cstep)
    def _():
        sc = [_dot_nt(stacked_q(rs, ls), k_ref[0, 0:n_ctx, ls]) for _, _, rs, ls in probs]
        mx = [rmax(a) for a in sc]
        ex = [jnp.exp(a - m) for a, m in zip(sc, mx)]
        den = [rsum(e) for e in ex]
        for n, (_, _, rs, ls) in enumerate(probs):
            o = _dot(ex[n].astype(BF16), v_ref[0, 0:n_ctx, ls]) / den[n]
            o_ref[0, rs, ls] = _sel(o, m0).astype(BF16)

    @pl.when(j >= n_cstep)
    def _():
        starts = []
        for u in range(ATT_ROWS):
            i = (j - n_cstep) * ATT_ROWS + u
            r0 = jnp.clip(i - WIN_ROWS // 2, 0, n_rows - WIN_ROWS)
            starts.append(pl.multiple_of(n_ctx + r0 * GRID_W, GRID_W))
        qs = [stacked_q(rs, ls) for _, _, rs, ls in probs]
        s_loc = [_dot_nt(qs[n], k_ref[0, pl.ds(starts[u], win), ls]) + bias_refs[u][0, p]
                 for n, (u, p, _, ls) in enumerate(probs)]
        s_ctx = [_dot_nt(qs[n], k_ref[0, 0:n_ctx, ls]) for n, (_, _, _, ls) in enumerate(probs)]
        mx = [jnp.maximum(rmax(a), rmax(b)) for a, b in zip(s_loc, s_ctx)]
        e_loc = [jnp.exp(a - m) for a, m in zip(s_loc, mx)]
        e_ctx = [jnp.exp(a - m) for a, m in zip(s_ctx, mx)]
        den = [rsum(a) + rsum(b) for a, b in zip(e_loc, e_ctx)]
        for n, (u, _, rs, ls) in enumerate(probs):
            o = _dot(e_loc[n].astype(BF16), v_ref[0, pl.ds(starts[u], win), ls])
            o = (o + _dot(e_ctx[n].astype(BF16), v_ref[0, 0:n_ctx, ls])) / den[n]
            o_ref[0, rs, ls] = _sel(o, m0).astype(BF16)


def _attention(qkv, bias_tab, n_ctx):
    B, T, _ = qkv.shape
    n_rows = (T - n_ctx) // GRID_W
    n_cstep = n_ctx // (GRID_W * ATT_ROWS)
    half = WIN_ROWS // 2
    blk = GRID_W * ATT_ROWS

    def delta(j, u):
        i = jnp.maximum(j - n_cstep, 0) * ATT_ROWS + u
        return jnp.minimum(i, half) + jnp.maximum(i - (n_rows - half), 0)

    bias_specs = [pl.BlockSpec((1, N_PAIRS, 128, WIN_ROWS * GRID_W), functools.partial(
        lambda b, j, u: (delta(j, u), 0, 0, 0), u=u)) for u in range(ATT_ROWS)]
    return pl.pallas_call(
        functools.partial(_attn_kernel, n_ctx=n_ctx, n_rows=n_rows),
        grid=(B, T // blk),
        in_specs=[
            pl.BlockSpec((1, blk, D_HEADS), lambda b, j: (b, j, 0)),
            pl.BlockSpec((1, T, D_HEADS), lambda b, j: (b, 0, 1)),
            pl.BlockSpec((1, T, D_HEADS), lambda b, j: (b, 0, 2)),
        ] + bias_specs,
        out_specs=pl.BlockSpec((1, blk, D_HEADS), lambda b, j: (b, j, 0)),
        out_shape=jax.ShapeDtypeStruct((B, T, D_HEADS), BF16),
        compiler_params=_cparams(("arbitrary", "arbitrary")),
    )(qkv, qkv, qkv, *([bias_tab] * ATT_ROWS))


def _na_bias_table(rpb):
    H = rpb.shape[0]
    c = np.arange(GRID_W)[:, None]
    kc = np.arange(GRID_W)[None, :]
    cs = np.clip(c - WIN_COLS // 2, 0, GRID_W - WIN_COLS)
    valid = (kc >= cs) & (kc < cs + WIN_COLS)
    cidx = np.clip(kc - c + (WIN_COLS - 1), 0, 2 * WIN_COLS - 2)
    t = jnp.where(valid[None, None], rpb[:, :, cidx], NEG)
    t = t.transpose(0, 2, 1, 3).astype(F32)
    tabs = [t[:, :, WIN_ROWS - 1 - dl:2 * WIN_ROWS - 1 - dl, :].reshape(H, GRID_W, WIN_ROWS * GRID_W)
            for dl in range(WIN_ROWS)]
    return jnp.stack(tabs, 0).reshape(WIN_ROWS, H // 2, 2 * GRID_W, WIN_ROWS * GRID_W)


def _layer_norm(h, g, b):
    mu = jnp.mean(h, axis=-1, keepdims=True)
    var = jnp.mean(jnp.square(h - mu), axis=-1, keepdims=True)
    return (h - mu) * lax.rsqrt(var + LN_EPS) * g + b


def _to_token_tiles(ref, val):
    n = val.shape[0]
    for j in range(TOK_SUB):
        ref[pl.ds(j, n, stride=TOK_SUB), :] = val[:, j * 128:(j + 1) * 128]


def _from_token_tiles(ref, n):
    return [ref[pl.ds(j, n, stride=TOK_SUB), :] for j in range(TOK_SUB)]


def _outproj_kernel(na_ref, rw_ref, w_ref, zc_ref, zx_ref, mod_ref, g_ref, b_ref, rw_w_ref,
                    z1_ref, hx_ref, lg_ref, *, d, alpha, n_ctx_tiles, t0):
    o = _dot(na_ref[0], w_ref[0:D_HEADS, :]) + _dot(rw_ref[0], w_ref[D_HEADS:2 * D_HEADS, :])
    mod = mod_ref[0, 0]
    gate = mod[:, 2 * d:3 * d]
    z = _z_tile(zc_ref, zx_ref, n_ctx_tiles, t0)
    z1 = _layer_norm(alpha * z + gate * o, g_ref[...], b_ref[...])
    z1_ref[0] = z1
    hx = z1 * (1.0 + mod[:, 4 * d:5 * d]) + mod[:, 3 * d:4 * d]
    _to_token_tiles(hx_ref, hx)
    lg_ref[0] = _dot3(hx, rw_w_ref[...])


def _outproj(na, rw, w_out_b, zc, zx, sub, modsel, ln_g, ln_b, router_wp, n_ctx_tiles, t0, alpha):
    B, T, _ = na.shape
    D = zc.shape[2]
    nt = T // ROW_TILE - t0
    half = pl.BlockSpec((1, ROW_TILE, D_HEADS), lambda b, t: (b, t + t0, 0))
    tile = pl.BlockSpec((1, ROW_TILE, D), lambda b, t: (b, t, 0))
    row = pl.BlockSpec((1, D), lambda b, t: (0, 0))
    cspec, xspec = _z_specs(D, n_ctx_tiles, sub, t0)
    return pl.pallas_call(
        functools.partial(_outproj_kernel, d=D, alpha=alpha, n_ctx_tiles=n_ctx_tiles, t0=t0),
        grid=(B, nt),
        in_specs=[
            half, half,
            pl.BlockSpec((2 * D_HEADS, D), lambda b, t: (0, 0)),
            cspec, xspec,
            _mod_spec(D, n_ctx_tiles, t0),
            row, row,
            pl.BlockSpec((D, 128), lambda b, t: (0, 0)),
        ],
        out_specs=[
            tile,
            pl.BlockSpec((ROW_TILE * TOK_SUB, 128), lambda b, t: (b * nt + t, 0)),
            pl.BlockSpec((1, ROW_TILE, 128), lambda b, t: (b, t, 0)),
        ],
        out_shape=[
            jax.ShapeDtypeStruct((B, nt * ROW_TILE, D), F32),
            jax.ShapeDtypeStruct((B * nt * ROW_TILE * TOK_SUB, 128), F32),
            jax.ShapeDtypeStruct((B, nt * ROW_TILE, 128), F32),
        ],
        compiler_params=_cparams(("arbitrary", "arbitrary")),
    )(na, rw, w_out_b, zc, zx, modsel, ln_g, ln_b, router_wp)


def _tok(ref, i):
    return ref.at[pl.ds(pl.multiple_of(i * TOK_SUB, TOK_SUB), TOK_SUB)]


def _dispatch_kernel(dest_ref, hx_ref, xs_in, xs_out, sem):
    del xs_in
    n = dest_ref.shape[0]

    def start(i, c):
        for par in range(2):
            r = 2 * i + par
            pltpu.make_async_copy(_tok(hx_ref, r), _tok(xs_out, dest_ref[r]), sem).start(priority=par)
        return c

    lax.fori_loop(0, n // 2, start, 0, unroll=4)

    def wait(r, c):
        pltpu.make_async_copy(_tok(hx_ref, 0), _tok(xs_out, 0), sem).wait()
        return c

    lax.fori_loop(0, n, wait, 0, unroll=8)


def _dispatch(dest, hx_tiles, n_rows_pad):
    N = dest.shape[0]
    tile = next(t for t in (4 * ROW_TILE, 2 * ROW_TILE, ROW_TILE) if N % t == 0)
    xs0 = jnp.zeros((n_rows_pad * TOK_SUB, 128), F32)
    return pl.pallas_call(
        _dispatch_kernel,
        grid=(N // tile,),
        in_specs=[
            pl.BlockSpec((tile,), lambda i: (i,), memory_space=pltpu.SMEM),
            pl.BlockSpec((tile * TOK_SUB, 128), lambda i: (i, 0)),
            pl.BlockSpec(memory_space=pl.ANY),
        ],
        out_specs=pl.BlockSpec(memory_space=pl.ANY),
        out_shape=jax.ShapeDtypeStruct((n_rows_pad * TOK_SUB, 128), F32),
        scratch_shapes=[pltpu.SemaphoreType.DMA(())],
        input_output_aliases={2: 0},
        compiler_params=_cparams(("arbitrary",)),
    )(dest, hx_tiles, xs0)


def _pack_bf16_pair(a, b):
    ha = lax.bitcast_convert_type(a.astype(BF16).astype(F32), jnp.uint32)
    hb = lax.bitcast_convert_type(b.astype(BF16).astype(F32), jnp.uint32)
    return jnp.bitwise_or(ha, jnp.right_shift(hb, jnp.uint32(16)))


def _unpack_bf16_pair(w):
    a = lax.bitcast_convert_type(jnp.bitwise_and(w, jnp.uint32(0xFFFF0000)), F32)
    b = lax.bitcast_convert_type(jnp.left_shift(w, jnp.uint32(16)), F32)
    return a, b


def _expert_kernel(sc_ref, xs_ref, w1_hbm, w3_hbm, w2_hbm, ys_ref, s13, s2, c13, c2, sem, *, layer):
    i = pl.program_id(0)
    slot = sc_ref[2, i]

    def copies(sl, ea, eb):
        out = []
        for j, e in enumerate((ea, eb)):
            out.append(pltpu.make_async_copy(w1_hbm.at[layer, e], s13.at[sl, 2 * j], sem.at[sl, 3 * j]))
            out.append(pltpu.make_async_copy(w3_hbm.at[layer, e], s13.at[sl, 2 * j + 1], sem.at[sl, 3 * j + 1]))
            out.append(pltpu.make_async_copy(w2_hbm.at[layer, e], s2.at[sl, j], sem.at[sl, 3 * j + 2]))
        return out

    @pl.when(i == 0)
    def _():
        for cp in copies(0, sc_ref[3, i], sc_ref[4, i]):
            cp.start()

    for sl in range(2):
        @pl.when(jnp.logical_and(sc_ref[1, i] != 0, slot == sl))
        def _():
            for cp in copies(sl, sc_ref[3, i], sc_ref[4, i]):
                cp.wait()
            def cast(r, c):
                rows = pl.ds(pl.multiple_of(r * 64, 64), 64)
                for j in range(4):
                    c13[j, rows, :] = s13[sl, j, rows, :].astype(BF16)
                return c

            lax.fori_loop(0, s13.shape[2] // 64, cast, 0)

            def cast2(r, c):
                rows = pl.ds(pl.multiple_of(r * 64, 64), 64)
                for j in range(2):
                    c2[j, rows, :] = s2[sl, j, rows, :].astype(BF16)
                return c

            lax.fori_loop(0, s2.shape[2] // 64, cast2, 0)

            @pl.when(sc_ref[5, i] != 0)
            def _():
                for cp in copies(1 - sl, sc_ref[6, i], sc_ref[7, i]):
                    cp.start()

    @pl.when(sc_ref[0, i] != 0)
    def _():
        x = jnp.concatenate(_from_token_tiles(xs_ref, MOE_TILE), axis=1).astype(BF16)

        def ffn(j):
            h1 = _dot(x, c13[2 * j])
            h3 = _dot(x, c13[2 * j + 1])
            h = (h1 * jax.nn.sigmoid(h1)) * h3
            return _dot(h.astype(BF16), c2[j])

        _to_token_tiles(ys_ref, _pack_bf16_pair(ffn(0), ffn(1)))

    @pl.when(sc_ref[0, i] == 0)
    def _():
        ys_ref[...] = jnp.zeros_like(ys_ref)


def _experts(blk_sched, xs, w1, w3, w2, l):
    nb = xs.shape[0] // (MOE_TILE * TOK_SUB)
    _, _, D, DE = w1.shape
    tok = pl.BlockSpec((MOE_TILE * TOK_SUB, 128), lambda i, sc: (i, 0))
    hbm = pl.BlockSpec(memory_space=pl.ANY)
    grid_spec = pltpu.PrefetchScalarGridSpec(
        num_scalar_prefetch=1,
        grid=(nb,),
        in_specs=[tok, hbm, hbm, hbm],
        out_specs=tok,
        scratch_shapes=[
            pltpu.VMEM((2, 4, D, DE), F32),
            pltpu.VMEM((2, 2, DE, D), F32),
            pltpu.VMEM((4, D, DE), BF16),
            pltpu.VMEM((2, DE, D), BF16),
            pltpu.SemaphoreType.DMA((2, 6)),
        ],
    )
    return pl.pallas_call(
        functools.partial(_expert_kernel, layer=l),
        grid_spec=grid_spec,
        out_shape=jax.ShapeDtypeStruct(xs.shape, jnp.uint32),
        compiler_params=_cparams(("arbitrary",)),
    )(blk_sched, xs, w1, w3, w2)


def _combine_kernel(dest_ref, dnext_ref, ys_ref, gate_ref, z1_ref, mod_ref, g_ref, b_ref, o_ref, buf, sem,
                    *, d, alpha):
    n = dest_ref.shape[0]
    step = pl.program_id(0) * pl.num_programs(1) + pl.program_id(1)
    n_steps = pl.num_programs(0) * pl.num_programs(1)
    slot = step % 2

    def gather(idx_ref, sl):
        def start(i, c):
            for par in range(2):
                r = 2 * i + par
                pltpu.make_async_copy(_tok(ys_ref, idx_ref[r]), _tok(buf.at[sl], r), sem.at[sl]).start(priority=par)
            return c

        lax.fori_loop(0, n // 2, start, 0, unroll=4)

    @pl.when(step == 0)
    def _():
        gather(dest_ref, 0)

    @pl.when(step + 1 < n_steps)
    def _():
        gather(dnext_ref, 1 - slot)

    def wait(r, c):
        pltpu.make_async_copy(_tok(ys_ref, 0), _tok(buf.at[slot], 0), sem.at[slot]).wait()
        return c

    lax.fori_loop(0, n, wait, 0, unroll=8)
    ga = gate_ref[:, 0:1]
    gb = gate_ref[:, 1:2]
    parts = []
    for w in _from_token_tiles(buf.at[slot], n):
        fa, fb = _unpack_bf16_pair(w)
        parts.append(ga * fa + gb * fb)
    y = jnp.concatenate(parts, axis=1)
    gate = mod_ref[0, 0][:, 5 * d:6 * d]
    o_ref[0] = _layer_norm(alpha * z1_ref[0] + gate * y, g_ref[...], b_ref[...])


def _combine(dest, ys, gates, z1, modsel, ln_g, ln_b, n_ctx_tiles, t0, t_out, alpha):
    B, T1, D = z1.shape
    nt1 = T1 // ROW_TILE
    skip = t_out - t0
    nt = nt1 - skip
    tile = lambda off: pl.BlockSpec((1, ROW_TILE, D), lambda b, t: (b, t + off, 0))
    row = pl.BlockSpec((1, D), lambda b, t: (0, 0))

    def nxt(b, t):
        last = jnp.logical_and(b == B - 1, t == nt - 1)
        wrap = t == nt - 1
        b2 = jnp.where(jnp.logical_and(wrap, jnp.logical_not(last)), b + 1, b)
        t2 = jnp.where(last, t, jnp.where(wrap, 0, t + 1))
        return b2 * nt1 + t2 + skip

    return pl.pallas_call(
        functools.partial(_combine_kernel, d=D, alpha=alpha),
        grid=(B, nt),
        in_specs=[
            pl.BlockSpec((ROW_TILE,), lambda b, t: (b * nt1 + t + skip,), memory_space=pltpu.SMEM),
            pl.BlockSpec((ROW_TILE,), lambda b, t: (nxt(b, t),), memory_space=pltpu.SMEM),
            pl.BlockSpec(memory_space=pl.ANY),
            pl.BlockSpec((ROW_TILE, 128), lambda b, t: (b * nt1 + t + skip, 0)),
            tile(skip),
            _mod_spec(D, n_ctx_tiles, t_out),
            row, row,
        ],
        out_specs=tile(0),
        out_shape=jax.ShapeDtypeStruct((B, nt * ROW_TILE, D), F32),
        scratch_shapes=[pltpu.VMEM((2, ROW_TILE * TOK_SUB, 128), jnp.uint32), pltpu.SemaphoreType.DMA((2,))],
        compiler_params=_cparams(("arbitrary", "arbitrary")),
    )(dest, dest, ys, gates, z1, modsel, ln_g, ln_b)


_PAIR_LO = np.array([0, 0, 0, 1, 1, 2], np.int32)
_PAIR_HI = np.array([1, 2, 3, 2, 3, 3], np.int32)


def _route(logits, router_bias):
    N = logits.shape[0]
    scores = jax.nn.sigmoid(logits[:, :N_EXPERTS])
    sel = (scores + router_bias.astype(F32)).reshape(N, N_GROUPS, EXPERTS_PER_GROUP)
    pos = jnp.arange(EXPERTS_PER_GROUP, dtype=jnp.int32)
    a1 = jnp.argmax(sel, axis=-1)
    m1 = jnp.max(sel, axis=-1)
    rest = jnp.where(pos == a1[..., None], -jnp.inf, sel)
    m2 = jnp.max(rest, axis=-1)
    g_idx = jnp.argmax(m1 + m2, axis=-1).astype(jnp.int32)
    in_g = (jnp.arange(N_GROUPS, dtype=jnp.int32)[None, :] == g_idx[:, None])[..., None]
    sel_g = jnp.sum(jnp.where(in_g, sel, 0.0), axis=1)
    i1 = jnp.argmax(sel_g, axis=-1).astype(jnp.int32)
    i2 = jnp.argmax(jnp.where(pos == i1[:, None], -jnp.inf, sel_g), axis=-1).astype(jnp.int32)
    sc_g = jnp.sum(jnp.where(in_g, scores.reshape(N, N_GROUPS, EXPERTS_PER_GROUP), 0.0), axis=1)
    lo = jnp.minimum(i1, i2)
    hi = jnp.maximum(i1, i2)
    g_lo = jnp.sum(jnp.where(pos == lo[:, None], sc_g, 0.0), axis=1)
    g_hi = jnp.sum(jnp.where(pos == hi[:, None], sc_g, 0.0), axis=1)
    tot = g_lo + g_hi
    g_lo = g_lo / tot
    g_hi = g_hi / tot
    pair = lo * 3 - (lo * (lo - 1)) // 2 + (hi - lo - 1)
    cls = g_idx * 6 + pair

    onehot = (cls[:, None] == jnp.arange(N_CLASSES, dtype=jnp.int32)[None, :])
    oh = onehot.astype(BF16).reshape(N // ROW_TILE, ROW_TILE, N_CLASSES)
    tri = jnp.tril(jnp.ones((ROW_TILE, ROW_TILE), BF16), -1)
    within = jnp.einsum("rs,tsc->trc", tri, oh, preferred_element_type=F32)
    tile_cnt = jnp.sum(oh.astype(F32), axis=1)
    tile_off = jnp.cumsum(tile_cnt, axis=0) - tile_cnt
    rank_all = (within + tile_off[:, None, :]).reshape(N, N_CLASSES)
    rank = jnp.sum(jnp.where(onehot, rank_all, 0.0), axis=1).astype(jnp.int32)
    counts = jnp.sum(tile_cnt, axis=0).astype(jnp.int32)
    padded = (counts + MOE_TILE - 1) // MOE_TILE * MOE_TILE
    cls_end = jnp.cumsum(padded)
    cls_start = cls_end - padded
    dest = jnp.sum(jnp.where(onehot, cls_start[None, :], 0), axis=1) + rank

    nb = N // MOE_TILE + N_CLASSES
    blk_row = jnp.arange(nb, dtype=jnp.int32) * MOE_TILE
    total = jnp.sum(padded)
    blk_valid = (blk_row < total).astype(jnp.int32)
    row_c = jnp.minimum(blk_row, jnp.maximum(total - MOE_TILE, 0))
    blk_cls = jnp.sum((row_c[:, None] >= cls_end[None, :]).astype(jnp.int32), axis=1)
    blk_cls = jnp.minimum(blk_cls, N_CLASSES - 1)

    def experts_of(c):
        hot = (c % 6)[:, None] == jnp.arange(6, dtype=jnp.int32)[None, :]
        base = (c // 6) * EXPERTS_PER_GROUP
        return (base + jnp.sum(jnp.where(hot, jnp.asarray(_PAIR_LO)[None, :], 0), axis=1),
                base + jnp.sum(jnp.where(hot, jnp.asarray(_PAIR_HI)[None, :], 0), axis=1))

    prev_cls = jnp.concatenate([jnp.full((1,), -1, jnp.int32), blk_cls[:-1]])
    first = jnp.logical_and(blk_valid != 0, blk_cls != prev_cls).astype(jnp.int32)
    parity = (jnp.cumsum(first) - 1) % 2
    cls_ids = jnp.arange(N_CLASSES, dtype=jnp.int32)
    used = jnp.where(counts > 0, cls_ids, N_CLASSES)
    later = jnp.where(cls_ids[None, :] > cls_ids[:, None], used[None, :], N_CLASSES)
    nxt_of_cls = jnp.min(later, axis=1)
    cls_hot = blk_cls[:, None] == cls_ids[None, :]
    nxt_cls = jnp.sum(jnp.where(cls_hot, nxt_of_cls[None, :], 0), axis=1)
    has_next = (nxt_cls < N_CLASSES).astype(jnp.int32)
    ea, eb = experts_of(blk_cls)
    na, nb_ = experts_of(jnp.minimum(nxt_cls, N_CLASSES - 1))
    blk_sched = jnp.stack([blk_valid, first, parity, ea, eb, has_next, na, nb_]).astype(jnp.int32)
    gates = jnp.pad(jnp.stack([g_lo, g_hi], axis=1), ((0, 0), (0, 126)))
    return dest.astype(jnp.int32), gates, blk_sched, nb * MOE_TILE


def _rope_tables(n_ctx, seq):
    t = np.arange(seq)
    row = (t // GRID_W).astype(np.float32)
    col = (t % GRID_W).astype(np.float32)
    n_freq = HEAD_DIM // 4
    inv = jnp.asarray(ROPE_BASE, F32) ** (-jnp.arange(n_freq, dtype=F32) / n_freq)
    ar = jnp.asarray(row)[:, None] * inv
    ac = jnp.asarray(col)[:, None] * inv
    ang = jnp.concatenate([ar, ar, ac, ac], -1)
    cos = jnp.cos(ang)
    sin = jnp.sin(ang)
    quarter = (np.arange(HEAD_DIM) // n_freq) % 2
    sa = jnp.where(quarter == 0, -sin, 0.0)
    sb = jnp.where(quarter == 1, sin, 0.0)
    pad = lambda a, fill: jnp.concatenate([jnp.full((n_ctx, HEAD_DIM), fill, F32), a], 0)
    tile = lambda a: jnp.tile(a, (1, N_HEADS))
    return tile(pad(cos, 1.0)), tile(pad(sa, 0.0)), tile(pad(sb, 0.0))


def _rw_consts(l, n_ctx, seq, rw_mu_prev, rw_mu_next, rw_w0, rw_w2, rw_a0, rw_a2, rw_g2, rw_k_k, rw_k_a, rw_r_k):
    d_rw_in = rw_mu_prev.shape[1]
    padw = lambda a: jnp.pad(a[l], (0, D_RWP - d_rw_in)).reshape(1, D_RWP)
    cos, sa, sb = _rope_tables(n_ctx, seq)
    w2p = jnp.zeros((128, 2 * D_HEADS), F32)
    w2p = w2p.at[0:LORA, 0:D_HEADS].set(rw_w2[l, 0]).at[LORA:2 * LORA, D_HEADS:].set(rw_w2[l, 1])
    a2p = jnp.zeros((128, 2 * D_HEADS), F32)
    a2p = a2p.at[2 * LORA:3 * LORA, 0:D_HEADS].set(rw_a2[l, 0]).at[3 * LORA:4 * LORA, D_HEADS:].set(rw_a2[l, 1])
    g2p = jnp.zeros((128, D_HEADS), F32).at[0:GATE_LORA].set(rw_g2[l])
    head = np.arange(D_HEADS) // HEAD_DIM
    ones_bd = jnp.asarray(head[:, None] == head[None, :], BF16)
    return dict(
        mu_prev=padw(rw_mu_prev), mu_next=padw(rw_mu_next), cos=cos, sa=sa, sb=sb,
        k_k=rw_k_k[l].reshape(1, D_HEADS), k_a=rw_k_a[l].reshape(1, D_HEADS), r_k=rw_r_k[l].reshape(1, D_HEADS),
        w0=rw_w0[l].reshape(1, 2 * D_HEADS), a0=rw_a0[l].reshape(1, 2 * D_HEADS),
        w2p=w2p.astype(BF16), a2p=a2p.astype(BF16), g2p=g2p.astype(BF16), ones_bd=ones_bd,
    )


def kernel(x, c, ctx, c_ctx, ada_w, ada_b, w_in, na_rpb, rw_mu_prev, rw_mu_next, rw_w0, rw_w2, rw_a0, rw_a2, rw_g2, rw_k_k, rw_k_a, rw_r_k, rw_gn_g, rw_gn_b, w_out, ln1_g, ln1_b, ln2_g, ln2_b, router_w, router_bias, exp_w1, exp_w3, exp_w2):
    B, S, D = x.shape
    C = ctx.shape[1]
    L = ada_w.shape[0]
    T = C + S
    assert D == 1024 and C % ROW_TILE == 0 and S % ROW_TILE == 0 and C % (2 * CHUNK) == 0
    assert S % GRID_W == 0 and S // GRID_W >= WIN_ROWS and w_in.shape[2] == D_INP - 32
    n_ctx_tiles = C // ROW_TILE
    alpha = float((2 * L) ** 0.25)

    zc, zx, sub = ctx, x, n_ctx_tiles
    n_mod = (B + 1 + 7) // 8 * 8
    cc = jnp.zeros((n_mod, D), F32).at[0:B].set(c).at[B].set(c_ctx)
    mod_all = _ada(cc, ada_w, ada_b)
    router_wp = jnp.pad(router_w, ((0, 0), (0, 128 - N_EXPERTS)))

    for l in range(L):
        mod_c = jnp.broadcast_to(mod_all[l, B][None], (B, 6 * D))
        modsel = jnp.stack([mod_c, mod_all[l, 0:B]], axis=1).reshape(B, 2, 1, 6 * D)
        w_in_p = jnp.pad(w_in[l], ((0, 0), (0, D_INP - w_in.shape[2]))).astype(BF16)
        qkv, p_rw = _inproj(zc, zx, sub, T, modsel, w_in_p, n_ctx_tiles)

        consts = _rw_consts(l, C, S, rw_mu_prev, rw_mu_next, rw_w0, rw_w2, rw_a0, rw_a2, rw_g2,
                            rw_k_k, rw_k_a, rw_r_k)
        prep = _rwprep(p_rw, consts, n_ctx_tiles)
        yf, yb = _wkv(prep[0:15], C)
        rw = _rwpost(yf, yb, prep[15], prep[16], rw_gn_g[l].reshape(1, D_HEADS), rw_gn_b[l].reshape(1, D_HEADS))

        na = _attention(qkv, _na_bias_table(na_rpb[l]), C)

        t0 = n_ctx_tiles if l == L - 1 else 0
        z1, hx_tiles, logits = _outproj(na, rw, w_out[l].astype(BF16), zc, zx, sub, modsel,
                                        ln1_g[l].reshape(1, D), ln1_b[l].reshape(1, D), router_wp,
                                        n_ctx_tiles, t0, alpha)

        dest, gates, blk_sched, n_rows_pad = _route(logits.reshape(-1, 128), router_bias)
        xs = _dispatch(dest, hx_tiles, n_rows_pad)
        ys = _experts(blk_sched, xs, exp_w1, exp_w3, exp_w2, l)
        z = _combine(dest, ys, gates, z1, modsel, ln2_g[l].reshape(1, D), ln2_b[l].reshape(1, D),
                     n_ctx_tiles, t0, t0, alpha)
        zc, zx, sub = z, z, 0

    return z
```

```python
import functools
import math

import jax
import jax.numpy as jnp
import numpy as np
from jax import lax
from jax.experimental import pallas as pl
from jax.experimental.pallas import tpu as pltpu

F32 = jnp.float32
BF16 = jnp.bfloat16

HEAD_DIM = 64
N_HEADS = 8
D_HEADS = N_HEADS * HEAD_DIM
N_PAIRS = N_HEADS // 2
GRID_W = 64
WIN_ROWS = 8
WIN_COLS = 16
LORA = 32
GATE_LORA = 96
N_EXPERTS = 32
N_GROUPS = 8
EXPERTS_PER_GROUP = 4
N_CLASSES = N_GROUPS * 6
ROPE_BASE = 10000.0
LN_EPS = 1e-6
GN_EPS = 64e-5
CHUNK = 64
ROW_TILE = 256
MOE_TILE = 256
D_QKV = 3 * D_HEADS
D_RWP = 3 * D_HEADS + 256
D_INP = D_QKV + D_RWP
TOK_SUB = 8
NEG = -1e30
VMEM_LIMIT = 56 * 1024 * 1024


def _cparams(sem):
    return pltpu.CompilerParams(dimension_semantics=sem, vmem_limit_bytes=VMEM_LIMIT)


def _dot(a, b):
    return jnp.dot(a, b, preferred_element_type=F32)


def _dot_nt(a, b):
    return lax.dot_general(a, b, (((1,), (1,)), ((), ())), preferred_element_type=F32)


def _split2(a):
    hi = a.astype(BF16)
    lo = (a - hi.astype(F32)).astype(BF16)
    return hi, lo


def _split3(a):
    hi = a.astype(BF16)
    r1 = a - hi.astype(F32)
    mid = r1.astype(BF16)
    lo = (r1 - mid.astype(F32)).astype(BF16)
    return hi, mid, lo


def _dot3(a, b):
    ah, al = _split2(a)
    bh, bl = _split2(b)
    return _dot(ah, bh) + _dot(al, bh) + _dot(ah, bl)


def _dot_exact_rhs(a, b_exact):
    h, m, l = _split3(a)
    return _dot(h, b_exact) + _dot(m, b_exact) + _dot(l, b_exact)


def _dot_exact_lhs(a_exact, b):
    h, m, l = _split3(b)
    return _dot(a_exact, h) + _dot(a_exact, m) + _dot(a_exact, l)


def _ada_kernel(cc_ref, w_ref, b_ref, o_ref):
    cc = cc_ref[...]
    s = cc * jax.nn.sigmoid(cc)
    o_ref[0] = _dot3(s, w_ref[0]) + b_ref[0]


def _ada(cc, ada_w, ada_b):
    L, D, D6 = ada_w.shape
    R = cc.shape[0]
    tn = 1536
    return pl.pallas_call(
        _ada_kernel,
        grid=(L, D6 // tn),
        in_specs=[
            pl.BlockSpec((R, D), lambda l, n: (0, 0)),
            pl.BlockSpec((1, D, tn), lambda l, n: (l, 0, n)),
            pl.BlockSpec((1, 1, tn), lambda l, n: (l, 0, n)),
        ],
        out_specs=pl.BlockSpec((1, R, tn), lambda l, n: (l, 0, n)),
        out_shape=jax.ShapeDtypeStruct((L, R, D6), F32),
        compiler_params=_cparams(("arbitrary", "arbitrary")),
    )(cc, ada_w, ada_b.reshape(L, 1, D6))


def _z_specs(D, n_ctx_tiles, sub, t0=0):
    cspec = pl.BlockSpec((1, ROW_TILE, D), lambda b, t: (b, jnp.minimum(t + t0, n_ctx_tiles - 1), 0))
    xspec = pl.BlockSpec((1, ROW_TILE, D), lambda b, t: (b, jnp.maximum(t + t0, n_ctx_tiles) - sub, 0))
    return cspec, xspec


def _z_tile(zc_ref, zx_ref, n_ctx_tiles, t0=0):
    return jnp.where(pl.program_id(1) + t0 < n_ctx_tiles, zc_ref[0], zx_ref[0])


def _mod_spec(D, n_ctx_tiles, t0=0):
    return pl.BlockSpec((1, 1, 1, 6 * D), lambda b, t: (b, jnp.where(t + t0 < n_ctx_tiles, 0, 1), 0, 0))


def _inproj_kernel(zc_ref, zx_ref, mod_ref, w_ref, qkv_ref, rw_ref, *, d, n_ctx_tiles):
    z = _z_tile(zc_ref, zx_ref, n_ctx_tiles)
    mod = mod_ref[0, 0]
    shift = mod[:, 0:d]
    scale = mod[:, d:2 * d]
    h = (z * (1.0 + scale) + shift).astype(BF16)
    q = _dot(h, w_ref[:, 0:D_HEADS])
    qkv_ref[0, :, 0:D_HEADS] = (q * (HEAD_DIM ** -0.5)).astype(BF16)
    kv = _dot(h, w_ref[:, D_HEADS:D_QKV])
    qkv_ref[0, :, D_HEADS:D_QKV] = kv.astype(BF16)
    rw_ref[0] = _dot(h, w_ref[:, D_QKV:D_INP])


def _inproj(zc, zx, sub, T, modsel, w_in_p, n_ctx_tiles):
    B, _, D = zc.shape
    nt = T // ROW_TILE
    cspec, xspec = _z_specs(D, n_ctx_tiles, sub)
    return pl.pallas_call(
        functools.partial(_inproj_kernel, d=D, n_ctx_tiles=n_ctx_tiles),
        grid=(B, nt),
        in_specs=[
            cspec, xspec,
            pl.BlockSpec((1, 1, 1, 6 * D), lambda b, t: (b, jnp.where(t < n_ctx_tiles, 0, 1), 0, 0)),
            pl.BlockSpec((D, D_INP), lambda b, t: (0, 0)),
        ],
        out_specs=[
            pl.BlockSpec((1, ROW_TILE, D_QKV), lambda b, t: (b, t, 0)),
            pl.BlockSpec((1, ROW_TILE, D_RWP), lambda b, t: (b, t, 0)),
        ],
        out_shape=[
            jax.ShapeDtypeStruct((B, T, D_QKV), BF16),
            jax.ShapeDtypeStruct((B, T, D_RWP), F32),
        ],
        compiler_params=_cparams(("arbitrary", "arbitrary")),
    )(zc, zx, modsel, w_in_p)


def _rwprep_kernel(p_ref, pp_ref, pn_ref, mup_ref, mun_ref, cos_ref, sa_ref, sb_ref,
                   kk_ref, ka_ref, rk_ref, w0_ref, a0_ref, w2_ref, a2_ref, g2_ref, ones_ref,
                   at_f, bt_f, kt_f, rt_f, bh_f, kh_f, pe_f,
                   at_b, bt_b, kt_b, rt_b, bh_b, kh_b, pe_b,
                   vt_ref, bonus_ref, g_ref, *, n_ctx_tiles, n_tiles):
    t = pl.program_id(1)
    P = p_ref[0]
    R = P.shape[0]
    prev_ok = jnp.logical_and(t != 0, t != n_ctx_tiles)
    next_ok = jnp.logical_and(t != n_ctx_tiles - 1, t != n_tiles - 1)
    prev_row = jnp.where(prev_ok, pp_ref[0, 7:8, :], 0.0)
    next_row = jnp.where(next_ok, pn_ref[0, 0:1, :], 0.0)
    row = lax.broadcasted_iota(jnp.int32, (R, 1), 0)
    prev = jnp.where(row == 0, prev_row, pltpu.roll(P, 1, axis=0))
    nxt = jnp.where(row == R - 1, next_row, pltpu.roll(P, R - 1, axis=0))
    mup = mup_ref[...]
    mun = mun_ref[...]
    z = (1.0 - mup - mun) * P + mup * prev + mun * nxt

    cos = cos_ref[...]
    sa = sa_ref[...]
    sb = sb_ref[...]

    def rope(u):
        q = HEAD_DIM // 4
        parts = []
        for m in range(D_HEADS // 128):
            ls = slice(m * 128, (m + 1) * 128)
            um = u[:, ls]
            parts.append(um * cos[:, ls] + pltpu.roll(um, 128 - q, axis=1) * sa[:, ls]
                         + pltpu.roll(um, q, axis=1) * sb[:, ls])
        return jnp.concatenate(parts, axis=1)

    r = rope(z[:, 0:D_HEADS])
    k = rope(z[:, D_HEADS:2 * D_HEADS])
    v = z[:, 2 * D_HEADS:3 * D_HEADS]
    ones_bd = ones_ref[...]

    kk = k * kk_ref[...]
    sq_hi, sq_lo = _split2(kk * kk)
    kk = kk * lax.rsqrt(jnp.maximum(_dot(sq_hi, ones_bd) + _dot(sq_lo, ones_bd), 1e-24))

    sigmoid = lambda u: 0.5 * jnp.tanh(0.5 * u) + 0.5
    slab = z[:, 3 * D_HEADS:3 * D_HEADS + 128]
    u_w = w0_ref[...] + _dot(jnp.tanh(slab).astype(BF16), w2_ref[...])
    u_a = a0_ref[...] + _dot(slab.astype(BF16), a2_ref[...])
    g_ref[0] = _dot(sigmoid(z[:, 3 * D_HEADS + 128:D_RWP]).astype(BF16), g2_ref[...]).astype(BF16)
    e_all = math.exp(-0.5) * sigmoid(u_w)
    a_all = sigmoid(u_a)

    ci = lax.broadcasted_iota(jnp.int32, (CHUNK, CHUNK), 0)
    cj = lax.broadcasted_iota(jnp.int32, (CHUNK, CHUNK), 1)
    ka = ka_ref[...]
    outs = ((at_f, bt_f, kt_f, rt_f, bh_f, kh_f, pe_f), (at_b, bt_b, kt_b, rt_b, bh_b, kh_b, pe_b))
    kd_sum = None
    for d in range(2):
        e = e_all[:, d * D_HEADS:(d + 1) * D_HEADS]
        a = a_all[:, d * D_HEADS:(d + 1) * D_HEADS]
        tri = (cj <= ci) if d == 0 else (cj >= ci)
        tri = tri.astype(BF16)
        cs, ce = [], []
        for q in range(R // CHUNK):
            e_hi, e_lo = _split2(e[q * CHUNK:(q + 1) * CHUNK])
            cq = -(_dot(tri, e_hi) + _dot(tri, e_lo))
            end = cq[CHUNK - 1:CHUNK] if d == 0 else cq[0:1]
            cs.append(cq)
            ce.append(jnp.broadcast_to(end, cq.shape))
        c = jnp.concatenate(cs, axis=0)
        cend = jnp.concatenate(ce, axis=0)
        kd = k * (1.0 + (a - 1.0) * ka)
        kd_sum = kd if kd_sum is None else kd_sum + kd
        beta = a * kk
        en = jnp.exp(-c)
        eh = jnp.exp(cend - c)
        o_at, o_bt, o_kt, o_rt, o_bh, o_kh, o_pe = outs[d]
        o_at[0] = (-kk * jnp.exp(c + e)).astype(BF16)
        o_bt[0] = (beta * en).astype(BF16)
        o_kt[0] = (kd * en).astype(BF16)
        o_rt[0] = (r * jnp.exp(c)).astype(BF16)
        o_bh[0] = (beta * eh).astype(BF16)
        o_kh[0] = (kd * eh).astype(BF16)
        for q in range(R // CHUNK):
            o_pe[0, q] = jnp.exp(cend[q * CHUNK:q * CHUNK + 8])
    bonus_ref[0] = (_dot((r * rk_ref[...] * kd_sum).astype(BF16), ones_bd) * v).astype(BF16)
    vt_ref[0] = v.T.astype(BF16)


def _rwprep(p_rw, consts, n_ctx_tiles):
    B, T, _ = p_rw.shape
    nt = T // ROW_TILE
    nh = ROW_TILE // 8
    row = lambda w: pl.BlockSpec((1, w), lambda b, t: (0, 0))
    full = lambda a: pl.BlockSpec(a.shape, lambda b, t: (0, 0))
    tm = pl.BlockSpec((1, ROW_TILE, D_HEADS), lambda b, t: (b, t, 0))
    tab = pl.BlockSpec((ROW_TILE, D_HEADS), lambda b, t: (t, 0))
    pe = pl.BlockSpec((1, ROW_TILE // CHUNK, 8, D_HEADS), lambda b, t: (b, t, 0, 0))
    tm_shape = jax.ShapeDtypeStruct((B, T, D_HEADS), BF16)
    pe_shape = jax.ShapeDtypeStruct((B, T // CHUNK, 8, D_HEADS), F32)
    dir_specs = [tm] * 6 + [pe]
    dir_shapes = [tm_shape] * 6 + [pe_shape]
    return pl.pallas_call(
        functools.partial(_rwprep_kernel, n_ctx_tiles=n_ctx_tiles, n_tiles=nt),
        grid=(B, nt),
        in_specs=[
            pl.BlockSpec((1, ROW_TILE, D_RWP), lambda b, t: (b, t, 0)),
            pl.BlockSpec((1, 8, D_RWP), lambda b, t: (b, jnp.maximum(t * nh - 1, 0), 0)),
            pl.BlockSpec((1, 8, D_RWP), lambda b, t: (b, jnp.minimum((t + 1) * nh, T // 8 - 1), 0)),
            row(D_RWP), row(D_RWP), tab, tab, tab,
            row(D_HEADS), row(D_HEADS), row(D_HEADS), row(2 * D_HEADS), row(2 * D_HEADS),
            full(consts["w2p"]), full(consts["a2p"]), full(consts["g2p"]), full(consts["ones_bd"]),
        ],
        out_specs=dir_specs + dir_specs + [
            pl.BlockSpec((1, D_HEADS, ROW_TILE), lambda b, t: (b, 0, t)),
            tm, tm,
        ],
        out_shape=dir_shapes + dir_shapes + [
            jax.ShapeDtypeStruct((B, D_HEADS, T), BF16),
            jax.ShapeDtypeStruct((B, T, D_HEADS), BF16),
            jax.ShapeDtypeStruct((B, T, D_HEADS), BF16),
        ],
        compiler_params=_cparams(("arbitrary", "arbitrary")),
    )(p_rw, p_rw, p_rw, consts["mu_prev"], consts["mu_next"], consts["cos"], consts["sa"], consts["sb"],
      consts["k_k"], consts["k_a"], consts["r_k"], consts["w0"], consts["a0"],
      consts["w2p"], consts["a2p"], consts["g2p"], consts["ones_bd"])


def _bd(y, m0):
    zero = jnp.zeros_like(y)
    return jnp.concatenate([jnp.where(m0, y, zero), jnp.where(m0, zero, y)], axis=0)


def _sel(w, m0):
    return jnp.where(m0, w[0:CHUNK], w[CHUNK:2 * CHUNK])


def _wkv_kernel(*refs):
    (at_f, bt_f, kt_f, rt_f, bh_f, kh_f, pe_f, vt_f,
     at_b, bt_b, kt_b, rt_b, bh_b, kh_b, pe_b, vt_b,
     yf_ref, yb_ref, s_ref) = refs
    s = pl.program_id(1)

    @pl.when(s == 0)
    def _():
        s_ref[...] = jnp.zeros_like(s_ref)

    lane = lax.broadcasted_iota(jnp.int32, (CHUNK, 2 * CHUNK), 1)
    rowi = lax.broadcasted_iota(jnp.int32, (CHUNK, 2 * CHUNK), 0)
    lm = jnp.bitwise_and(lane, CHUNK - 1)
    m0 = lane < CHUNK
    dirs = ((at_f, bt_f, kt_f, rt_f, bh_f, kh_f, pe_f, vt_f, yf_ref),
            (at_b, bt_b, kt_b, rt_b, bh_b, kh_b, pe_b, vt_b, yb_ref))
    masks = (((rowi < lm), (rowi <= lm)), ((rowi > lm), (rowi >= lm)))
    zero = jnp.zeros((CHUNK, 2 * CHUNK), F32)
    bd = lambda y: _bd(y, m0)
    bf = lambda y: y.astype(BF16)

    lane2 = lax.broadcasted_iota(jnp.int32, (2 * CHUNK, 2 * CHUNK), 1)
    cat2 = lambda y: jnp.concatenate([y, y], axis=0)
    probs = []
    for rnd in range(WKV_CPS):
        for d in range(2):
            for p in range(N_PAIRS):
                ck = rnd if d == 0 else WKV_CPS - 1 - rnd
                probs.append(dict(d=d, p=p, ck=ck, rnd=rnd,
                                  rs=slice(ck * CHUNK, (ck + 1) * CHUNK), ls=slice(p * 128, (p + 1) * 128)))

    def ld(pr, i):
        return dirs[pr["d"]][i][0, pr["rs"], pr["ls"]]

    def rhs1(pr):
        return jnp.concatenate([bd(ld(pr, 0)), bd(ld(pr, 3))], axis=0)

    for pr in probs:
        G = _dot_nt(jnp.concatenate([ld(pr, 1), ld(pr, 2)], axis=0), rhs1(pr))
        strict, incl = masks[pr["d"]]
        pr["N"] = jnp.where(strict, G[0:CHUNK, 0:128], zero)
        pr["N_br"] = bf(jnp.where(incl, G[0:CHUNK, 128:256], zero))
        pr["A_ak"] = bf(jnp.where(strict, G[CHUNK:128, 0:128], zero))
        pr["N_kr"] = bf(jnp.where(incl, G[CHUNK:128, 128:256], zero))
    for pr in probs:
        Ab = bf(pr["N"])
        pr["M"] = _dot(Ab, bd(Ab))
    for _ in range(4):
        for pr in probs:
            Mb = bf(pr["M"])
            Rm = _dot(jnp.concatenate([bf(pr["N"]), Mb], axis=0), bd(Mb))
            pr["N"] = pr["N"] + pr["M"] + Rm[0:CHUNK]
            pr["M"] = Rm[CHUNK:2 * CHUNK]
    for pr in probs:
        pr["N"] = bf(pr["N"] + pr["M"] + _dot(bf(pr["N"]), bd(bf(pr["M"]))))
        del pr["M"]
    for pr in probs:
        bh = ld(pr, 4)
        NZ = _dot(pr["N"], jnp.concatenate([bd(pr["N_br"]), bd(bh)], axis=1))
        z_br = bf(pr["N_br"].astype(F32) + NZ[:, 0:128])
        z_bh = bf(bh.astype(F32) + NZ[:, 128:256])
        pr["Z"] = jnp.concatenate([bd(z_br), bd(z_bh)], axis=1)
    for pr in probs:
        tile, half = pr["ck"] // 2, pr["ck"] % 2
        vt_p = dirs[pr["d"]][7][0, pr["ls"], tile * 128:(tile + 1) * 128]
        in_half = (lane2 < CHUNK) if half == 0 else (lane2 >= CHUNK)
        vtm = jnp.where(in_half, vt_p, jnp.zeros_like(vt_p))
        VG = _dot(vtm, jnp.concatenate([cat2(pr["A_ak"]), cat2(pr["N_kr"]), cat2(ld(pr, 5))], axis=1))
        pr["VA"] = _sel(VG[:, 0:128], m0)
        pr["VN"] = _sel(VG[:, 128:256], m0)
        pr["VK"] = _sel(VG[:, 256:384], m0)

    S = {(d, p): s_ref[d, p] for d in range(2) for p in range(N_PAIRS)}
    ys = {}
    for rnd in range(WKV_CPS):
        cur = [pr for pr in probs if pr["rnd"] == rnd]
        for pr in cur:
            St = S[(pr["d"], pr["p"])]
            SG = _dot_nt(bf(St), rhs1(pr))
            pr["X"] = bf(SG[:, 0:128] + pr["VA"])
            pr["Y"] = SG[:, 128:256] + pr["VN"]
        for pr in cur:
            UG = _dot(pr["X"], pr["Z"])
            key = (pr["d"], pr["p"])
            pend = dirs[pr["d"]][6][0, pr["ck"], 0:1, pr["ls"]]
            S[key] = S[key] * pend + UG[:, 128:256] + pr["VK"]
            ys[(pr["d"], pr["p"], pr["ck"])] = pr["Y"] + UG[:, 0:128]
    for d in range(2):
        y_ref = dirs[d][8]
        for p in range(N_PAIRS):
            s_ref[d, p] = S[(d, p)]
            for tile in range(WKV_CPS // 2):
                y0, y1 = ys[(d, p, 2 * tile)], ys[(d, p, 2 * tile + 1)]
                ts = slice(tile * 128, (tile + 1) * 128)
                y_ref[0, p * 128:p * 128 + CHUNK, ts] = jnp.where(m0, y0, pltpu.roll(y1, CHUNK, axis=1)).astype(BF16)
                y_ref[0, p * 128 + CHUNK:(p + 1) * 128, ts] = jnp.where(m0, pltpu.roll(y0, CHUNK, axis=1), y1).astype(BF16)


def _wkv(prep, n_ctx):
    (at_f, bt_f, kt_f, rt_f, bh_f, kh_f, pe_f, at_b, bt_b, kt_b, rt_b, bh_b, kh_b, pe_b, vt) = prep
    B, T, _ = at_f.shape
    blk = WKV_CPS * CHUNK
    assert T % blk == 0 and n_ctx % blk == 0
    ns = T // blk
    nc2 = n_ctx // blk

    def mrev(s):
        return jnp.where(s < nc2, nc2 - 1 - s, ns - 1 - (s - nc2))

    def specs(idx):
        tm = pl.BlockSpec((1, blk, D_HEADS), lambda b, s: (b, idx(s), 0))
        pe = pl.BlockSpec((1, WKV_CPS, 8, D_HEADS), lambda b, s: (b, idx(s), 0, 0))
        vts = pl.BlockSpec((1, D_HEADS, blk), lambda b, s: (b, 0, idx(s)))
        return [tm] * 6 + [pe, vts]

    fwd = lambda s: s
    yt = lambda idx: pl.BlockSpec((1, D_HEADS, blk), lambda b, s: (b, 0, idx(s)))
    return pl.pallas_call(
        _wkv_kernel,
        grid=(B, ns),
        in_specs=specs(fwd) + specs(mrev),
        out_specs=[yt(fwd), yt(mrev)],
        out_shape=[jax.ShapeDtypeStruct((B, D_HEADS, T), BF16)] * 2,
        scratch_shapes=[pltpu.VMEM((2, N_PAIRS, CHUNK, 2 * CHUNK), F32)],
        compiler_params=_cparams(("arbitrary", "arbitrary")),
    )(at_f, bt_f, kt_f, rt_f, bh_f, kh_f, pe_f, vt, at_b, bt_b, kt_b, rt_b, bh_b, kh_b, pe_b, vt)


def _rwpost_kernel(yf_ref, yb_ref, bonus_ref, g_ref, gg_ref, gb_ref, o_ref):
    y = yf_ref[0].astype(F32) + yb_ref[0].astype(F32)
    R = y.shape[1]
    y3 = y.reshape(N_HEADS, HEAD_DIM, R)
    mu = jnp.mean(y3, axis=1, keepdims=True)
    var = jnp.mean(jnp.square(y3 - mu), axis=1, keepdims=True)
    yn = ((y3 - mu) * lax.rsqrt(var + GN_EPS)).reshape(D_HEADS, R)
    out = (yn.T * gg_ref[...] + gb_ref[...] + bonus_ref[0].astype(F32)) * g_ref[0].astype(F32)
    o_ref[0] = out.astype(BF16)


def _rwpost(yf, yb, bonus, g, gn_g, gn_b):
    B, _, T = yf.shape
    nt = T // ROW_TILE
    ytile = pl.BlockSpec((1, D_HEADS, ROW_TILE), lambda b, t: (b, 0, t))
    tm = pl.BlockSpec((1, ROW_TILE, D_HEADS), lambda b, t: (b, t, 0))
    row = pl.BlockSpec((1, D_HEADS), lambda b, t: (0, 0))
    return pl.pallas_call(
        _rwpost_kernel,
        grid=(B, nt),
        in_specs=[ytile, ytile, tm, tm, row, row],
        out_specs=tm,
        out_shape=jax.ShapeDtypeStruct((B, T, D_HEADS), BF16),
        compiler_params=_cparams(("arbitrary", "arbitrary")),
    )(yf, yb, bonus, g, gn_g, gn_b)


ATT_ROWS = 4
WKV_CPS = 4


def _attn_kernel(q_ref, k_ref, v_ref, *rest, n_ctx, n_rows):
    bias_refs, o_ref = rest[:ATT_ROWS], rest[ATT_ROWS]
    j = pl.program_id(1)
    n_cstep = n_ctx // (GRID_W * ATT_ROWS)
    lane = lax.broadcasted_iota(jnp.int32, (GRID_W, 128), 1)
    m0 = lane < HEAD_DIM
    win = WIN_ROWS * GRID_W
    rmax = lambda a: jnp.max(a, axis=-1, keepdims=True)
    rsum = lambda a: jnp.sum(a, axis=-1, keepdims=True)
    probs = [(u, p, slice(u * GRID_W, (u + 1) * GRID_W), slice(p * 128, (p + 1) * 128))
             for u in range(ATT_ROWS) for p in range(N_PAIRS)]

    def stacked_q(rs, ls):
        return _bd(q_ref[0, rs, ls], m0)

    @pl.when(j < n_cstep)
    def _():
        sc = [_dot_nt(stacked_q(rs, ls), k_ref[0, 0:n_ctx, ls]) for _, _, rs, ls in probs]
        mx = [rmax(a) for a in sc]
        ex = [jnp.exp(a - m) for a, m in zip(sc, mx)]
        den = [rsum(e) for e in ex]
        for n, (_, _, rs, ls) in enumerate(probs):
            o = _dot(ex[n].astype(BF16), v_ref[0, 0:n_ctx, ls]) / den[n]
            o_ref[0, rs, ls] = _sel(o, m0).astype(BF16)

    @pl.when(j >= n_cstep)
    def _():
        starts = []
        for u in range(ATT_ROWS):
            i = (j - n_cstep) * ATT_ROWS + u
            r0 = jnp.clip(i - WIN_ROWS // 2, 0, n_rows - WIN_ROWS)
            starts.append(pl.multiple_of(n_ctx + r0 * GRID_W, GRID_W))
        qs = [stacked_q(rs, ls) for _, _, rs, ls in probs]
        s_loc = [_dot_nt(qs[n], k_ref[0, pl.ds(starts[u], win), ls]) + bias_refs[u][0, p]
                 for n, (u, p, _, ls) in enumerate(probs)]
        s_ctx = [_dot_nt(qs[n], k_ref[0, 0:n_ctx, ls]) for n, (_, _, _, ls) in enumerate(probs)]
        mx = [jnp.maximum(rmax(a), rmax(b)) for a, b in zip(s_loc, s_ctx)]
        e_loc = [jnp.exp(a - m) for a, m in zip(s_loc, mx)]
        e_ctx = [jnp.exp(a - m) for a, m in zip(s_ctx, mx)]
        den = [rsum(a) + rsum(b) for a, b in zip(e_loc, e_ctx)]
        for n, (u, _, rs, ls) in enumerate(probs):
            o = _dot(e_loc[n].astype(BF16), v_ref[0, pl.ds(starts[u], win), ls])
            o = (o + _dot(e_ctx[n].astype(BF16), v_ref[0, 0:n_ctx, ls])) / den[n]
            o_ref[0, rs, ls] = _sel(o, m0).astype(BF16)


def _attention(qkv, bias_tab, n_ctx):
    B, T, _ = qkv.shape
    n_rows = (T - n_ctx) // GRID_W
    n_cstep = n_ctx // (GRID_W * ATT_ROWS)
    half = WIN_ROWS // 2
    blk = GRID_W * ATT_ROWS

    def delta(j, u):
        i = jnp.maximum(j - n_cstep, 0) * ATT_ROWS + u
        return jnp.minimum(i, half) + jnp.maximum(i - (n_rows - half), 0)

    bias_specs = [pl.BlockSpec((1, N_PAIRS, 128, WIN_ROWS * GRID_W), functools.partial(
        lambda b, j, u: (delta(j, u), 0, 0, 0), u=u)) for u in range(ATT_ROWS)]
    return pl.pallas_call(
        functools.partial(_attn_kernel, n_ctx=n_ctx, n_rows=n_rows),
        grid=(B, T // blk),
        in_specs=[
            pl.BlockSpec((1, blk, D_HEADS), lambda b, j: (b, j, 0)),
            pl.BlockSpec((1, T, D_HEADS), lambda b, j: (b, 0, 1)),
            pl.BlockSpec((1, T, D_HEADS), lambda b, j: (b, 0, 2)),
        ] + bias_specs,
        out_specs=pl.BlockSpec((1, blk, D_HEADS), lambda b, j: (b, j, 0)),
        out_shape=jax.ShapeDtypeStruct((B, T, D_HEADS), BF16),
        compiler_params=_cparams(("arbitrary", "arbitrary")),
    )(qkv, qkv, qkv, *([bias_tab] * ATT_ROWS))


def _na_bias_table(rpb):
    H = rpb.shape[0]
    c = np.arange(GRID_W)[:, None]
    kc = np.arange(GRID_W)[None, :]
    cs = np.clip(c - WIN_COLS // 2, 0, GRID_W - WIN_COLS)
    valid = (kc >= cs) & (kc < cs + WIN_COLS)
    cidx = np.clip(kc - c + (WIN_COLS - 1), 0, 2 * WIN_COLS - 2)
    t = jnp.where(valid[None, None], rpb[:, :, cidx], NEG)
    t = t.transpose(0, 2, 1, 3).astype(F32)
    tabs = [t[:, :, WIN_ROWS - 1 - dl:2 * WIN_ROWS - 1 - dl, :].reshape(H, GRID_W, WIN_ROWS * GRID_W)
            for dl in range(WIN_ROWS)]
    return jnp.stack(tabs, 0).reshape(WIN_ROWS, H // 2, 2 * GRID_W, WIN_ROWS * GRID_W)


def _layer_norm(h, g, b):
    mu = jnp.mean(h, axis=-1, keepdims=True)
    var = jnp.mean(jnp.square(h - mu), axis=-1, keepdims=True)
    return (h - mu) * lax.rsqrt(var + LN_EPS) * g + b


def _to_token_tiles(ref, val):
    n = val.shape[0]
    for j in range(TOK_SUB):
        ref[pl.ds(j, n, stride=TOK_SUB), :] = val[:, j * 128:(j + 1) * 128]


def _from_token_tiles(ref, n):
    return [ref[pl.ds(j, n, stride=TOK_SUB), :] for j in range(TOK_SUB)]


CLS_ROWS = 64
_EXPERT_ORDER = np.arange(N_EXPERTS).reshape(N_GROUPS, EXPERTS_PER_GROUP).T.reshape(-1)


def _route_tile(logits_t, bias_col, tri_t, carry):
    n = logits_t.shape[1]
    neg = jnp.float32(-jnp.inf)
    scores = jax.nn.sigmoid(logits_t)
    sel = scores + bias_col
    s = [sel[N_GROUPS * k:N_GROUPS * (k + 1)] for k in range(EXPERTS_PER_GROUP)]
    c = [scores[N_GROUPS * k:N_GROUPS * (k + 1)] for k in range(EXPERTS_PER_GROUP)]
    hi01, lo01 = jnp.maximum(s[0], s[1]), jnp.minimum(s[0], s[1])
    hi23, lo23 = jnp.maximum(s[2], s[3]), jnp.minimum(s[2], s[3])
    top2 = jnp.maximum(hi01, hi23) + jnp.maximum(jnp.minimum(hi01, hi23), jnp.maximum(lo01, lo23))
    grp = lax.broadcasted_iota(jnp.int32, (N_GROUPS, n), 0).astype(F32)
    gmax = jnp.max(top2, axis=0, keepdims=True)
    g_idx = jnp.min(jnp.where(top2 == gmax, grp, jnp.float32(N_GROUPS)), axis=0, keepdims=True)
    pick = grp == g_idx
    v = [jnp.sum(jnp.where(pick, a, 0.0), axis=0, keepdims=True) for a in s]
    w = [jnp.sum(jnp.where(pick, a, 0.0), axis=0, keepdims=True) for a in c]

    def first_argmax(vals):
        best, idx = vals[0], jnp.zeros_like(vals[0])
        for k in range(1, len(vals)):
            upd = vals[k] > best
            best = jnp.where(upd, vals[k], best)
            idx = jnp.where(upd, jnp.float32(k), idx)
        return idx

    i1 = first_argmax(v)
    i2 = first_argmax([jnp.where(i1 == k, neg, v[k]) for k in range(EXPERTS_PER_GROUP)])
    lo, hi = jnp.minimum(i1, i2), jnp.maximum(i1, i2)
    g_lo = sum(jnp.where(lo == k, w[k], 0.0) for k in range(EXPERTS_PER_GROUP))
    g_hi = sum(jnp.where(hi == k, w[k], 0.0) for k in range(EXPERTS_PER_GROUP))
    tot = g_lo + g_hi
    pair = lo * 3.0 - lo * (lo - 1.0) * 0.5 + (hi - lo - 1.0)
    cls = g_idx * 6.0 + pair
    ranks = []
    m = tri_t.shape[0]
    crow = lax.broadcasted_iota(jnp.int32, (CLS_ROWS, m), 0).astype(F32)
    for q in range(n // m):
        onehot = crow == cls[:, q * m:(q + 1) * m]
        within = _dot(onehot.astype(BF16), tri_t)
        ranks.append(jnp.sum(jnp.where(onehot, within + carry, 0.0), axis=0, keepdims=True))
        carry = carry + jnp.sum(onehot.astype(F32), axis=1, keepdims=True)
    zero = jnp.zeros((4, n), F32)
    return jnp.concatenate([g_lo / tot, g_hi / tot, cls, jnp.concatenate(ranks, axis=1), zero], axis=0), carry


def _route_kernel(lg_ref, rb_ref, tri_ref, route_ref, cnt_ref, carry_ref):
    @pl.when(pl.program_id(0) == 0)
    def _():
        carry_ref[...] = jnp.zeros_like(carry_ref)

    route, carry = _route_tile(lg_ref[...], rb_ref[...], tri_ref[...], carry_ref[:, 0:1])
    route_ref[...] = route
    carry_ref[...] = jnp.broadcast_to(carry, carry_ref.shape)
    cnt_ref[...] = jnp.broadcast_to(carry, cnt_ref.shape)


def _route(logits_t, router_bias):
    N = logits_t.shape[1]
    tile = next(t for t in (8 * ROW_TILE, 4 * ROW_TILE, 2 * ROW_TILE, ROW_TILE) if N % t == 0)
    r = np.arange(ROW_TILE)
    tri = jnp.asarray(r[:, None] < r[None, :], BF16)
    bias_col = router_bias.astype(F32)[_EXPERT_ORDER].reshape(N_EXPERTS, 1)
    full = lambda a: pl.BlockSpec(a.shape, lambda i: (0, 0))
    return pl.pallas_call(
        _route_kernel,
        grid=(N // tile,),
        in_specs=[pl.BlockSpec((N_EXPERTS, tile), lambda i: (0, i)), full(bias_col), full(tri)],
        out_specs=[pl.BlockSpec((8, tile), lambda i: (0, i)), pl.BlockSpec((CLS_ROWS, 128), lambda i: (0, 0))],
        out_shape=[jax.ShapeDtypeStruct((8, N), F32), jax.ShapeDtypeStruct((CLS_ROWS, 128), F32)],
        scratch_shapes=[pltpu.VMEM((CLS_ROWS, 128), F32)],
        compiler_params=_cparams(("arbitrary",)),
    )(logits_t, bias_col, tri)


def _outproj_kernel(na_ref, rw_ref, w_ref, zc_ref, zx_ref, mod_ref, g_ref, b_ref, rwh_ref, rwl_ref,
                    z1_ref, hx_ref, lg_ref, *, d, alpha, n_ctx_tiles, t0):
    o = _dot(na_ref[0], w_ref[0:D_HEADS, :]) + _dot(rw_ref[0], w_ref[D_HEADS:2 * D_HEADS, :])
    mod = mod_ref[0, 0]
    gate = mod[:, 2 * d:3 * d]
    z = _z_tile(zc_ref, zx_ref, n_ctx_tiles, t0)
    z1 = _layer_norm(alpha * z + gate * o, g_ref[...], b_ref[...])
    z1_ref[0] = z1
    hx = z1 * (1.0 + mod[:, 4 * d:5 * d]) + mod[:, 3 * d:4 * d]
    _to_token_tiles(hx_ref, hx)
    hx_hi, hx_lo = _split2(hx)
    wh, wl = rwh_ref[...], rwl_ref[...]
    logits = _dot(hx_hi, wh) + _dot(hx_hi, wl) + _dot(hx_lo, wh)
    lg_ref[...] = logits.T[0:N_EXPERTS]


def _outproj(na, rw, w_out_b, zc, zx, sub, modsel, ln_g, ln_b, router_w, n_ctx_tiles, t0, alpha):
    B, T, _ = na.shape
    D = zc.shape[2]
    nt = T // ROW_TILE - t0
    half = pl.BlockSpec((1, ROW_TILE, D_HEADS), lambda b, t: (b, t + t0, 0))
    tile = pl.BlockSpec((1, ROW_TILE, D), lambda b, t: (b, t, 0))
    row = pl.BlockSpec((1, D), lambda b, t: (0, 0))
    cspec, xspec = _z_specs(D, n_ctx_tiles, sub, t0)
    wt = jnp.pad(router_w[:, _EXPERT_ORDER], ((0, 0), (0, 128 - N_EXPERTS)))
    wt_hi = wt.astype(BF16)
    wt_lo = (wt - wt_hi.astype(F32)).astype(BF16)
    full = lambda a: pl.BlockSpec(a.shape, lambda b, t: (0, 0))
    return pl.pallas_call(
        functools.partial(_outproj_kernel, d=D, alpha=alpha, n_ctx_tiles=n_ctx_tiles, t0=t0),
        grid=(B, nt),
        in_specs=[
            half, half,
            pl.BlockSpec((2 * D_HEADS, D), lambda b, t: (0, 0)),
            cspec, xspec,
            _mod_spec(D, n_ctx_tiles, t0),
            row, row,
            full(wt_hi), full(wt_lo),
        ],
        out_specs=[
            tile,
            pl.BlockSpec((ROW_TILE * TOK_SUB, 128), lambda b, t: (b * nt + t, 0)),
            pl.BlockSpec((N_EXPERTS, ROW_TILE), lambda b, t: (0, b * nt + t)),
        ],
        out_shape=[
            jax.ShapeDtypeStruct((B, nt * ROW_TILE, D), F32),
            jax.ShapeDtypeStruct((B * nt * ROW_TILE * TOK_SUB, 128), F32),
            jax.ShapeDtypeStruct((N_EXPERTS, B * nt * ROW_TILE), F32),
        ],
        compiler_params=_cparams(("arbitrary", "arbitrary")),
    )(na, rw, w_out_b, zc, zx, modsel, ln_g, ln_b, wt_hi, wt_lo)


def _tok(ref, i):
    return ref.at[pl.ds(pl.multiple_of(i * TOK_SUB, TOK_SUB), TOK_SUB)]


def _dispatch_kernel(dest_ref, hx_ref, xs_in, xs_out, sem):
    del xs_in
    n = dest_ref.shape[0]

    def start(i, c):
        for par in range(2):
            r = 2 * i + par
            pltpu.make_async_copy(_tok(hx_ref, r), _tok(xs_out, dest_ref[r]), sem).start(priority=par)
        return c

    lax.fori_loop(0, n // 2, start, 0, unroll=4)

    def wait(r, c):
        pltpu.make_async_copy(_tok(hx_ref, 0), _tok(xs_out, 0), sem).wait()
        return c

    lax.fori_loop(0, n, wait, 0, unroll=8)


def _dispatch(dest, hx_tiles, n_rows_pad):
    N = dest.shape[0]
    tile = next(t for t in (4 * ROW_TILE, 2 * ROW_TILE, ROW_TILE) if N % t == 0)
    xs0 = jnp.zeros((n_rows_pad * TOK_SUB, 128), F32)
    return pl.pallas_call(
        _dispatch_kernel,
        grid=(N // tile,),
        in_specs=[
            pl.BlockSpec((tile,), lambda i: (i,), memory_space=pltpu.SMEM),
            pl.BlockSpec((tile * TOK_SUB, 128), lambda i: (i, 0)),
            pl.BlockSpec(memory_space=pl.ANY),
        ],
        out_specs=pl.BlockSpec(memory_space=pl.ANY),
        out_shape=jax.ShapeDtypeStruct((n_rows_pad * TOK_SUB, 128), F32),
        scratch_shapes=[pltpu.SemaphoreType.DMA(())],
        input_output_aliases={2: 0},
        compiler_params=_cparams(("arbitrary",)),
    )(dest, hx_tiles, xs0)


def _pack_bf16_pair(a, b):
    ha = lax.bitcast_convert_type(a.astype(BF16).astype(F32), jnp.uint32)
    hb = lax.bitcast_convert_type(b.astype(BF16).astype(F32), jnp.uint32)
    return jnp.bitwise_or(ha, jnp.right_shift(hb, jnp.uint32(16)))


def _unpack_bf16_pair(w):
    a = lax.bitcast_convert_type(jnp.bitwise_and(w, jnp.uint32(0xFFFF0000)), F32)
    b = lax.bitcast_convert_type(jnp.left_shift(w, jnp.uint32(16)), F32)
    return a, b


def _expert_kernel(sc_ref, xs_ref, w1_hbm, w3_hbm, w2_hbm, ys_ref, s13, s2, c13, c2, sem, *, layer):
    i = pl.program_id(0)
    slot = sc_ref[2, i]

    def copies(sl, ea, eb):
        out = []
        for j, e in enumerate((ea, eb)):
            out.append(pltpu.make_async_copy(w1_hbm.at[layer, e], s13.at[sl, 2 * j], sem.at[sl, 3 * j]))
            out.append(pltpu.make_async_copy(w3_hbm.at[layer, e], s13.at[sl, 2 * j + 1], sem.at[sl, 3 * j + 1]))
            out.append(pltpu.make_async_copy(w2_hbm.at[layer, e], s2.at[sl, j], sem.at[sl, 3 * j + 2]))
        return out

    @pl.when(i == 0)
    def _():
        for cp in copies(0, sc_ref[3, i], sc_ref[4, i]):
            cp.start()

    for sl in range(2):
        @pl.when(jnp.logical_and(sc_ref[1, i] != 0, slot == sl))
        def _():
            for cp in copies(sl, sc_ref[3, i], sc_ref[4, i]):
                cp.wait()
            def cast(r, c):
                rows = pl.ds(pl.multiple_of(r * 64, 64), 64)
                for j in range(4):
                    c13[j, rows, :] = s13[sl, j, rows, :].astype(BF16)
                return c

            lax.fori_loop(0, s13.shape[2] // 64, cast, 0)

            def cast2(r, c):
                rows = pl.ds(pl.multiple_of(r * 64, 64), 64)
                for j in range(2):
                    c2[j, rows, :] = s2[sl, j, rows, :].astype(BF16)
                return c

            lax.fori_loop(0, s2.shape[2] // 64, cast2, 0)

            @pl.when(sc_ref[5, i] != 0)
            def _():
                for cp in copies(1 - sl, sc_ref[6, i], sc_ref[7, i]):
                    cp.start()

    @pl.when(sc_ref[0, i] != 0)
    def _():
        x = jnp.concatenate(_from_token_tiles(xs_ref, MOE_TILE), axis=1).astype(BF16)

        def ffn(j):
            h1 = _dot(x, c13[2 * j])
            h3 = _dot(x, c13[2 * j + 1])
            h = (h1 * jax.nn.sigmoid(h1)) * h3
            return _dot(h.astype(BF16), c2[j])

        _to_token_tiles(ys_ref, _pack_bf16_pair(ffn(0), ffn(1)))

    @pl.when(sc_ref[0, i] == 0)
    def _():
        ys_ref[...] = jnp.zeros_like(ys_ref)


def _experts(blk_sched, xs, w1, w3, w2, l):
    nb = xs.shape[0] // (MOE_TILE * TOK_SUB)
    _, _, D, DE = w1.shape
    tok = pl.BlockSpec((MOE_TILE * TOK_SUB, 128), lambda i, sc: (i, 0))
    hbm = pl.BlockSpec(memory_space=pl.ANY)
    grid_spec = pltpu.PrefetchScalarGridSpec(
        num_scalar_prefetch=1,
        grid=(nb,),
        in_specs=[tok, hbm, hbm, hbm],
        out_specs=tok,
        scratch_shapes=[
            pltpu.VMEM((2, 4, D, DE), F32),
            pltpu.VMEM((2, 2, DE, D), F32),
            pltpu.VMEM((4, D, DE), BF16),
            pltpu.VMEM((2, DE, D), BF16),
            pltpu.SemaphoreType.DMA((2, 6)),
        ],
    )
    return pl.pallas_call(
        functools.partial(_expert_kernel, layer=l),
        grid_spec=grid_spec,
        out_shape=jax.ShapeDtypeStruct(xs.shape, jnp.uint32),
        compiler_params=_cparams(("arbitrary",)),
    )(blk_sched, xs, w1, w3, w2)


def _combine_kernel(dest_ref, dnext_ref, ys_ref, gate_ref, z1_ref, mod_ref, g_ref, b_ref, o_ref, buf, sem,
                    *, d, alpha):
    n = dest_ref.shape[0]
    step = pl.program_id(0) * pl.num_programs(1) + pl.program_id(1)
    n_steps = pl.num_programs(0) * pl.num_programs(1)
    slot = step % 2

    def gather(idx_ref, sl):
        def start(i, c):
            for par in range(2):
                r = 2 * i + par
                pltpu.make_async_copy(_tok(ys_ref, idx_ref[r]), _tok(buf.at[sl], r), sem.at[sl]).start(priority=par)
            return c

        lax.fori_loop(0, n // 2, start, 0, unroll=4)

    @pl.when(step == 0)
    def _():
        gather(dest_ref, 0)

    @pl.when(step + 1 < n_steps)
    def _():
        gather(dnext_ref, 1 - slot)

    def wait(r, c):
        pltpu.make_async_copy(_tok(ys_ref, 0), _tok(buf.at[slot], 0), sem.at[slot]).wait()
        return c

    lax.fori_loop(0, n, wait, 0, unroll=8)
    gates = gate_ref[...].T
    ga = gates[:, 0:1]
    gb = gates[:, 1:2]
    parts = []
    for w in _from_token_tiles(buf.at[slot], n):
        fa, fb = _unpack_bf16_pair(w)
        parts.append(ga * fa + gb * fb)
    y = jnp.concatenate(parts, axis=1)
    gate = mod_ref[0, 0][:, 5 * d:6 * d]
    o_ref[0] = _layer_norm(alpha * z1_ref[0] + gate * y, g_ref[...], b_ref[...])


def _combine(dest, ys, gates, z1, modsel, ln_g, ln_b, n_ctx_tiles, t0, t_out, alpha):
    B, T1, D = z1.shape
    nt1 = T1 // ROW_TILE
    skip = t_out - t0
    nt = nt1 - skip
    tile = lambda off: pl.BlockSpec((1, ROW_TILE, D), lambda b, t: (b, t + off, 0))
    row = pl.BlockSpec((1, D), lambda b, t: (0, 0))

    def nxt(b, t):
        last = jnp.logical_and(b == B - 1, t == nt - 1)
        wrap = t == nt - 1
        b2 = jnp.where(jnp.logical_and(wrap, jnp.logical_not(last)), b + 1, b)
        t2 = jnp.where(last, t, jnp.where(wrap, 0, t + 1))
        return b2 * nt1 + t2 + skip

    return pl.pallas_call(
        functools.partial(_combine_kernel, d=D, alpha=alpha),
        grid=(B, nt),
        in_specs=[
            pl.BlockSpec((ROW_TILE,), lambda b, t: (b * nt1 + t + skip,), memory_space=pltpu.SMEM),
            pl.BlockSpec((ROW_TILE,), lambda b, t: (nxt(b, t),), memory_space=pltpu.SMEM),
            pl.BlockSpec(memory_space=pl.ANY),
            pl.BlockSpec((8, ROW_TILE), lambda b, t: (0, b * nt1 + t + skip)),
            tile(skip),
            _mod_spec(D, n_ctx_tiles, t_out),
            row, row,
        ],
        out_specs=tile(0),
        out_shape=jax.ShapeDtypeStruct((B, nt * ROW_TILE, D), F32),
        scratch_shapes=[pltpu.VMEM((2, ROW_TILE * TOK_SUB, 128), jnp.uint32), pltpu.SemaphoreType.DMA((2,))],
        compiler_params=_cparams(("arbitrary", "arbitrary")),
    )(dest, dest, ys, gates, z1, modsel, ln_g, ln_b)


_PAIR_LO = np.array([0, 0, 0, 1, 1, 2], np.int32)
_PAIR_HI = np.array([1, 2, 3, 2, 3, 3], np.int32)


def _schedule(route, cnt):
    cls = route[2].astype(jnp.int32)
    rank = route[3].astype(jnp.int32)
    N = cls.shape[0]
    counts = cnt[:N_CLASSES, 0].astype(jnp.int32)
    onehot = (cls[:, None] == jnp.arange(N_CLASSES, dtype=jnp.int32)[None, :])
    padded = (counts + MOE_TILE - 1) // MOE_TILE * MOE_TILE
    cls_end = jnp.cumsum(padded)
    cls_start = cls_end - padded
    dest = jnp.sum(jnp.where(onehot, cls_start[None, :], 0), axis=1) + rank

    nb = N // MOE_TILE + N_CLASSES
    blk_row = jnp.arange(nb, dtype=jnp.int32) * MOE_TILE
    total = jnp.sum(padded)
    blk_valid = (blk_row < total).astype(jnp.int32)
    row_c = jnp.minimum(blk_row, jnp.maximum(total - MOE_TILE, 0))
    blk_cls = jnp.sum((row_c[:, None] >= cls_end[None, :]).astype(jnp.int32), axis=1)
    blk_cls = jnp.minimum(blk_cls, N_CLASSES - 1)

    def experts_of(c):
        hot = (c % 6)[:, None] == jnp.arange(6, dtype=jnp.int32)[None, :]
        base = (c // 6) * EXPERTS_PER_GROUP
        return (base + jnp.sum(jnp.where(hot, jnp.asarray(_PAIR_LO)[None, :], 0), axis=1),
                base + jnp.sum(jnp.where(hot, jnp.asarray(_PAIR_HI)[None, :], 0), axis=1))

    prev_cls = jnp.concatenate([jnp.full((1,), -1, jnp.int32), blk_cls[:-1]])
    first = jnp.logical_and(blk_valid != 0, blk_cls != prev_cls).astype(jnp.int32)
    parity = (jnp.cumsum(first) - 1) % 2
    cls_ids = jnp.arange(N_CLASSES, dtype=jnp.int32)
    used = jnp.where(counts > 0, cls_ids, N_CLASSES)
    later = jnp.where(cls_ids[None, :] > cls_ids[:, None], used[None, :], N_CLASSES)
    nxt_of_cls = jnp.min(later, axis=1)
    cls_hot = blk_cls[:, None] == cls_ids[None, :]
    nxt_cls = jnp.sum(jnp.where(cls_hot, nxt_of_cls[None, :], 0), axis=1)
    has_next = (nxt_cls < N_CLASSES).astype(jnp.int32)
    ea, eb = experts_of(blk_cls)
    na, nb_ = experts_of(jnp.minimum(nxt_cls, N_CLASSES - 1))
    blk_sched = jnp.stack([blk_valid, first, parity, ea, eb, has_next, na, nb_]).astype(jnp.int32)
    return dest.astype(jnp.int32), blk_sched, nb * MOE_TILE


def _rope_tables(n_ctx, seq):
    t = np.arange(seq)
    row = (t // GRID_W).astype(np.float32)
    col = (t % GRID_W).astype(np.float32)
    n_freq = HEAD_DIM // 4
    inv = jnp.asarray(ROPE_BASE, F32) ** (-jnp.arange(n_freq, dtype=F32) / n_freq)
    ar = jnp.asarray(row)[:, None] * inv
    ac = jnp.asarray(col)[:, None] * inv
    ang = jnp.concatenate([ar, ar, ac, ac], -1)
    cos = jnp.cos(ang)
    sin = jnp.sin(ang)
    quarter = (np.arange(HEAD_DIM) // n_freq) % 2
    sa = jnp.where(quarter == 0, -sin, 0.0)
    sb = jnp.where(quarter == 1, sin, 0.0)
    pad = lambda a, fill: jnp.concatenate([jnp.full((n_ctx, HEAD_DIM), fill, F32), a], 0)
    tile = lambda a: jnp.tile(a, (1, N_HEADS))
    return tile(pad(cos, 1.0)), tile(pad(sa, 0.0)), tile(pad(sb, 0.0))


def _rw_consts(l, n_ctx, seq, rw_mu_prev, rw_mu_next, rw_w0, rw_w2, rw_a0, rw_a2, rw_g2, rw_k_k, rw_k_a, rw_r_k):
    d_rw_in = rw_mu_prev.shape[1]
    padw = lambda a: jnp.pad(a[l], (0, D_RWP - d_rw_in)).reshape(1, D_RWP)
    cos, sa, sb = _rope_tables(n_ctx, seq)
    w2p = jnp.zeros((128, 2 * D_HEADS), F32)
    w2p = w2p.at[0:LORA, 0:D_HEADS].set(rw_w2[l, 0]).at[LORA:2 * LORA, D_HEADS:].set(rw_w2[l, 1])
    a2p = jnp.zeros((128, 2 * D_HEADS), F32)
    a2p = a2p.at[2 * LORA:3 * LORA, 0:D_HEADS].set(rw_a2[l, 0]).at[3 * LORA:4 * LORA, D_HEADS:].set(rw_a2[l, 1])
    g2p = jnp.zeros((128, D_HEADS), F32).at[0:GATE_LORA].set(rw_g2[l])
    head = np.arange(D_HEADS) // HEAD_DIM
    ones_bd = jnp.asarray(head[:, None] == head[None, :], BF16)
    return dict(
        mu_prev=padw(rw_mu_prev), mu_next=padw(rw_mu_next), cos=cos, sa=sa, sb=sb,
        k_k=rw_k_k[l].reshape(1, D_HEADS), k_a=rw_k_a[l].reshape(1, D_HEADS), r_k=rw_r_k[l].reshape(1, D_HEADS),
        w0=rw_w0[l].reshape(1, 2 * D_HEADS), a0=rw_a0[l].reshape(1, 2 * D_HEADS),
        w2p=w2p.astype(BF16), a2p=a2p.astype(BF16), g2p=g2p.astype(BF16), ones_bd=ones_bd,
    )


def kernel(x, c, ctx, c_ctx, ada_w, ada_b, w_in, na_rpb, rw_mu_prev, rw_mu_next, rw_w0, rw_w2, rw_a0, rw_a2, rw_g2, rw_k_k, rw_k_a, rw_r_k, rw_gn_g, rw_gn_b, w_out, ln1_g, ln1_b, ln2_g, ln2_b, router_w, router_bias, exp_w1, exp_w3, exp_w2):
    B, S, D = x.shape
    C = ctx.shape[1]
    L = ada_w.shape[0]
    T = C + S
    assert D == 1024 and C % ROW_TILE == 0 and S % ROW_TILE == 0 and C % (2 * CHUNK) == 0
    assert S % GRID_W == 0 and S // GRID_W >= WIN_ROWS and w_in.shape[2] == D_INP - 32
    n_ctx_tiles = C // ROW_TILE
    alpha = float((2 * L) ** 0.25)

    zc, zx, sub = ctx, x, n_ctx_tiles
    n_mod = (B + 1 + 7) // 8 * 8
    cc = jnp.zeros((n_mod, D), F32).at[0:B].set(c).at[B].set(c_ctx)
    mod_all = _ada(cc, ada_w, ada_b)

    for l in range(L):
        mod_c = jnp.broadcast_to(mod_all[l, B][None], (B, 6 * D))
        modsel = jnp.stack([mod_c, mod_all[l, 0:B]], axis=1).reshape(B, 2, 1, 6 * D)
        w_in_p = jnp.pad(w_in[l], ((0, 0), (0, D_INP - w_in.shape[2]))).astype(BF16)
        qkv, p_rw = _inproj(zc, zx, sub, T, modsel, w_in_p, n_ctx_tiles)

        consts = _rw_consts(l, C, S, rw_mu_prev, rw_mu_next, rw_w0, rw_w2, rw_a0, rw_a2, rw_g2,
                            rw_k_k, rw_k_a, rw_r_k)
        prep = _rwprep(p_rw, consts, n_ctx_tiles)
        yf, yb = _wkv(prep[0:15], C)
        rw = _rwpost(yf, yb, prep[15], prep[16], rw_gn_g[l].reshape(1, D_HEADS), rw_gn_b[l].reshape(1, D_HEADS))

        na = _attention(qkv, _na_bias_table(na_rpb[l]), C)

        t0 = n_ctx_tiles if l == L - 1 else 0
        z1, hx_tiles, logits_t = _outproj(na, rw, w_out[l].astype(BF16), zc, zx, sub, modsel,
                                          ln1_g[l].reshape(1, D), ln1_b[l].reshape(1, D), router_w,
                                          n_ctx_tiles, t0, alpha)
        route, cnt = _route(logits_t, router_bias)
        dest, blk_sched, n_rows_pad = _schedule(route, cnt)
        xs = _dispatch(dest, hx_tiles, n_rows_pad)
        ys = _experts(blk_sched, xs, exp_w1, exp_w3, exp_w2, l)
        z = _combine(dest, ys, route, z1, modsel, ln2_g[l].reshape(1, D), ln2_b[l].reshape(1, D),
                     n_ctx_tiles, t0, t0, alpha)
        zc, zx, sub = z, z, 0

    return z
```

```python
import functools
import math

import jax
import jax.numpy as jnp
import numpy as np
from jax import lax
from jax.experimental import pallas as pl
from jax.experimental.pallas import tpu as pltpu

F32 = jnp.float32
BF16 = jnp.bfloat16

HEAD_DIM = 64
N_HEADS = 8
D_HEADS = N_HEADS * HEAD_DIM
N_PAIRS = N_HEADS // 2
GRID_W = 64
WIN_ROWS = 8
WIN_COLS = 16
LORA = 32
GATE_LORA = 96
N_EXPERTS = 32
N_GROUPS = 8
EXPERTS_PER_GROUP = 4
N_CLASSES = N_GROUPS * 6
ROPE_BASE = 10000.0
LN_EPS = 1e-6
GN_EPS = 64e-5
CHUNK = 64
ROW_TILE = 256
MOE_TILE = 256
D_QKV = 3 * D_HEADS
D_RWP = 3 * D_HEADS + 256
D_INP = D_QKV + D_RWP
TOK_SUB = 8
NEG = -1e30
VMEM_LIMIT = 56 * 1024 * 1024


def _cparams(sem):
    return pltpu.CompilerParams(dimension_semantics=sem, vmem_limit_bytes=VMEM_LIMIT)


def _dot(a, b):
    return jnp.dot(a, b, preferred_element_type=F32)


def _dot_nt(a, b):
    return lax.dot_general(a, b, (((1,), (1,)), ((), ())), preferred_element_type=F32)


def _split2(a):
    hi = a.astype(BF16)
    lo = (a - hi.astype(F32)).astype(BF16)
    return hi, lo


def _split3(a):
    hi = a.astype(BF16)
    r1 = a - hi.astype(F32)
    mid = r1.astype(BF16)
    lo = (r1 - mid.astype(F32)).astype(BF16)
    return hi, mid, lo


def _dot3(a, b):
    ah, al = _split2(a)
    bh, bl = _split2(b)
    return _dot(ah, bh) + _dot(al, bh) + _dot(ah, bl)


def _dot_exact_rhs(a, b_exact):
    h, m, l = _split3(a)
    return _dot(h, b_exact) + _dot(m, b_exact) + _dot(l, b_exact)


def _dot_exact_lhs(a_exact, b):
    h, m, l = _split3(b)
    return _dot(a_exact, h) + _dot(a_exact, m) + _dot(a_exact, l)


def _ada_kernel(cc_ref, w_ref, b_ref, o_ref):
    cc = cc_ref[...]
    s = cc * jax.nn.sigmoid(cc)
    o_ref[0] = _dot3(s, w_ref[0]) + b_ref[0]


def _ada(cc, ada_w, ada_b):
    L, D, D6 = ada_w.shape
    R = cc.shape[0]
    tn = 1536
    return pl.pallas_call(
        _ada_kernel,
        grid=(L, D6 // tn),
        in_specs=[
            pl.BlockSpec((R, D), lambda l, n: (0, 0)),
            pl.BlockSpec((1, D, tn), lambda l, n: (l, 0, n)),
            pl.BlockSpec((1, 1, tn), lambda l, n: (l, 0, n)),
        ],
        out_specs=pl.BlockSpec((1, R, tn), lambda l, n: (l, 0, n)),
        out_shape=jax.ShapeDtypeStruct((L, R, D6), F32),
        compiler_params=_cparams(("arbitrary", "arbitrary")),
    )(cc, ada_w, ada_b.reshape(L, 1, D6))


def _z_specs(D, n_ctx_tiles, sub, t0=0):
    cspec = pl.BlockSpec((1, ROW_TILE, D), lambda b, t: (b, jnp.minimum(t + t0, n_ctx_tiles - 1), 0))
    xspec = pl.BlockSpec((1, ROW_TILE, D), lambda b, t: (b, jnp.maximum(t + t0, n_ctx_tiles) - sub, 0))
    return cspec, xspec


def _z_tile(zc_ref, zx_ref, n_ctx_tiles, t0=0):
    return jnp.where(pl.program_id(1) + t0 < n_ctx_tiles, zc_ref[0], zx_ref[0])


def _mod_spec(D, n_ctx_tiles, t0=0):
    return pl.BlockSpec((1, 1, 1, 6 * D), lambda b, t: (b, jnp.where(t + t0 < n_ctx_tiles, 0, 1), 0, 0))


def _inproj_kernel(zc_ref, zx_ref, mod_ref, w_ref, qkv_ref, rw_ref, *, d, n_ctx_tiles):
    z = _z_tile(zc_ref, zx_ref, n_ctx_tiles)
    mod = mod_ref[0, 0]
    shift = mod[:, 0:d]
    scale = mod[:, d:2 * d]
    h = (z * (1.0 + scale) + shift).astype(BF16)
    q = _dot(h, w_ref[:, 0:D_HEADS])
    qkv_ref[0, :, 0:D_HEADS] = (q * (HEAD_DIM ** -0.5)).astype(BF16)
    kv = _dot(h, w_ref[:, D_HEADS:D_QKV])
    qkv_ref[0, :, D_HEADS:D_QKV] = kv.astype(BF16)
    rw_ref[0] = _dot(h, w_ref[:, D_QKV:D_INP])


def _inproj(zc, zx, sub, T, modsel, w_in_p, n_ctx_tiles):
    B, _, D = zc.shape
    nt = T // ROW_TILE
    cspec, xspec = _z_specs(D, n_ctx_tiles, sub)
    return pl.pallas_call(
        functools.partial(_inproj_kernel, d=D, n_ctx_tiles=n_ctx_tiles),
        grid=(B, nt),
        in_specs=[
            cspec, xspec,
            pl.BlockSpec((1, 1, 1, 6 * D), lambda b, t: (b, jnp.where(t < n_ctx_tiles, 0, 1), 0, 0)),
            pl.BlockSpec((D, D_INP), lambda b, t: (0, 0)),
        ],
        out_specs=[
            pl.BlockSpec((1, ROW_TILE, D_QKV), lambda b, t: (b, t, 0)),
            pl.BlockSpec((1, ROW_TILE, D_RWP), lambda b, t: (b, t, 0)),
        ],
        out_shape=[
            jax.ShapeDtypeStruct((B, T, D_QKV), BF16),
            jax.ShapeDtypeStruct((B, T, D_RWP), F32),
        ],
        compiler_params=_cparams(("arbitrary", "arbitrary")),
    )(zc, zx, modsel, w_in_p)


def _rwprep_kernel(p_ref, pp_ref, pn_ref, mup_ref, mun_ref, cos_ref, sa_ref, sb_ref,
                   kk_ref, ka_ref, rk_ref, w0_ref, a0_ref, w2_ref, a2_ref, g2_ref, ones_ref,
                   at_f, bt_f, kt_f, rt_f, bh_f, kh_f, pe_f,
                   at_b, bt_b, kt_b, rt_b, bh_b, kh_b, pe_b,
                   vt_ref, bonus_ref, g_ref, sh_ref, *, n_ctx_tiles, n_tiles):
    t = pl.program_id(1)
    P = p_ref[0]
    R = P.shape[0]
    prev_ok = jnp.logical_and(t != 0, t != n_ctx_tiles)
    next_ok = jnp.logical_and(t != n_ctx_tiles - 1, t != n_tiles - 1)
    sh_ref[8:R + 8, :] = P
    sh_ref[0:8, :] = jnp.where(prev_ok, pp_ref[0], 0.0)
    sh_ref[R + 8:R + 16, :] = jnp.where(next_ok, pn_ref[0], 0.0)
    prev = sh_ref[7:R + 7, :]
    nxt = sh_ref[9:R + 9, :]
    mup = mup_ref[...]
    mun = mun_ref[...]
    z = (1.0 - mup - mun) * P + mup * prev + mun * nxt

    cos = cos_ref[...]
    sa = sa_ref[...]
    sb = sb_ref[...]

    def rope(u):
        q = HEAD_DIM // 4
        parts = []
        for m in range(D_HEADS // 128):
            ls = slice(m * 128, (m + 1) * 128)
            um = u[:, ls]
            parts.append(um * cos[:, ls] + pltpu.roll(um, 128 - q, axis=1) * sa[:, ls]
                         + pltpu.roll(um, q, axis=1) * sb[:, ls])
        return jnp.concatenate(parts, axis=1)

    r = rope(z[:, 0:D_HEADS])
    k = rope(z[:, D_HEADS:2 * D_HEADS])
    v = z[:, 2 * D_HEADS:3 * D_HEADS]
    ones_bd = ones_ref[...]

    kk = k * kk_ref[...]
    sq_hi, sq_lo = _split2(kk * kk)
    kk = kk * lax.rsqrt(jnp.maximum(_dot(sq_hi, ones_bd) + _dot(sq_lo, ones_bd), 1e-24))

    sigmoid = lambda u: 0.5 * jnp.tanh(0.5 * u) + 0.5
    slab = z[:, 3 * D_HEADS:3 * D_HEADS + 128]
    u_w = w0_ref[...] + _dot(jnp.tanh(slab).astype(BF16), w2_ref[...])
    u_a = a0_ref[...] + _dot(slab.astype(BF16), a2_ref[...])
    g_ref[0] = _dot(sigmoid(z[:, 3 * D_HEADS + 128:D_RWP]).astype(BF16), g2_ref[...]).astype(BF16)
    e_all = math.exp(-0.5) * sigmoid(u_w)
    a_all = sigmoid(u_a)

    ci = lax.broadcasted_iota(jnp.int32, (CHUNK, CHUNK), 0)
    cj = lax.broadcasted_iota(jnp.int32, (CHUNK, CHUNK), 1)
    ka = ka_ref[...]
    outs = ((at_f, bt_f, kt_f, rt_f, bh_f, kh_f, pe_f), (at_b, bt_b, kt_b, rt_b, bh_b, kh_b, pe_b))
    kd_sum = None
    for d in range(2):
        e = e_all[:, d * D_HEADS:(d + 1) * D_HEADS]
        a = a_all[:, d * D_HEADS:(d + 1) * D_HEADS]
        tri = (cj <= ci) if d == 0 else (cj >= ci)
        tri = tri.astype(BF16)
        cs, ce = [], []
        for q in range(R // CHUNK):
            e_hi, e_lo = _split2(e[q * CHUNK:(q + 1) * CHUNK])
            cq = -(_dot(tri, e_hi) + _dot(tri, e_lo))
            end = cq[CHUNK - 1:CHUNK] if d == 0 else cq[0:1]
            cs.append(cq)
            ce.append(jnp.broadcast_to(end, cq.shape))
        c = jnp.concatenate(cs, axis=0)
        cend = jnp.concatenate(ce, axis=0)
        kd = k * (1.0 + (a - 1.0) * ka)
        kd_sum = kd if kd_sum is None else kd_sum + kd
        beta = a * kk
        en = jnp.exp(-c)
        eh = jnp.exp(cend - c)
        o_at, o_bt, o_kt, o_rt, o_bh, o_kh, o_pe = outs[d]
        o_at[0] = (-kk * jnp.exp(c + e)).astype(BF16)
        o_bt[0] = (beta * en).astype(BF16)
        o_kt[0] = (kd * en).astype(BF16)
        o_rt[0] = (r * jnp.exp(c)).astype(BF16)
        o_bh[0] = (beta * eh).astype(BF16)
        o_kh[0] = (kd * eh).astype(BF16)
        for q in range(R // CHUNK):
            o_pe[0, q] = jnp.exp(cend[q * CHUNK:q * CHUNK + 8])
    bonus_ref[0] = (_dot((r * rk_ref[...] * kd_sum).astype(BF16), ones_bd) * v).astype(BF16)
    vt_ref[0] = v.T.astype(BF16)


def _rwprep(p_rw, consts, n_ctx_tiles):
    B, T, _ = p_rw.shape
    nt = T // ROW_TILE
    nh = ROW_TILE // 8
    row = lambda w: pl.BlockSpec((1, w), lambda b, t: (0, 0))
    full = lambda a: pl.BlockSpec(a.shape, lambda b, t: (0, 0))
    tm = pl.BlockSpec((1, ROW_TILE, D_HEADS), lambda b, t: (b, t, 0))
    tab = pl.BlockSpec((ROW_TILE, D_HEADS), lambda b, t: (t, 0))
    pe = pl.BlockSpec((1, ROW_TILE // CHUNK, 8, D_HEADS), lambda b, t: (b, t, 0, 0))
    tm_shape = jax.ShapeDtypeStruct((B, T, D_HEADS), BF16)
    pe_shape = jax.ShapeDtypeStruct((B, T // CHUNK, 8, D_HEADS), F32)
    dir_specs = [tm] * 6 + [pe]
    dir_shapes = [tm_shape] * 6 + [pe_shape]
    return pl.pallas_call(
        functools.partial(_rwprep_kernel, n_ctx_tiles=n_ctx_tiles, n_tiles=nt),
        grid=(B, nt),
        in_specs=[
            pl.BlockSpec((1, ROW_TILE, D_RWP), lambda b, t: (b, t, 0)),
            pl.BlockSpec((1, 8, D_RWP), lambda b, t: (b, jnp.maximum(t * nh - 1, 0), 0)),
            pl.BlockSpec((1, 8, D_RWP), lambda b, t: (b, jnp.minimum((t + 1) * nh, T // 8 - 1), 0)),
            row(D_RWP), row(D_RWP), tab, tab, tab,
            row(D_HEADS), row(D_HEADS), row(D_HEADS), row(2 * D_HEADS), row(2 * D_HEADS),
            full(consts["w2p"]), full(consts["a2p"]), full(consts["g2p"]), full(consts["ones_bd"]),
        ],
        out_specs=dir_specs + dir_specs + [
            pl.BlockSpec((1, D_HEADS, ROW_TILE), lambda b, t: (b, 0, t)),
            tm, tm,
        ],
        out_shape=dir_shapes + dir_shapes + [
            jax.ShapeDtypeStruct((B, D_HEADS, T), BF16),
            jax.ShapeDtypeStruct((B, T, D_HEADS), BF16),
            jax.ShapeDtypeStruct((B, T, D_HEADS), BF16),
        ],
        scratch_shapes=[pltpu.VMEM((ROW_TILE + 16, D_RWP), F32)],
        compiler_params=_cparams(("arbitrary", "arbitrary")),
    )(p_rw, p_rw, p_rw, consts["mu_prev"], consts["mu_next"], consts["cos"], consts["sa"], consts["sb"],
      consts["k_k"], consts["k_a"], consts["r_k"], consts["w0"], consts["a0"],
      consts["w2p"], consts["a2p"], consts["g2p"], consts["ones_bd"])


def _bd(y, m0):
    zero = jnp.zeros_like(y)
    return jnp.concatenate([jnp.where(m0, y, zero), jnp.where(m0, zero, y)], axis=0)


def _sel(w, m0):
    return jnp.where(m0, w[0:CHUNK], w[CHUNK:2 * CHUNK])


def _wkv_kernel(*refs):
    (at_f, bt_f, kt_f, rt_f, bh_f, kh_f, pe_f, vt_f,
     at_b, bt_b, kt_b, rt_b, bh_b, kh_b, pe_b, vt_b,
     yf_ref, yb_ref, s_ref) = refs
    s = pl.program_id(1)

    @pl.when(s == 0)
    def _():
        s_ref[...] = jnp.zeros_like(s_ref)

    lane = lax.broadcasted_iota(jnp.int32, (CHUNK, 2 * CHUNK), 1)
    rowi = lax.broadcasted_iota(jnp.int32, (CHUNK, 2 * CHUNK), 0)
    lm = jnp.bitwise_and(lane, CHUNK - 1)
    m0 = lane < CHUNK
    dirs = ((at_f, bt_f, kt_f, rt_f, bh_f, kh_f, pe_f, vt_f, yf_ref),
            (at_b, bt_b, kt_b, rt_b, bh_b, kh_b, pe_b, vt_b, yb_ref))
    masks = (((rowi < lm), (rowi <= lm)), ((rowi > lm), (rowi >= lm)))
    zero = jnp.zeros((CHUNK, 2 * CHUNK), F32)
    bd = lambda y: _bd(y, m0)
    bf = lambda y: y.astype(BF16)

    lane2 = lax.broadcasted_iota(jnp.int32, (2 * CHUNK, 2 * CHUNK), 1)
    cat2 = lambda y: jnp.concatenate([y, y], axis=0)
    probs = []
    for rnd in range(WKV_CPS):
        for d in range(2):
            for p in range(N_PAIRS):
                ck = rnd if d == 0 else WKV_CPS - 1 - rnd
                probs.append(dict(d=d, p=p, ck=ck, rnd=rnd,
                                  rs=slice(ck * CHUNK, (ck + 1) * CHUNK), ls=slice(p * 128, (p + 1) * 128)))

    def ld(pr, i):
        return dirs[pr["d"]][i][0, pr["rs"], pr["ls"]]

    def rhs1(pr):
        return jnp.concatenate([bd(ld(pr, 0)), bd(ld(pr, 3))], axis=0)

    for pr in probs:
        G = _dot_nt(jnp.concatenate([ld(pr, 1), ld(pr, 2)], axis=0), rhs1(pr))
        strict, incl = masks[pr["d"]]
        pr["N"] = jnp.where(strict, G[0:CHUNK, 0:128], zero)
        pr["N_br"] = bf(jnp.where(incl, G[0:CHUNK, 128:256], zero))
        pr["A_ak"] = bf(jnp.where(strict, G[CHUNK:128, 0:128], zero))
        pr["N_kr"] = bf(jnp.where(incl, G[CHUNK:128, 128:256], zero))
    for pr in probs:
        Ab = bf(pr["N"])
        pr["M"] = _dot(Ab, bd(Ab))
    for _ in range(4):
        for pr in probs:
            Mb = bf(pr["M"])
            Rm = _dot(jnp.concatenate([bf(pr["N"]), Mb], axis=0), bd(Mb))
            pr["N"] = pr["N"] + pr["M"] + Rm[0:CHUNK]
            pr["M"] = Rm[CHUNK:2 * CHUNK]
    for pr in probs:
        pr["N"] = bf(pr["N"] + pr["M"] + _dot(bf(pr["N"]), bd(bf(pr["M"]))))
        del pr["M"]
    for pr in probs:
        bh = ld(pr, 4)
        NZ = _dot(pr["N"], jnp.concatenate([bd(pr["N_br"]), bd(bh)], axis=1))
        z_br = bf(pr["N_br"].astype(F32) + NZ[:, 0:128])
        z_bh = bf(bh.astype(F32) + NZ[:, 128:256])
        pr["Z"] = jnp.concatenate([bd(z_br), bd(z_bh)], axis=1)
    for pr in probs:
        tile, half = pr["ck"] // 2, pr["ck"] % 2
        vt_p = dirs[pr["d"]][7][0, pr["ls"], tile * 128:(tile + 1) * 128]
        in_half = (lane2 < CHUNK) if half == 0 else (lane2 >= CHUNK)
        vtm = jnp.where(in_half, vt_p, jnp.zeros_like(vt_p))
        VG = _dot(vtm, jnp.concatenate([cat2(pr["A_ak"]), cat2(pr["N_kr"]), cat2(ld(pr, 5))], axis=1))
        pr["VA"] = _sel(VG[:, 0:128], m0)
        pr["VN"] = _sel(VG[:, 128:256], m0)
        pr["VK"] = _sel(VG[:, 256:384], m0)

    S = {(d, p): s_ref[d, p] for d in range(2) for p in range(N_PAIRS)}
    ys = {}
    for rnd in range(WKV_CPS):
        cur = [pr for pr in probs if pr["rnd"] == rnd]
        for pr in cur:
            St = S[(pr["d"], pr["p"])]
            SG = _dot_nt(bf(St), rhs1(pr))
            pr["X"] = bf(SG[:, 0:128] + pr["VA"])
            pr["Y"] = SG[:, 128:256] + pr["VN"]
        for pr in cur:
            UG = _dot(pr["X"], pr["Z"])
            key = (pr["d"], pr["p"])
            pend = dirs[pr["d"]][6][0, pr["ck"], 0:1, pr["ls"]]
            S[key] = S[key] * pend + UG[:, 128:256] + pr["VK"]
            ys[(pr["d"], pr["p"], pr["ck"])] = pr["Y"] + UG[:, 0:128]
    for d in range(2):
        y_ref = dirs[d][8]
        for p in range(N_PAIRS):
            s_ref[d, p] = S[(d, p)]
            for tile in range(WKV_CPS // 2):
                y0, y1 = ys[(d, p, 2 * tile)], ys[(d, p, 2 * tile + 1)]
                ts = slice(tile * 128, (tile + 1) * 128)
                y_ref[0, p * 128:p * 128 + CHUNK, ts] = jnp.where(m0, y0, pltpu.roll(y1, CHUNK, axis=1)).astype(BF16)
                y_ref[0, p * 128 + CHUNK:(p + 1) * 128, ts] = jnp.where(m0, pltpu.roll(y0, CHUNK, axis=1), y1).astype(BF16)


def _wkv(prep, n_ctx):
    (at_f, bt_f, kt_f, rt_f, bh_f, kh_f, pe_f, at_b, bt_b, kt_b, rt_b, bh_b, kh_b, pe_b, vt) = prep
    B, T, _ = at_f.shape
    blk = WKV_CPS * CHUNK
    assert T % blk == 0 and n_ctx % blk == 0
    ns = T // blk
    nc2 = n_ctx // blk

    def mrev(s):
        return jnp.where(s < nc2, nc2 - 1 - s, ns - 1 - (s - nc2))

    def specs(idx):
        tm = pl.BlockSpec((1, blk, D_HEADS), lambda b, s: (b, idx(s), 0))
        pe = pl.BlockSpec((1, WKV_CPS, 8, D_HEADS), lambda b, s: (b, idx(s), 0, 0))
        vts = pl.BlockSpec((1, D_HEADS, blk), lambda b, s: (b, 0, idx(s)))
        return [tm] * 6 + [pe, vts]

    fwd = lambda s: s
    yt = lambda idx: pl.BlockSpec((1, D_HEADS, blk), lambda b, s: (b, 0, idx(s)))
    return pl.pallas_call(
        _wkv_kernel,
        grid=(B, ns),
        in_specs=specs(fwd) + specs(mrev),
        out_specs=[yt(fwd), yt(mrev)],
        out_shape=[jax.ShapeDtypeStruct((B, D_HEADS, T), BF16)] * 2,
        scratch_shapes=[pltpu.VMEM((2, N_PAIRS, CHUNK, 2 * CHUNK), F32)],
        compiler_params=_cparams(("arbitrary", "arbitrary")),
    )(at_f, bt_f, kt_f, rt_f, bh_f, kh_f, pe_f, vt, at_b, bt_b, kt_b, rt_b, bh_b, kh_b, pe_b, vt)


def _rwpost_tile(yf, yb, bonus, g, gn_g, gn_b):
    y = yf.astype(F32) + yb.astype(F32)
    R = y.shape[1]
    y3 = y.reshape(N_HEADS, HEAD_DIM, R)
    mu = jnp.mean(y3, axis=1, keepdims=True)
    var = jnp.mean(jnp.square(y3 - mu), axis=1, keepdims=True)
    yn = ((y3 - mu) * lax.rsqrt(var + GN_EPS)).reshape(D_HEADS, R)
    out = (yn.T * gn_g + gn_b + bonus.astype(F32)) * g.astype(F32)
    return out.astype(BF16)


ATT_ROWS = 4
WKV_CPS = 4


def _attn_kernel(q_ref, k_ref, v_ref, *rest, n_ctx, n_rows):
    bias_refs, o_ref = rest[:ATT_ROWS], rest[ATT_ROWS]
    j = pl.program_id(1)
    n_cstep = n_ctx // (GRID_W * ATT_ROWS)
    lane = lax.broadcasted_iota(jnp.int32, (GRID_W, 128), 1)
    m0 = lane < HEAD_DIM
    win = WIN_ROWS * GRID_W
    rmax = lambda a: jnp.max(a, axis=-1, keepdims=True)
    rsum = lambda a: jnp.sum(a, axis=-1, keepdims=True)
    probs = [(u, p, slice(u * GRID_W, (u + 1) * GRID_W), slice(p * 128, (p + 1) * 128))
             for u in range(ATT_ROWS) for p in range(N_PAIRS)]

    def stacked_q(rs, ls):
        return _bd(q_ref[0, rs, ls], m0)

    @pl.when(j < n_cstep)
    def _():
        sc = [_dot_nt(stacked_q(rs, ls), k_ref[0, 0:n_ctx, ls]) for _, _, rs, ls in probs]
        mx = [rmax(a) for a in sc]
        ex = [jnp.exp(a - m) for a, m in zip(sc, mx)]
        den = [rsum(e) for e in ex]
        for n, (_, _, rs, ls) in enumerate(probs):
            o = _dot(ex[n].astype(BF16), v_ref[0, 0:n_ctx, ls]) / den[n]
            o_ref[0, rs, ls] = _sel(o, m0).astype(BF16)

    @pl.when(j >= n_cstep)
    def _():
        starts = []
        for u in range(ATT_ROWS):
            i = (j - n_cstep) * ATT_ROWS + u
            r0 = jnp.clip(i - WIN_ROWS // 2, 0, n_rows - WIN_ROWS)
            starts.append(pl.multiple_of(n_ctx + r0 * GRID_W, GRID_W))
        qs = [stacked_q(rs, ls) for _, _, rs, ls in probs]
        s_loc = [_dot_nt(qs[n], k_ref[0, pl.ds(starts[u], win), ls]) + bias_refs[u][0, p]
                 for n, (u, p, _, ls) in enumerate(probs)]
        s_ctx = [_dot_nt(qs[n], k_ref[0, 0:n_ctx, ls]) for n, (_, _, _, ls) in enumerate(probs)]
        mx = [jnp.maximum(rmax(a), rmax(b)) for a, b in zip(s_loc, s_ctx)]
        e_loc = [jnp.exp(a - m) for a, m in zip(s_loc, mx)]
        e_ctx = [jnp.exp(a - m) for a, m in zip(s_ctx, mx)]
        den = [rsum(a) + rsum(b) for a, b in zip(e_loc, e_ctx)]
        for n, (u, _, rs, ls) in enumerate(probs):
            o = _dot(e_loc[n].astype(BF16), v_ref[0, pl.ds(starts[u], win), ls])
            o = (o + _dot(e_ctx[n].astype(BF16), v_ref[0, 0:n_ctx, ls])) / den[n]
            o_ref[0, rs, ls] = _sel(o, m0).astype(BF16)


def _attention(qkv, bias_tab, n_ctx):
    B, T, _ = qkv.shape
    n_rows = (T - n_ctx) // GRID_W
    n_cstep = n_ctx // (GRID_W * ATT_ROWS)
    half = WIN_ROWS // 2
    blk = GRID_W * ATT_ROWS

    def delta(j, u):
        i = jnp.maximum(j - n_cstep, 0) * ATT_ROWS + u
        return jnp.minimum(i, half) + jnp.maximum(i - (n_rows - half), 0)

    bias_specs = [pl.BlockSpec((1, N_PAIRS, 128, WIN_ROWS * GRID_W), functools.partial(
        lambda b, j, u: (delta(j, u), 0, 0, 0), u=u)) for u in range(ATT_ROWS)]
    return pl.pallas_call(
        functools.partial(_attn_kernel, n_ctx=n_ctx, n_rows=n_rows),
        grid=(B, T // blk),
        in_specs=[
            pl.BlockSpec((1, blk, D_HEADS), lambda b, j: (b, j, 0)),
            pl.BlockSpec((1, T, D_HEADS), lambda b, j: (b, 0, 1)),
            pl.BlockSpec((1, T, D_HEADS), lambda b, j: (b, 0, 2)),
        ] + bias_specs,
        out_specs=pl.BlockSpec((1, blk, D_HEADS), lambda b, j: (b, j, 0)),
        out_shape=jax.ShapeDtypeStruct((B, T, D_HEADS), BF16),
        compiler_params=_cparams(("arbitrary", "arbitrary")),
    )(qkv, qkv, qkv, *([bias_tab] * ATT_ROWS))


def _na_bias_table(rpb):
    H = rpb.shape[0]
    c = np.arange(GRID_W)[:, None]
    kc = np.arange(GRID_W)[None, :]
    cs = np.clip(c - WIN_COLS // 2, 0, GRID_W - WIN_COLS)
    valid = (kc >= cs) & (kc < cs + WIN_COLS)
    cidx = np.clip(kc - c + (WIN_COLS - 1), 0, 2 * WIN_COLS - 2)
    hot = jnp.asarray(cidx[None] == np.arange(2 * WIN_COLS - 1)[:, None, None], F32)
    t = jnp.einsum("hro,ock->hcrk", rpb.astype(F32), hot, precision=lax.Precision.HIGHEST)
    t = jnp.where(valid[None, :, None, :], t, NEG)
    tabs = [t[:, :, WIN_ROWS - 1 - dl:2 * WIN_ROWS - 1 - dl, :].reshape(H, GRID_W, WIN_ROWS * GRID_W)
            for dl in range(WIN_ROWS)]
    return jnp.stack(tabs, 0).reshape(WIN_ROWS, H // 2, 2 * GRID_W, WIN_ROWS * GRID_W)


def _layer_norm(h, g, b):
    mu = jnp.mean(h, axis=-1, keepdims=True)
    var = jnp.mean(jnp.square(h - mu), axis=-1, keepdims=True)
    return (h - mu) * lax.rsqrt(var + LN_EPS) * g + b


def _to_token_tiles(ref, val):
    n = val.shape[0]
    for j in range(TOK_SUB):
        ref[pl.ds(j, n, stride=TOK_SUB), :] = val[:, j * 128:(j + 1) * 128]


def _from_token_tiles(ref, n):
    return [ref[pl.ds(j, n, stride=TOK_SUB), :] for j in range(TOK_SUB)]


CLS_ROWS = 64
_EXPERT_ORDER = np.arange(N_EXPERTS).reshape(N_GROUPS, EXPERTS_PER_GROUP).T.reshape(-1)


def _route_tile(logits_t, bias_col, tri_t, carry):
    n = logits_t.shape[1]
    neg = jnp.float32(-jnp.inf)
    scores = jax.nn.sigmoid(logits_t)
    sel = scores + bias_col
    s = [sel[N_GROUPS * k:N_GROUPS * (k + 1)] for k in range(EXPERTS_PER_GROUP)]
    c = [scores[N_GROUPS * k:N_GROUPS * (k + 1)] for k in range(EXPERTS_PER_GROUP)]
    hi01, lo01 = jnp.maximum(s[0], s[1]), jnp.minimum(s[0], s[1])
    hi23, lo23 = jnp.maximum(s[2], s[3]), jnp.minimum(s[2], s[3])
    top2 = jnp.maximum(hi01, hi23) + jnp.maximum(jnp.minimum(hi01, hi23), jnp.maximum(lo01, lo23))
    grp = lax.broadcasted_iota(jnp.int32, (N_GROUPS, n), 0).astype(F32)
    gmax = jnp.max(top2, axis=0, keepdims=True)
    g_idx = jnp.min(jnp.where(top2 == gmax, grp, jnp.float32(N_GROUPS)), axis=0, keepdims=True)
    pick = grp == g_idx
    v = [jnp.sum(jnp.where(pick, a, 0.0), axis=0, keepdims=True) for a in s]
    w = [jnp.sum(jnp.where(pick, a, 0.0), axis=0, keepdims=True) for a in c]

    def first_argmax(vals):
        best, idx = vals[0], jnp.zeros_like(vals[0])
        for k in range(1, len(vals)):
            upd = vals[k] > best
            best = jnp.where(upd, vals[k], best)
            idx = jnp.where(upd, jnp.float32(k), idx)
        return idx

    i1 = first_argmax(v)
    i2 = first_argmax([jnp.where(i1 == k, neg, v[k]) for k in range(EXPERTS_PER_GROUP)])
    lo, hi = jnp.minimum(i1, i2), jnp.maximum(i1, i2)
    g_lo = sum(jnp.where(lo == k, w[k], 0.0) for k in range(EXPERTS_PER_GROUP))
    g_hi = sum(jnp.where(hi == k, w[k], 0.0) for k in range(EXPERTS_PER_GROUP))
    tot = g_lo + g_hi
    pair = lo * 3.0 - lo * (lo - 1.0) * 0.5 + (hi - lo - 1.0)
    cls = g_idx * 6.0 + pair
    ranks = []
    m = tri_t.shape[0]
    crow = lax.broadcasted_iota(jnp.int32, (CLS_ROWS, m), 0).astype(F32)
    for q in range(n // m):
        onehot = crow == cls[:, q * m:(q + 1) * m]
        within = _dot(onehot.astype(BF16), tri_t)
        ranks.append(jnp.sum(jnp.where(onehot, within + carry, 0.0), axis=0, keepdims=True))
        carry = carry + jnp.sum(onehot.astype(F32), axis=1, keepdims=True)
    zero = jnp.zeros((4, n), F32)
    return jnp.concatenate([g_lo / tot, g_hi / tot, cls, jnp.concatenate(ranks, axis=1), zero], axis=0), carry


def _route_kernel(lg_ref, rb_ref, tri_ref, route_ref, cnt_ref, carry_ref):
    @pl.when(pl.program_id(0) == 0)
    def _():
        carry_ref[...] = jnp.zeros_like(carry_ref)

    route, carry = _route_tile(lg_ref[...], rb_ref[...], tri_ref[...], carry_ref[:, 0:1])
    route_ref[...] = route
    carry_ref[...] = jnp.broadcast_to(carry, carry_ref.shape)
    cnt_ref[...] = jnp.broadcast_to(carry, cnt_ref.shape)


def _route(logits_t, router_bias):
    N = logits_t.shape[1]
    tile = next(t for t in (8 * ROW_TILE, 4 * ROW_TILE, 2 * ROW_TILE, ROW_TILE) if N % t == 0)
    r = np.arange(ROW_TILE)
    tri = jnp.asarray(r[:, None] < r[None, :], BF16)
    bias_col = router_bias.astype(F32)[_EXPERT_ORDER].reshape(N_EXPERTS, 1)
    full = lambda a: pl.BlockSpec(a.shape, lambda i: (0, 0))
    return pl.pallas_call(
        _route_kernel,
        grid=(N // tile,),
        in_specs=[pl.BlockSpec((N_EXPERTS, tile), lambda i: (0, i)), full(bias_col), full(tri)],
        out_specs=[pl.BlockSpec((8, tile), lambda i: (0, i)), pl.BlockSpec((CLS_ROWS, 128), lambda i: (0, 0))],
        out_shape=[jax.ShapeDtypeStruct((8, N), F32), jax.ShapeDtypeStruct((CLS_ROWS, 128), F32)],
        scratch_shapes=[pltpu.VMEM((CLS_ROWS, 128), F32)],
        compiler_params=_cparams(("arbitrary",)),
    )(logits_t, bias_col, tri)


def _outproj_kernel(na_ref, yf_ref, yb_ref, bonus_ref, rg_ref, gg_ref, gb_ref, w_ref, zc_ref, zx_ref, mod_ref,
                    g_ref, b_ref, rwh_ref, rwl_ref, z1_ref, hx_ref, lg_ref, *, d, alpha, n_ctx_tiles, t0):
    rw = _rwpost_tile(yf_ref[0], yb_ref[0], bonus_ref[0], rg_ref[0], gg_ref[...], gb_ref[...])
    o = _dot(na_ref[0], w_ref[0:D_HEADS, :]) + _dot(rw, w_ref[D_HEADS:2 * D_HEADS, :])
    mod = mod_ref[0, 0]
    gate = mod[:, 2 * d:3 * d]
    z = _z_tile(zc_ref, zx_ref, n_ctx_tiles, t0)
    z1 = _layer_norm(alpha * z + gate * o, g_ref[...], b_ref[...])
    z1_ref[0] = z1
    hx = z1 * (1.0 + mod[:, 4 * d:5 * d]) + mod[:, 3 * d:4 * d]
    _to_token_tiles(hx_ref, hx)
    hx_hi, hx_lo = _split2(hx)
    wh, wl = rwh_ref[...], rwl_ref[...]
    logits = _dot(hx_hi, wh) + _dot(hx_hi, wl) + _dot(hx_lo, wh)
    lg_ref[...] = logits.T[0:N_EXPERTS]


def _outproj(na, rwkv, gn_g, gn_b, w_out_b, zc, zx, sub, modsel, ln_g, ln_b, router_w, n_ctx_tiles, t0, alpha):
    yf, yb, bonus, rgate = rwkv
    B, T, _ = na.shape
    D = zc.shape[2]
    nt = T // ROW_TILE - t0
    half = pl.BlockSpec((1, ROW_TILE, D_HEADS), lambda b, t: (b, t + t0, 0))
    ytile = pl.BlockSpec((1, D_HEADS, ROW_TILE), lambda b, t: (b, 0, t + t0))
    hrow = pl.BlockSpec((1, D_HEADS), lambda b, t: (0, 0))
    tile = pl.BlockSpec((1, ROW_TILE, D), lambda b, t: (b, t, 0))
    row = pl.BlockSpec((1, D), lambda b, t: (0, 0))
    cspec, xspec = _z_specs(D, n_ctx_tiles, sub, t0)
    wt = jnp.pad(router_w[:, _EXPERT_ORDER], ((0, 0), (0, 128 - N_EXPERTS)))
    wt_hi = wt.astype(BF16)
    wt_lo = (wt - wt_hi.astype(F32)).astype(BF16)
    full = lambda a: pl.BlockSpec(a.shape, lambda b, t: (0, 0))
    return pl.pallas_call(
        functools.partial(_outproj_kernel, d=D, alpha=alpha, n_ctx_tiles=n_ctx_tiles, t0=t0),
        grid=(B, nt),
        in_specs=[
            half, ytile, ytile, half, half, hrow, hrow,
            pl.BlockSpec((2 * D_HEADS, D), lambda b, t: (0, 0)),
            cspec, xspec,
            _mod_spec(D, n_ctx_tiles, t0),
            row, row,
            full(wt_hi), full(wt_lo),
        ],
        out_specs=[
            tile,
            pl.BlockSpec((ROW_TILE * TOK_SUB, 128), lambda b, t: (b * nt + t, 0)),
            pl.BlockSpec((N_EXPERTS, ROW_TILE), lambda b, t: (0, b * nt + t)),
        ],
        out_shape=[
            jax.ShapeDtypeStruct((B, nt * ROW_TILE, D), F32),
            jax.ShapeDtypeStruct((B * nt * ROW_TILE * TOK_SUB, 128), F32),
            jax.ShapeDtypeStruct((N_EXPERTS, B * nt * ROW_TILE), F32),
        ],
        compiler_params=_cparams(("arbitrary", "arbitrary")),
    )(na, yf, yb, bonus, rgate, gn_g, gn_b, w_out_b, zc, zx, modsel, ln_g, ln_b, wt_hi, wt_lo)


def _tok(ref, i):
    return ref.at[pl.ds(pl.multiple_of(i * TOK_SUB, TOK_SUB), TOK_SUB)]


def _dispatch_kernel(dest_ref, hx_ref, xs_in, xs_out, sem):
    del xs_in
    n = dest_ref.shape[0]

    def start(i, c):
        for par in range(2):
            r = 2 * i + par
            pltpu.make_async_copy(_tok(hx_ref, r), _tok(xs_out, dest_ref[r]), sem).start(priority=par)
        return c

    lax.fori_loop(0, n // 2, start, 0, unroll=4)

    def wait(r, c):
        pltpu.make_async_copy(_tok(hx_ref, 0), _tok(xs_out, 0), sem).wait()
        return c

    lax.fori_loop(0, n, wait, 0, unroll=8)


def _dispatch(dest, hx_tiles, n_rows_pad):
    N = dest.shape[0]
    tile = next(t for t in (4 * ROW_TILE, 2 * ROW_TILE, ROW_TILE) if N % t == 0)
    xs0 = jnp.zeros((n_rows_pad * TOK_SUB, 128), F32)
    return pl.pallas_call(
        _dispatch_kernel,
        grid=(N // tile,),
        in_specs=[
            pl.BlockSpec((tile,), lambda i: (i,), memory_space=pltpu.SMEM),
            pl.BlockSpec((tile * TOK_SUB, 128), lambda i: (i, 0)),
            pl.BlockSpec(memory_space=pl.ANY),
        ],
        out_specs=pl.BlockSpec(memory_space=pl.ANY),
        out_shape=jax.ShapeDtypeStruct((n_rows_pad * TOK_SUB, 128), F32),
        scratch_shapes=[pltpu.SemaphoreType.DMA(())],
        input_output_aliases={2: 0},
        compiler_params=_cparams(("arbitrary",)),
    )(dest, hx_tiles, xs0)


def _pack_bf16_pair(a, b):
    ha = lax.bitcast_convert_type(a.astype(BF16).astype(F32), jnp.uint32)
    hb = lax.bitcast_convert_type(b.astype(BF16).astype(F32), jnp.uint32)
    return jnp.bitwise_or(ha, jnp.right_shift(hb, jnp.uint32(16)))


def _unpack_bf16_pair(w):
    a = lax.bitcast_convert_type(jnp.bitwise_and(w, jnp.uint32(0xFFFF0000)), F32)
    b = lax.bitcast_convert_type(jnp.left_shift(w, jnp.uint32(16)), F32)
    return a, b


def _expert_kernel(sc_ref, xs_ref, w1_hbm, w3_hbm, w2_hbm, ys_ref, s13, s2, c13, c2, sem, *, layer):
    i = pl.program_id(0)
    slot = sc_ref[2, i]

    def copies(sl, ea, eb):
        out = []
        for j, e in enumerate((ea, eb)):
            out.append(pltpu.make_async_copy(w1_hbm.at[layer, e], s13.at[sl, 2 * j], sem.at[sl, 3 * j]))
            out.append(pltpu.make_async_copy(w3_hbm.at[layer, e], s13.at[sl, 2 * j + 1], sem.at[sl, 3 * j + 1]))
            out.append(pltpu.make_async_copy(w2_hbm.at[layer, e], s2.at[sl, j], sem.at[sl, 3 * j + 2]))
        return out

    @pl.when(i == 0)
    def _():
        for cp in copies(0, sc_ref[3, i], sc_ref[4, i]):
            cp.start()

    for sl in range(2):
        @pl.when(jnp.logical_and(sc_ref[1, i] != 0, slot == sl))
        def _():
            for cp in copies(sl, sc_ref[3, i], sc_ref[4, i]):
                cp.wait()
            def cast(r, c):
                rows = pl.ds(pl.multiple_of(r * 64, 64), 64)
                for j in range(4):
                    c13[j, rows, :] = s13[sl, j, rows, :].astype(BF16)
                return c

            lax.fori_loop(0, s13.shape[2] // 64, cast, 0)

            def cast2(r, c):
                rows = pl.ds(pl.multiple_of(r * 64, 64), 64)
                for j in range(2):
                    c2[j, rows, :] = s2[sl, j, rows, :].astype(BF16)
                return c

            lax.fori_loop(0, s2.shape[2] // 64, cast2, 0)

            @pl.when(sc_ref[5, i] != 0)
            def _():
                for cp in copies(1 - sl, sc_ref[6, i], sc_ref[7, i]):
                    cp.start()

    @pl.when(sc_ref[0, i] != 0)
    def _():
        x = jnp.concatenate(_from_token_tiles(xs_ref, MOE_TILE), axis=1).astype(BF16)

        def ffn(j):
            h1 = _dot(x, c13[2 * j])
            h3 = _dot(x, c13[2 * j + 1])
            h = (h1 * jax.nn.sigmoid(h1)) * h3
            return _dot(h.astype(BF16), c2[j])

        _to_token_tiles(ys_ref, _pack_bf16_pair(ffn(0), ffn(1)))

    @pl.when(sc_ref[0, i] == 0)
    def _():
        ys_ref[...] = jnp.zeros_like(ys_ref)


def _experts(blk_sched, xs, w1, w3, w2, l):
    nb = xs.shape[0] // (MOE_TILE * TOK_SUB)
    _, _, D, DE = w1.shape
    tok = pl.BlockSpec((MOE_TILE * TOK_SUB, 128), lambda i, sc: (i, 0))
    hbm = pl.BlockSpec(memory_space=pl.ANY)
    grid_spec = pltpu.PrefetchScalarGridSpec(
        num_scalar_prefetch=1,
        grid=(nb,),
        in_specs=[tok, hbm, hbm, hbm],
        out_specs=tok,
        scratch_shapes=[
            pltpu.VMEM((2, 4, D, DE), F32),
            pltpu.VMEM((2, 2, DE, D), F32),
            pltpu.VMEM((4, D, DE), BF16),
            pltpu.VMEM((2, DE, D), BF16),
            pltpu.SemaphoreType.DMA((2, 6)),
        ],
    )
    return pl.pallas_call(
        functools.partial(_expert_kernel, layer=l),
        grid_spec=grid_spec,
        out_shape=jax.ShapeDtypeStruct(xs.shape, jnp.uint32),
        compiler_params=_cparams(("arbitrary",)),
    )(blk_sched, xs, w1, w3, w2)


def _combine_kernel(dest_ref, dnext_ref, ys_ref, gate_ref, z1_ref, mod_ref, g_ref, b_ref, o_ref, buf, sem,
                    *, d, alpha):
    n = dest_ref.shape[0]
    step = pl.program_id(0) * pl.num_programs(1) + pl.program_id(1)
    n_steps = pl.num_programs(0) * pl.num_programs(1)
    slot = step % 2

    def gather(idx_ref, sl):
        def start(i, c):
            for par in range(2):
                r = 2 * i + par
                pltpu.make_async_copy(_tok(ys_ref, idx_ref[r]), _tok(buf.at[sl], r), sem.at[sl]).start(priority=par)
            return c

        lax.fori_loop(0, n // 2, start, 0, unroll=4)

    @pl.when(step == 0)
    def _():
        gather(dest_ref, 0)

    @pl.when(step + 1 < n_steps)
    def _():
        gather(dnext_ref, 1 - slot)

    def wait(r, c):
        pltpu.make_async_copy(_tok(ys_ref, 0), _tok(buf.at[slot], 0), sem.at[slot]).wait()
        return c

    lax.fori_loop(0, n, wait, 0, unroll=8)
    gates = gate_ref[...].T
    ga = gates[:, 0:1]
    gb = gates[:, 1:2]
    parts = []
    for w in _from_token_tiles(buf.at[slot], n):
        fa, fb = _unpack_bf16_pair(w)
        parts.append(ga * fa + gb * fb)
    y = jnp.concatenate(parts, axis=1)
    gate = mod_ref[0, 0][:, 5 * d:6 * d]
    o_ref[0] = _layer_norm(alpha * z1_ref[0] + gate * y, g_ref[...], b_ref[...])


def _combine(dest, ys, gates, z1, modsel, ln_g, ln_b, n_ctx_tiles, t0, t_out, alpha):
    B, T1, D = z1.shape
    nt1 = T1 // ROW_TILE
    skip = t_out - t0
    nt = nt1 - skip
    tile = lambda off: pl.BlockSpec((1, ROW_TILE, D), lambda b, t: (b, t + off, 0))
    row = pl.BlockSpec((1, D), lambda b, t: (0, 0))

    def nxt(b, t):
        last = jnp.logical_and(b == B - 1, t == nt - 1)
        wrap = t == nt - 1
        b2 = jnp.where(jnp.logical_and(wrap, jnp.logical_not(last)), b + 1, b)
        t2 = jnp.where(last, t, jnp.where(wrap, 0, t + 1))
        return b2 * nt1 + t2 + skip

    return pl.pallas_call(
        functools.partial(_combine_kernel, d=D, alpha=alpha),
        grid=(B, nt),
        in_specs=[
            pl.BlockSpec((ROW_TILE,), lambda b, t: (b * nt1 + t + skip,), memory_space=pltpu.SMEM),
            pl.BlockSpec((ROW_TILE,), lambda b, t: (nxt(b, t),), memory_space=pltpu.SMEM),
            pl.BlockSpec(memory_space=pl.ANY),
            pl.BlockSpec((8, ROW_TILE), lambda b, t: (0, b * nt1 + t + skip)),
            tile(skip),
            _mod_spec(D, n_ctx_tiles, t_out),
            row, row,
        ],
        out_specs=tile(0),
        out_shape=jax.ShapeDtypeStruct((B, nt * ROW_TILE, D), F32),
        scratch_shapes=[pltpu.VMEM((2, ROW_TILE * TOK_SUB, 128), jnp.uint32), pltpu.SemaphoreType.DMA((2,))],
        compiler_params=_cparams(("arbitrary", "arbitrary")),
    )(dest, dest, ys, gates, z1, modsel, ln_g, ln_b)


_PAIR_LO = np.array([0, 0, 0, 1, 1, 2], np.int32)
_PAIR_HI = np.array([1, 2, 3, 2, 3, 3], np.int32)


def _schedule(route, cnt):
    cls = route[2].astype(jnp.int32)
    rank = route[3].astype(jnp.int32)
    N = cls.shape[0]
    counts = cnt[:N_CLASSES, 0].astype(jnp.int32)
    onehot = (cls[:, None] == jnp.arange(N_CLASSES, dtype=jnp.int32)[None, :])
    padded = (counts + MOE_TILE - 1) // MOE_TILE * MOE_TILE
    cls_end = jnp.cumsum(padded)
    cls_start = cls_end - padded
    dest = jnp.sum(jnp.where(onehot, cls_start[None, :], 0), axis=1) + rank

    nb = N // MOE_TILE + N_CLASSES
    blk_row = jnp.arange(nb, dtype=jnp.int32) * MOE_TILE
    total = jnp.sum(padded)
    blk_valid = (blk_row < total).astype(jnp.int32)
    row_c = jnp.minimum(blk_row, jnp.maximum(total - MOE_TILE, 0))
    blk_cls = jnp.sum((row_c[:, None] >= cls_end[None, :]).astype(jnp.int32), axis=1)
    blk_cls = jnp.minimum(blk_cls, N_CLASSES - 1)

    def experts_of(c):
        hot = (c % 6)[:, None] == jnp.arange(6, dtype=jnp.int32)[None, :]
        base = (c // 6) * EXPERTS_PER_GROUP
        return (base + jnp.sum(jnp.where(hot, jnp.asarray(_PAIR_LO)[None, :], 0), axis=1),
                base + jnp.sum(jnp.where(hot, jnp.asarray(_PAIR_HI)[None, :], 0), axis=1))

    prev_cls = jnp.concatenate([jnp.full((1,), -1, jnp.int32), blk_cls[:-1]])
    first = jnp.logical_and(blk_valid != 0, blk_cls != prev_cls).astype(jnp.int32)
    parity = (jnp.cumsum(first) - 1) % 2
    cls_ids = jnp.arange(N_CLASSES, dtype=jnp.int32)
    used = jnp.where(counts > 0, cls_ids, N_CLASSES)
    later = jnp.where(cls_ids[None, :] > cls_ids[:, None], used[None, :], N_CLASSES)
    nxt_of_cls = jnp.min(later, axis=1)
    cls_hot = blk_cls[:, None] == cls_ids[None, :]
    nxt_cls = jnp.sum(jnp.where(cls_hot, nxt_of_cls[None, :], 0), axis=1)
    has_next = (nxt_cls < N_CLASSES).astype(jnp.int32)
    ea, eb = experts_of(blk_cls)
    na, nb_ = experts_of(jnp.minimum(nxt_cls, N_CLASSES - 1))
    blk_sched = jnp.stack([blk_valid, first, parity, ea, eb, has_next, na, nb_]).astype(jnp.int32)
    return dest.astype(jnp.int32), blk_sched, nb * MOE_TILE


def _rope_tables(n_ctx, seq):
    t = np.arange(seq)
    row = (t // GRID_W).astype(np.float32)
    col = (t % GRID_W).astype(np.float32)
    n_freq = HEAD_DIM // 4
    inv = jnp.asarray(ROPE_BASE, F32) ** (-jnp.arange(n_freq, dtype=F32) / n_freq)
    ar = jnp.asarray(row)[:, None] * inv
    ac = jnp.asarray(col)[:, None] * inv
    ang = jnp.concatenate([ar, ar, ac, ac], -1)
    cos = jnp.cos(ang)
    sin = jnp.sin(ang)
    quarter = (np.arange(HEAD_DIM) // n_freq) % 2
    sa = jnp.where(quarter == 0, -sin, 0.0)
    sb = jnp.where(quarter == 1, sin, 0.0)
    pad = lambda a, fill: jnp.concatenate([jnp.full((n_ctx, HEAD_DIM), fill, F32), a], 0)
    tile = lambda a: jnp.tile(a, (1, N_HEADS))
    return tile(pad(cos, 1.0)), tile(pad(sa, 0.0)), tile(pad(sb, 0.0))


def _rw_consts(l, n_ctx, seq, rw_mu_prev, rw_mu_next, rw_w0, rw_w2, rw_a0, rw_a2, rw_g2, rw_k_k, rw_k_a, rw_r_k):
    d_rw_in = rw_mu_prev.shape[1]
    padw = lambda a: jnp.pad(a[l], (0, D_RWP - d_rw_in)).reshape(1, D_RWP)
    cos, sa, sb = _rope_tables(n_ctx, seq)
    w2p = jnp.zeros((128, 2 * D_HEADS), F32)
    w2p = w2p.at[0:LORA, 0:D_HEADS].set(rw_w2[l, 0]).at[LORA:2 * LORA, D_HEADS:].set(rw_w2[l, 1])
    a2p = jnp.zeros((128, 2 * D_HEADS), F32)
    a2p = a2p.at[2 * LORA:3 * LORA, 0:D_HEADS].set(rw_a2[l, 0]).at[3 * LORA:4 * LORA, D_HEADS:].set(rw_a2[l, 1])
    g2p = jnp.zeros((128, D_HEADS), F32).at[0:GATE_LORA].set(rw_g2[l])
    head = np.arange(D_HEADS) // HEAD_DIM
    ones_bd = jnp.asarray(head[:, None] == head[None, :], BF16)
    return dict(
        mu_prev=padw(rw_mu_prev), mu_next=padw(rw_mu_next), cos=cos, sa=sa, sb=sb,
        k_k=rw_k_k[l].reshape(1, D_HEADS), k_a=rw_k_a[l].reshape(1, D_HEADS), r_k=rw_r_k[l].reshape(1, D_HEADS),
        w0=rw_w0[l].reshape(1, 2 * D_HEADS), a0=rw_a0[l].reshape(1, 2 * D_HEADS),
        w2p=w2p.astype(BF16), a2p=a2p.astype(BF16), g2p=g2p.astype(BF16), ones_bd=ones_bd,
    )


def kernel(x, c, ctx, c_ctx, ada_w, ada_b, w_in, na_rpb, rw_mu_prev, rw_mu_next, rw_w0, rw_w2, rw_a0, rw_a2, rw_g2, rw_k_k, rw_k_a, rw_r_k, rw_gn_g, rw_gn_b, w_out, ln1_g, ln1_b, ln2_g, ln2_b, router_w, router_bias, exp_w1, exp_w3, exp_w2):
    B, S, D = x.shape
    C = ctx.shape[1]
    L = ada_w.shape[0]
    T = C + S
    assert D == 1024 and C % ROW_TILE == 0 and S % ROW_TILE == 0 and C % (2 * CHUNK) == 0
    assert S % GRID_W == 0 and S // GRID_W >= WIN_ROWS and w_in.shape[2] == D_INP - 32
    n_ctx_tiles = C // ROW_TILE
    alpha = float((2 * L) ** 0.25)

    zc, zx, sub = ctx, x, n_ctx_tiles
    n_mod = (B + 1 + 7) // 8 * 8
    cc = jnp.zeros((n_mod, D), F32).at[0:B].set(c).at[B].set(c_ctx)
    mod_all = _ada(cc, ada_w, ada_b)

    for l in range(L):
        mod_c = jnp.broadcast_to(mod_all[l, B][None], (B, 6 * D))
        modsel = jnp.stack([mod_c, mod_all[l, 0:B]], axis=1).reshape(B, 2, 1, 6 * D)
        w_in_p = jnp.pad(w_in[l], ((0, 0), (0, D_INP - w_in.shape[2]))).astype(BF16)
        qkv, p_rw = _inproj(zc, zx, sub, T, modsel, w_in_p, n_ctx_tiles)

        consts = _rw_consts(l, C, S, rw_mu_prev, rw_mu_next, rw_w0, rw_w2, rw_a0, rw_a2, rw_g2,
                            rw_k_k, rw_k_a, rw_r_k)
        prep = _rwprep(p_rw, consts, n_ctx_tiles)
        yf, yb = _wkv(prep[0:15], C)
        na = _attention(qkv, _na_bias_table(na_rpb[l]), C)

        t0 = n_ctx_tiles if l == L - 1 else 0
        z1, hx_tiles, logits_t = _outproj(na, (yf, yb, prep[15], prep[16]), rw_gn_g[l].reshape(1, D_HEADS),
                                          rw_gn_b[l].reshape(1, D_HEADS), w_out[l].astype(BF16), zc, zx, sub,
                                          modsel, ln1_g[l].reshape(1, D), ln1_b[l].reshape(1, D), router_w,
                                          n_ctx_tiles, t0, alpha)
        route, cnt = _route(logits_t, router_bias)
        dest, blk_sched, n_rows_pad = _schedule(route, cnt)
        xs = _dispatch(dest, hx_tiles, n_rows_pad)
        ys = _experts(blk_sched, xs, exp_w1, exp_w3, exp_w2, l)
        z = _combine(dest, ys, route, z1, modsel, ln2_g[l].reshape(1, D), ln2_b[l].reshape(1, D),
                     n_ctx_tiles, t0, t0, alpha)
        zc, zx, sub = z, z, 0

    return z
```

```python
import functools
import math

import jax
import jax.numpy as jnp
import numpy as np
from jax import lax
from jax.experimental import pallas as pl
from jax.experimental.pallas import tpu as pltpu

F32 = jnp.float32
BF16 = jnp.bfloat16

HEAD_DIM = 64
N_HEADS = 8
D_HEADS = N_HEADS * HEAD_DIM
N_PAIRS = N_HEADS // 2
GRID_W = 64
WIN_ROWS = 8
WIN_COLS = 16
LORA = 32
GATE_LORA = 96
N_EXPERTS = 32
N_GROUPS = 8
EXPERTS_PER_GROUP = 4
N_CLASSES = N_GROUPS * 6
ROPE_BASE = 10000.0
LN_EPS = 1e-6
GN_EPS = 64e-5
CHUNK = 64
ROW_TILE = 256
MOE_TILE = 256
D_QKV = 3 * D_HEADS
D_RWP = 3 * D_HEADS + 256
D_INP = D_QKV + D_RWP
TOK_SUB = 8
NEG = -1e30
VMEM_LIMIT = 56 * 1024 * 1024


def _cparams(sem):
    return pltpu.CompilerParams(dimension_semantics=sem, vmem_limit_bytes=VMEM_LIMIT)


def _dot(a, b):
    return jnp.dot(a, b, preferred_element_type=F32)


def _dot_nt(a, b):
    return lax.dot_general(a, b, (((1,), (1,)), ((), ())), preferred_element_type=F32)


def _split2(a):
    hi = a.astype(BF16)
    lo = (a - hi.astype(F32)).astype(BF16)
    return hi, lo


def _dot3(a, b):
    ah, al = _split2(a)
    bh, bl = _split2(b)
    return _dot(ah, bh) + _dot(al, bh) + _dot(ah, bl)


def _ada_kernel(cc_ref, w_ref, b_ref, o_ref):
    cc = cc_ref[...]
    s = cc * jax.nn.sigmoid(cc)
    o_ref[0] = _dot3(s, w_ref[0]) + b_ref[0]


def _ada(cc, ada_w, ada_b):
    L, D, D6 = ada_w.shape
    R = cc.shape[0]
    tn = 1536
    return pl.pallas_call(
        _ada_kernel,
        grid=(L, D6 // tn),
        in_specs=[
            pl.BlockSpec((R, D), lambda l, n: (0, 0)),
            pl.BlockSpec((1, D, tn), lambda l, n: (l, 0, n)),
            pl.BlockSpec((1, 1, tn), lambda l, n: (l, 0, n)),
        ],
        out_specs=pl.BlockSpec((1, R, tn), lambda l, n: (l, 0, n)),
        out_shape=jax.ShapeDtypeStruct((L, R, D6), F32),
        compiler_params=_cparams(("arbitrary", "arbitrary")),
    )(cc, ada_w, ada_b.reshape(L, 1, D6))


def _z_specs(D, n_ctx_tiles, sub, t0=0):
    cspec = pl.BlockSpec((1, ROW_TILE, D), lambda b, t: (b, jnp.minimum(t + t0, n_ctx_tiles - 1), 0))
    xspec = pl.BlockSpec((1, ROW_TILE, D), lambda b, t: (b, jnp.maximum(t + t0, n_ctx_tiles) - sub, 0))
    return cspec, xspec


def _z_tile(zc_ref, zx_ref, n_ctx_tiles, t0=0):
    return jnp.where(pl.program_id(1) + t0 < n_ctx_tiles, zc_ref[0], zx_ref[0])


def _mod_spec(D, n_ctx_tiles, t0=0):
    return pl.BlockSpec((1, 1, 1, 6 * D), lambda b, t: (b, jnp.where(t + t0 < n_ctx_tiles, 0, 1), 0, 0))


def _inproj_kernel(zc_ref, zx_ref, mod_ref, w_ref, qkv_ref, rw_ref, *, d, n_ctx_tiles):
    z = _z_tile(zc_ref, zx_ref, n_ctx_tiles)
    mod = mod_ref[0, 0]
    shift = mod[:, 0:d]
    scale = mod[:, d:2 * d]
    h = (z * (1.0 + scale) + shift).astype(BF16)
    q = _dot(h, w_ref[:, 0:D_HEADS])
    qkv_ref[0, :, 0:D_HEADS] = (q * (HEAD_DIM ** -0.5)).astype(BF16)
    kv = _dot(h, w_ref[:, D_HEADS:D_QKV])
    qkv_ref[0, :, D_HEADS:D_QKV] = kv.astype(BF16)
    rw_ref[0] = _dot(h, w_ref[:, D_QKV:D_INP])


def _inproj(zc, zx, sub, T, modsel, w_in_p, n_ctx_tiles):
    B, _, D = zc.shape
    nt = T // ROW_TILE
    cspec, xspec = _z_specs(D, n_ctx_tiles, sub)
    return pl.pallas_call(
        functools.partial(_inproj_kernel, d=D, n_ctx_tiles=n_ctx_tiles),
        grid=(B, nt),
        in_specs=[
            cspec, xspec,
            pl.BlockSpec((1, 1, 1, 6 * D), lambda b, t: (b, jnp.where(t < n_ctx_tiles, 0, 1), 0, 0)),
            pl.BlockSpec((D, D_INP), lambda b, t: (0, 0)),
        ],
        out_specs=[
            pl.BlockSpec((1, ROW_TILE, D_QKV), lambda b, t: (b, t, 0)),
            pl.BlockSpec((1, ROW_TILE, D_RWP), lambda b, t: (b, t, 0)),
        ],
        out_shape=[
            jax.ShapeDtypeStruct((B, T, D_QKV), BF16),
            jax.ShapeDtypeStruct((B, T, D_RWP), F32),
        ],
        compiler_params=_cparams(("arbitrary", "arbitrary")),
    )(zc, zx, modsel, w_in_p)


def _rwprep_kernel(p_ref, pp_ref, pn_ref, mup_ref, mun_ref, cos_ref, sa_ref, sb_ref,
                   kk_ref, ka_ref, rk_ref, w0_ref, a0_ref, w2_ref, a2_ref, g2_ref, ones_ref,
                   at_f, bt_f, kt_f, rt_f, bh_f, kh_f, pe_f,
                   at_b, bt_b, kt_b, rt_b, bh_b, kh_b, pe_b,
                   vt_ref, bonus_ref, g_ref, sh_ref, *, n_ctx_tiles, n_tiles):
    t = pl.program_id(1)
    P = p_ref[0]
    R = P.shape[0]
    prev_ok = jnp.logical_and(t != 0, t != n_ctx_tiles)
    next_ok = jnp.logical_and(t != n_ctx_tiles - 1, t != n_tiles - 1)
    sh_ref[8:R + 8, :] = P
    sh_ref[0:8, :] = jnp.where(prev_ok, pp_ref[0], 0.0)
    sh_ref[R + 8:R + 16, :] = jnp.where(next_ok, pn_ref[0], 0.0)
    prev = sh_ref[7:R + 7, :]
    nxt = sh_ref[9:R + 9, :]
    mup = mup_ref[...]
    mun = mun_ref[...]
    z = (1.0 - mup - mun) * P + mup * prev + mun * nxt

    cos = cos_ref[...]
    sa = sa_ref[...]
    sb = sb_ref[...]

    def rope(u):
        q = HEAD_DIM // 4
        parts = []
        for m in range(D_HEADS // 128):
            ls = slice(m * 128, (m + 1) * 128)
            um = u[:, ls]
            parts.append(um * cos[:, ls] + pltpu.roll(um, 128 - q, axis=1) * sa[:, ls]
                         + pltpu.roll(um, q, axis=1) * sb[:, ls])
        return jnp.concatenate(parts, axis=1)

    r = rope(z[:, 0:D_HEADS])
    k = rope(z[:, D_HEADS:2 * D_HEADS])
    v = z[:, 2 * D_HEADS:3 * D_HEADS]
    ones_bd = ones_ref[...]

    kk = k * kk_ref[...]
    sq_hi, sq_lo = _split2(kk * kk)
    kk = kk * lax.rsqrt(jnp.maximum(_dot(sq_hi, ones_bd) + _dot(sq_lo, ones_bd), 1e-24))

    sigmoid = lambda u: 0.5 * jnp.tanh(0.5 * u) + 0.5
    slab = z[:, 3 * D_HEADS:3 * D_HEADS + 128]
    u_w = w0_ref[...] + _dot(jnp.tanh(slab).astype(BF16), w2_ref[...])
    u_a = a0_ref[...] + _dot(slab.astype(BF16), a2_ref[...])
    g_ref[0] = _dot(sigmoid(z[:, 3 * D_HEADS + 128:D_RWP]).astype(BF16), g2_ref[...]).astype(BF16)
    e_all = math.exp(-0.5) * sigmoid(u_w)
    a_all = sigmoid(u_a)

    ci = lax.broadcasted_iota(jnp.int32, (CHUNK, CHUNK), 0)
    cj = lax.broadcasted_iota(jnp.int32, (CHUNK, CHUNK), 1)
    ka = ka_ref[...]
    outs = ((at_f, bt_f, kt_f, rt_f, bh_f, kh_f, pe_f), (at_b, bt_b, kt_b, rt_b, bh_b, kh_b, pe_b))
    kd_sum = None
    for d in range(2):
        e = e_all[:, d * D_HEADS:(d + 1) * D_HEADS]
        a = a_all[:, d * D_HEADS:(d + 1) * D_HEADS]
        tri = (cj <= ci) if d == 0 else (cj >= ci)
        tri = tri.astype(BF16)
        cs, ce = [], []
        for q in range(R // CHUNK):
            e_hi, e_lo = _split2(e[q * CHUNK:(q + 1) * CHUNK])
            cq = -(_dot(tri, e_hi) + _dot(tri, e_lo))
            end = cq[CHUNK - 1:CHUNK] if d == 0 else cq[0:1]
            cs.append(cq)
            ce.append(jnp.broadcast_to(end, cq.shape))
        c = jnp.concatenate(cs, axis=0)
        cend = jnp.concatenate(ce, axis=0)
        kd = k * (1.0 + (a - 1.0) * ka)
        kd_sum = kd if kd_sum is None else kd_sum + kd
        beta = a * kk
        en = jnp.exp(-c)
        eh = jnp.exp(cend - c)
        o_at, o_bt, o_kt, o_rt, o_bh, o_kh, o_pe = outs[d]
        o_at[0] = (-kk * jnp.exp(c + e)).astype(BF16)
        o_bt[0] = (beta * en).astype(BF16)
        o_kt[0] = (kd * en).astype(BF16)
        o_rt[0] = (r * jnp.exp(c)).astype(BF16)
        o_bh[0] = (beta * eh).astype(BF16)
        o_kh[0] = (kd * eh).astype(BF16)
        for q in range(R // CHUNK):
            o_pe[0, q] = jnp.exp(cend[q * CHUNK:q * CHUNK + 8])
    bonus_ref[0] = (_dot((r * rk_ref[...] * kd_sum).astype(BF16), ones_bd) * v).astype(BF16)
    vt_ref[0] = v.T.astype(BF16)


def _rwprep(p_rw, consts, n_ctx_tiles):
    B, T, _ = p_rw.shape
    nt = T // ROW_TILE
    nh = ROW_TILE // 8
    row = lambda w: pl.BlockSpec((1, w), lambda b, t: (0, 0))
    full = lambda a: pl.BlockSpec(a.shape, lambda b, t: (0, 0))
    tm = pl.BlockSpec((1, ROW_TILE, D_HEADS), lambda b, t: (b, t, 0))
    tab = pl.BlockSpec((ROW_TILE, D_HEADS), lambda b, t: (t, 0))
    pe = pl.BlockSpec((1, ROW_TILE // CHUNK, 8, D_HEADS), lambda b, t: (b, t, 0, 0))
    tm_shape = jax.ShapeDtypeStruct((B, T, D_HEADS), BF16)
    pe_shape = jax.ShapeDtypeStruct((B, T // CHUNK, 8, D_HEADS), F32)
    dir_specs = [tm] * 6 + [pe]
    dir_shapes = [tm_shape] * 6 + [pe_shape]
    return pl.pallas_call(
        functools.partial(_rwprep_kernel, n_ctx_tiles=n_ctx_tiles, n_tiles=nt),
        grid=(B, nt),
        in_specs=[
            pl.BlockSpec((1, ROW_TILE, D_RWP), lambda b, t: (b, t, 0)),
            pl.BlockSpec((1, 8, D_RWP), lambda b, t: (b, jnp.maximum(t * nh - 1, 0), 0)),
            pl.BlockSpec((1, 8, D_RWP), lambda b, t: (b, jnp.minimum((t + 1) * nh, T // 8 - 1), 0)),
            row(D_RWP), row(D_RWP), tab, tab, tab,
            row(D_HEADS), row(D_HEADS), row(D_HEADS), row(2 * D_HEADS), row(2 * D_HEADS),
            full(consts["w2p"]), full(consts["a2p"]), full(consts["g2p"]), full(consts["ones_bd"]),
        ],
        out_specs=dir_specs + dir_specs + [
            pl.BlockSpec((1, D_HEADS, ROW_TILE), lambda b, t: (b, 0, t)),
            tm, tm,
        ],
        out_shape=dir_shapes + dir_shapes + [
            jax.ShapeDtypeStruct((B, D_HEADS, T), BF16),
            jax.ShapeDtypeStruct((B, T, D_HEADS), BF16),
            jax.ShapeDtypeStruct((B, T, D_HEADS), BF16),
        ],
        scratch_shapes=[pltpu.VMEM((ROW_TILE + 16, D_RWP), F32)],
        compiler_params=_cparams(("arbitrary", "arbitrary")),
    )(p_rw, p_rw, p_rw, consts["mu_prev"], consts["mu_next"], consts["cos"], consts["sa"], consts["sb"],
      consts["k_k"], consts["k_a"], consts["r_k"], consts["w0"], consts["a0"],
      consts["w2p"], consts["a2p"], consts["g2p"], consts["ones_bd"])


def _bd(y, m0):
    zero = jnp.zeros_like(y)
    return jnp.concatenate([jnp.where(m0, y, zero), jnp.where(m0, zero, y)], axis=0)


def _sel(w, m0):
    return jnp.where(m0, w[0:CHUNK], w[CHUNK:2 * CHUNK])


def _wkv_kernel(*refs):
    (at_f, bt_f, kt_f, rt_f, bh_f, kh_f, pe_f, vt_f,
     at_b, bt_b, kt_b, rt_b, bh_b, kh_b, pe_b, vt_b,
     yf_ref, yb_ref, s_ref) = refs
    s = pl.program_id(1)

    @pl.when(s == 0)
    def _():
        s_ref[...] = jnp.zeros_like(s_ref)

    lane = lax.broadcasted_iota(jnp.int32, (CHUNK, 2 * CHUNK), 1)
    rowi = lax.broadcasted_iota(jnp.int32, (CHUNK, 2 * CHUNK), 0)
    lm = jnp.bitwise_and(lane, CHUNK - 1)
    m0 = lane < CHUNK
    dirs = ((at_f, bt_f, kt_f, rt_f, bh_f, kh_f, pe_f, vt_f, yf_ref),
            (at_b, bt_b, kt_b, rt_b, bh_b, kh_b, pe_b, vt_b, yb_ref))
    masks = (((rowi < lm), (rowi <= lm)), ((rowi > lm), (rowi >= lm)))
    zero = jnp.zeros((CHUNK, 2 * CHUNK), F32)
    bd = lambda y: _bd(y, m0)
    bf = lambda y: y.astype(BF16)

    lane2 = lax.broadcasted_iota(jnp.int32, (2 * CHUNK, 2 * CHUNK), 1)
    cat2 = lambda y: jnp.concatenate([y, y], axis=0)
    probs = []
    for rnd in range(WKV_CPS):
        for d in range(2):
            for p in range(N_PAIRS):
                ck = rnd if d == 0 else WKV_CPS - 1 - rnd
                probs.append(dict(d=d, p=p, ck=ck, rnd=rnd,
                                  rs=slice(ck * CHUNK, (ck + 1) * CHUNK), ls=slice(p * 128, (p + 1) * 128)))

    def ld(pr, i):
        return dirs[pr["d"]][i][0, pr["rs"], pr["ls"]]

    def rhs1(pr):
        return jnp.concatenate([bd(ld(pr, 0)), bd(ld(pr, 3))], axis=0)

    for pr in probs:
        G = _dot_nt(jnp.concatenate([ld(pr, 1), ld(pr, 2)], axis=0), rhs1(pr))
        strict, incl = masks[pr["d"]]
        pr["N"] = jnp.where(strict, G[0:CHUNK, 0:128], zero)
        pr["N_br"] = bf(jnp.where(incl, G[0:CHUNK, 128:256], zero))
        pr["A_ak"] = bf(jnp.where(strict, G[CHUNK:128, 0:128], zero))
        pr["N_kr"] = bf(jnp.where(incl, G[CHUNK:128, 128:256], zero))
    for pr in probs:
        Ab = bf(pr["N"])
        pr["M"] = _dot(Ab, bd(Ab))
    for _ in range(4):
        for pr in probs:
            Mb = bf(pr["M"])
            Rm = _dot(jnp.concatenate([bf(pr["N"]), Mb], axis=0), bd(Mb))
            pr["N"] = pr["N"] + pr["M"] + Rm[0:CHUNK]
            pr["M"] = Rm[CHUNK:2 * CHUNK]
    for pr in probs:
        pr["N"] = bf(pr["N"] + pr["M"] + _dot(bf(pr["N"]), bd(bf(pr["M"]))))
        del pr["M"]
    for pr in probs:
        bh = ld(pr, 4)
        NZ = _dot(pr["N"], jnp.concatenate([bd(pr["N_br"]), bd(bh)], axis=1))
        z_br = bf(pr["N_br"].astype(F32) + NZ[:, 0:128])
        z_bh = bf(bh.astype(F32) + NZ[:, 128:256])
        pr["Z"] = jnp.concatenate([bd(z_br), bd(z_bh)], axis=1)
    for pr in probs:
        tile, half = pr["ck"] // 2, pr["ck"] % 2
        vt_p = dirs[pr["d"]][7][0, pr["ls"], tile * 128:(tile + 1) * 128]
        in_half = (lane2 < CHUNK) if half == 0 else (lane2 >= CHUNK)
        vtm = jnp.where(in_half, vt_p, jnp.zeros_like(vt_p))
        VG = _dot(vtm, jnp.concatenate([cat2(pr["A_ak"]), cat2(pr["N_kr"]), cat2(ld(pr, 5))], axis=1))
        pr["VA"] = _sel(VG[:, 0:128], m0)
        pr["VN"] = _sel(VG[:, 128:256], m0)
        pr["VK"] = _sel(VG[:, 256:384], m0)

    S = {(d, p): s_ref[d, p] for d in range(2) for p in range(N_PAIRS)}
    ys = {}
    for rnd in range(WKV_CPS):
        cur = [pr for pr in probs if pr["rnd"] == rnd]
        for pr in cur:
            St = S[(pr["d"], pr["p"])]
            SG = _dot_nt(bf(St), rhs1(pr))
            pr["X"] = bf(SG[:, 0:128] + pr["VA"])
            pr["Y"] = SG[:, 128:256] + pr["VN"]
        for pr in cur:
            UG = _dot(pr["X"], pr["Z"])
            key = (pr["d"], pr["p"])
            pend = dirs[pr["d"]][6][0, pr["ck"], 0:1, pr["ls"]]
            S[key] = S[key] * pend + UG[:, 128:256] + pr["VK"]
            ys[(pr["d"], pr["p"], pr["ck"])] = pr["Y"] + UG[:, 0:128]
    for d in range(2):
        y_ref = dirs[d][8]
        for p in range(N_PAIRS):
            s_ref[d, p] = S[(d, p)]
            for tile in range(WKV_CPS // 2):
                y0, y1 = ys[(d, p, 2 * tile)], ys[(d, p, 2 * tile + 1)]
                ts = slice(tile * 128, (tile + 1) * 128)
                y_ref[0, p * 128:p * 128 + CHUNK, ts] = jnp.where(m0, y0, pltpu.roll(y1, CHUNK, axis=1)).astype(BF16)
                y_ref[0, p * 128 + CHUNK:(p + 1) * 128, ts] = jnp.where(m0, pltpu.roll(y0, CHUNK, axis=1), y1).astype(BF16)


def _wkv(prep, n_ctx):
    (at_f, bt_f, kt_f, rt_f, bh_f, kh_f, pe_f, at_b, bt_b, kt_b, rt_b, bh_b, kh_b, pe_b, vt) = prep
    B, T, _ = at_f.shape
    blk = WKV_CPS * CHUNK
    assert T % blk == 0 and n_ctx % blk == 0
    ns = T // blk
    nc2 = n_ctx // blk

    def mrev(s):
        return jnp.where(s < nc2, nc2 - 1 - s, ns - 1 - (s - nc2))

    def specs(idx):
        tm = pl.BlockSpec((1, blk, D_HEADS), lambda b, s: (b, idx(s), 0))
        pe = pl.BlockSpec((1, WKV_CPS, 8, D_HEADS), lambda b, s: (b, idx(s), 0, 0))
        vts = pl.BlockSpec((1, D_HEADS, blk), lambda b, s: (b, 0, idx(s)))
        return [tm] * 6 + [pe, vts]

    fwd = lambda s: s
    yt = lambda idx: pl.BlockSpec((1, D_HEADS, blk), lambda b, s: (b, 0, idx(s)))
    return pl.pallas_call(
        _wkv_kernel,
        grid=(B, ns),
        in_specs=specs(fwd) + specs(mrev),
        out_specs=[yt(fwd), yt(mrev)],
        out_shape=[jax.ShapeDtypeStruct((B, D_HEADS, T), BF16)] * 2,
        scratch_shapes=[pltpu.VMEM((2, N_PAIRS, CHUNK, 2 * CHUNK), F32)],
        compiler_params=_cparams(("arbitrary", "arbitrary")),
    )(at_f, bt_f, kt_f, rt_f, bh_f, kh_f, pe_f, vt, at_b, bt_b, kt_b, rt_b, bh_b, kh_b, pe_b, vt)


def _rwpost_tile(yf, yb, bonus, g, gn_g, gn_b):
    y = yf.astype(F32) + yb.astype(F32)
    R = y.shape[1]
    y3 = y.reshape(N_HEADS, HEAD_DIM, R)
    mu = jnp.mean(y3, axis=1, keepdims=True)
    var = jnp.mean(jnp.square(y3 - mu), axis=1, keepdims=True)
    yn = ((y3 - mu) * lax.rsqrt(var + GN_EPS)).reshape(D_HEADS, R)
    out = (yn.T * gn_g + gn_b + bonus.astype(F32)) * g.astype(F32)
    return out.astype(BF16)


ATT_ROWS = 4
WKV_CPS = 4


def _attn_kernel(q_ref, k_ref, v_ref, *rest, n_ctx, n_rows):
    bias_refs, o_ref = rest[:ATT_ROWS], rest[ATT_ROWS]
    j = pl.program_id(1)
    n_cstep = n_ctx // (GRID_W * ATT_ROWS)
    lane = lax.broadcasted_iota(jnp.int32, (GRID_W, 128), 1)
    m0 = lane < HEAD_DIM
    win = WIN_ROWS * GRID_W
    rmax = lambda a: jnp.max(a, axis=-1, keepdims=True)
    rsum = lambda a: jnp.sum(a, axis=-1, keepdims=True)
    probs = [(u, p, slice(u * GRID_W, (u + 1) * GRID_W), slice(p * 128, (p + 1) * 128))
             for u in range(ATT_ROWS) for p in range(N_PAIRS)]

    def stacked_q(rs, ls):
        return _bd(q_ref[0, rs, ls], m0)

    @pl.when(j < n_cstep)
    def _():
        sc = [_dot_nt(stacked_q(rs, ls), k_ref[0, 0:n_ctx, ls]) for _, _, rs, ls in probs]
        mx = [rmax(a) for a in sc]
        ex = [jnp.exp(a - m) for a, m in zip(sc, mx)]
        den = [rsum(e) for e in ex]
        for n, (_, _, rs, ls) in enumerate(probs):
            o = _dot(ex[n].astype(BF16), v_ref[0, 0:n_ctx, ls]) / den[n]
            o_ref[0, rs, ls] = _sel(o, m0).astype(BF16)

    @pl.when(j >= n_cstep)
    def _():
        starts = []
        for u in range(ATT_ROWS):
            i = (j - n_cstep) * ATT_ROWS + u
            r0 = jnp.clip(i - WIN_ROWS // 2, 0, n_rows - WIN_ROWS)
            starts.append(pl.multiple_of(n_ctx + r0 * GRID_W, GRID_W))
        qs = [stacked_q(rs, ls) for _, _, rs, ls in probs]
        s_loc = [_dot_nt(qs[n], k_ref[0, pl.ds(starts[u], win), ls]) + bias_refs[u][0, p]
                 for n, (u, p, _, ls) in enumerate(probs)]
        s_ctx = [_dot_nt(qs[n], k_ref[0, 0:n_ctx, ls]) for n, (_, _, _, ls) in enumerate(probs)]
        mx = [jnp.maximum(rmax(a), rmax(b)) for a, b in zip(s_loc, s_ctx)]
        e_loc = [jnp.exp(a - m) for a, m in zip(s_loc, mx)]
        e_ctx = [jnp.exp(a - m) for a, m in zip(s_ctx, mx)]
        den = [rsum(a) + rsum(b) for a, b in zip(e_loc, e_ctx)]
        for n, (u, _, rs, ls) in enumerate(probs):
            o = _dot(e_loc[n].astype(BF16), v_ref[0, pl.ds(starts[u], win), ls])
            o = (o + _dot(e_ctx[n].astype(BF16), v_ref[0, 0:n_ctx, ls])) / den[n]
            o_ref[0, rs, ls] = _sel(o, m0).astype(BF16)


def _attention(qkv, bias_tab, n_ctx):
    B, T, _ = qkv.shape
    n_rows = (T - n_ctx) // GRID_W
    n_cstep = n_ctx // (GRID_W * ATT_ROWS)
    half = WIN_ROWS // 2
    blk = GRID_W * ATT_ROWS

    def delta(j, u):
        i = jnp.maximum(j - n_cstep, 0) * ATT_ROWS + u
        return jnp.minimum(i, half) + jnp.maximum(i - (n_rows - half), 0)

    bias_specs = [pl.BlockSpec((1, N_PAIRS, 128, WIN_ROWS * GRID_W), functools.partial(
        lambda b, j, u: (delta(j, u), 0, 0, 0), u=u)) for u in range(ATT_ROWS)]
    return pl.pallas_call(
        functools.partial(_attn_kernel, n_ctx=n_ctx, n_rows=n_rows),
        grid=(B, T // blk),
        in_specs=[
            pl.BlockSpec((1, blk, D_HEADS), lambda b, j: (b, j, 0)),
            pl.BlockSpec((1, T, D_HEADS), lambda b, j: (b, 0, 1)),
            pl.BlockSpec((1, T, D_HEADS), lambda b, j: (b, 0, 2)),
        ] + bias_specs,
        out_specs=pl.BlockSpec((1, blk, D_HEADS), lambda b, j: (b, j, 0)),
        out_shape=jax.ShapeDtypeStruct((B, T, D_HEADS), BF16),
        compiler_params=_cparams(("arbitrary", "arbitrary")),
    )(qkv, qkv, qkv, *([bias_tab] * ATT_ROWS))


def _na_bias_table(rpb):
    H = rpb.shape[0]
    c = np.arange(GRID_W)[:, None]
    kc = np.arange(GRID_W)[None, :]
    cs = np.clip(c - WIN_COLS // 2, 0, GRID_W - WIN_COLS)
    valid = (kc >= cs) & (kc < cs + WIN_COLS)
    cidx = np.clip(kc - c + (WIN_COLS - 1), 0, 2 * WIN_COLS - 2)
    hot = jnp.asarray(cidx[None] == np.arange(2 * WIN_COLS - 1)[:, None, None], F32)
    t = jnp.einsum("hro,ock->hcrk", rpb.astype(F32), hot, precision=lax.Precision.HIGHEST)
    t = jnp.where(valid[None, :, None, :], t, NEG)
    tabs = [t[:, :, WIN_ROWS - 1 - dl:2 * WIN_ROWS - 1 - dl, :].reshape(H, GRID_W, WIN_ROWS * GRID_W)
            for dl in range(WIN_ROWS)]
    return jnp.stack(tabs, 0).reshape(WIN_ROWS, H // 2, 2 * GRID_W, WIN_ROWS * GRID_W)


def _layer_norm(h, g, b):
    mu = jnp.mean(h, axis=-1, keepdims=True)
    var = jnp.mean(jnp.square(h - mu), axis=-1, keepdims=True)
    return (h - mu) * lax.rsqrt(var + LN_EPS) * g + b


def _to_token_tiles(ref, val):
    n = val.shape[0]
    for j in range(TOK_SUB):
        ref[pl.ds(j, n, stride=TOK_SUB), :] = val[:, j * 128:(j + 1) * 128]


def _from_token_tiles(ref, n):
    return [ref[pl.ds(j, n, stride=TOK_SUB), :] for j in range(TOK_SUB)]


CLS_ROWS = 64
_EXPERT_ORDER = np.arange(N_EXPERTS).reshape(N_GROUPS, EXPERTS_PER_GROUP).T.reshape(-1)


def _route_tile(logits_t, bias_col, tri_t, carry):
    n = logits_t.shape[1]
    neg = jnp.float32(-jnp.inf)
    scores = jax.nn.sigmoid(logits_t)
    sel = scores + bias_col
    s = [sel[N_GROUPS * k:N_GROUPS * (k + 1)] for k in range(EXPERTS_PER_GROUP)]
    c = [scores[N_GROUPS * k:N_GROUPS * (k + 1)] for k in range(EXPERTS_PER_GROUP)]
    hi01, lo01 = jnp.maximum(s[0], s[1]), jnp.minimum(s[0], s[1])
    hi23, lo23 = jnp.maximum(s[2], s[3]), jnp.minimum(s[2], s[3])
    top2 = jnp.maximum(hi01, hi23) + jnp.maximum(jnp.minimum(hi01, hi23), jnp.maximum(lo01, lo23))
    grp = lax.broadcasted_iota(jnp.int32, (N_GROUPS, n), 0).astype(F32)
    gmax = jnp.max(top2, axis=0, keepdims=True)
    g_idx = jnp.min(jnp.where(top2 == gmax, grp, jnp.float32(N_GROUPS)), axis=0, keepdims=True)
    pick = grp == g_idx
    v = [jnp.sum(jnp.where(pick, a, 0.0), axis=0, keepdims=True) for a in s]
    w = [jnp.sum(jnp.where(pick, a, 0.0), axis=0, keepdims=True) for a in c]

    def first_argmax(vals):
        best, idx = vals[0], jnp.zeros_like(vals[0])
        for k in range(1, len(vals)):
            upd = vals[k] > best
            best = jnp.where(upd, vals[k], best)
            idx = jnp.where(upd, jnp.float32(k), idx)
        return idx

    i1 = first_argmax(v)
    i2 = first_argmax([jnp.where(i1 == k, neg, v[k]) for k in range(EXPERTS_PER_GROUP)])
    lo, hi = jnp.minimum(i1, i2), jnp.maximum(i1, i2)
    g_lo = sum(jnp.where(lo == k, w[k], 0.0) for k in range(EXPERTS_PER_GROUP))
    g_hi = sum(jnp.where(hi == k, w[k], 0.0) for k in range(EXPERTS_PER_GROUP))
    tot = g_lo + g_hi
    pair = lo * 3.0 - lo * (lo - 1.0) * 0.5 + (hi - lo - 1.0)
    cls = g_idx * 6.0 + pair
    ranks = []
    m = tri_t.shape[0]
    crow = lax.broadcasted_iota(jnp.int32, (CLS_ROWS, m), 0).astype(F32)
    for q in range(n // m):
        onehot = crow == cls[:, q * m:(q + 1) * m]
        within = _dot(onehot.astype(BF16), tri_t)
        ranks.append(jnp.sum(jnp.where(onehot, within + carry, 0.0), axis=0, keepdims=True))
        carry = carry + jnp.sum(onehot.astype(F32), axis=1, keepdims=True)
    zero = jnp.zeros((4, n), F32)
    return jnp.concatenate([g_lo / tot, g_hi / tot, cls, jnp.concatenate(ranks, axis=1), zero], axis=0), carry


def _route_kernel(lg_ref, rb_ref, tri_ref, route_ref, cnt_ref, carry_ref):
    @pl.when(pl.program_id(0) == 0)
    def _():
        carry_ref[...] = jnp.zeros_like(carry_ref)

    route, carry = _route_tile(lg_ref[...], rb_ref[...], tri_ref[...], carry_ref[:, 0:1])
    route_ref[...] = route
    carry_ref[...] = jnp.broadcast_to(carry, carry_ref.shape)
    cnt_ref[...] = jnp.broadcast_to(carry, cnt_ref.shape)


def _route(logits_t, router_bias):
    N = logits_t.shape[1]
    tile = next(t for t in (8 * ROW_TILE, 4 * ROW_TILE, 2 * ROW_TILE, ROW_TILE) if N % t == 0)
    r = np.arange(ROW_TILE)
    tri = jnp.asarray(r[:, None] < r[None, :], BF16)
    bias_col = router_bias.astype(F32)[_EXPERT_ORDER].reshape(N_EXPERTS, 1)
    full = lambda a: pl.BlockSpec(a.shape, lambda i: (0, 0))
    return pl.pallas_call(
        _route_kernel,
        grid=(N // tile,),
        in_specs=[pl.BlockSpec((N_EXPERTS, tile), lambda i: (0, i)), full(bias_col), full(tri)],
        out_specs=[pl.BlockSpec((8, tile), lambda i: (0, i)), pl.BlockSpec((CLS_ROWS, 128), lambda i: (0, 0))],
        out_shape=[jax.ShapeDtypeStruct((8, N), F32), jax.ShapeDtypeStruct((CLS_ROWS, 128), F32)],
        scratch_shapes=[pltpu.VMEM((CLS_ROWS, 128), F32)],
        compiler_params=_cparams(("arbitrary",)),
    )(logits_t, bias_col, tri)


def _outproj_kernel(na_ref, yf_ref, yb_ref, bonus_ref, rg_ref, gg_ref, gb_ref, w_ref, zc_ref, zx_ref, mod_ref,
                    g_ref, b_ref, rwh_ref, rwl_ref, z1_ref, hx_ref, lg_ref, *, d, alpha, n_ctx_tiles, t0):
    rw = _rwpost_tile(yf_ref[0], yb_ref[0], bonus_ref[0], rg_ref[0], gg_ref[...], gb_ref[...])
    o = _dot(na_ref[0], w_ref[0:D_HEADS, :]) + _dot(rw, w_ref[D_HEADS:2 * D_HEADS, :])
    mod = mod_ref[0, 0]
    gate = mod[:, 2 * d:3 * d]
    z = _z_tile(zc_ref, zx_ref, n_ctx_tiles, t0)
    z1 = _layer_norm(alpha * z + gate * o, g_ref[...], b_ref[...])
    z1_ref[0] = z1
    hx = z1 * (1.0 + mod[:, 4 * d:5 * d]) + mod[:, 3 * d:4 * d]
    _to_token_tiles(hx_ref, hx)
    hx_hi, hx_lo = _split2(hx)
    wh, wl = rwh_ref[...], rwl_ref[...]
    logits = _dot(hx_hi, wh) + _dot(hx_hi, wl) + _dot(hx_lo, wh)
    lg_ref[...] = logits.T[0:N_EXPERTS]


def _outproj(na, rwkv, gn_g, gn_b, w_out_b, zc, zx, sub, modsel, ln_g, ln_b, router_w, n_ctx_tiles, t0, alpha):
    yf, yb, bonus, rgate = rwkv
    B, T, _ = na.shape
    D = zc.shape[2]
    nt = T // ROW_TILE - t0
    half = pl.BlockSpec((1, ROW_TILE, D_HEADS), lambda b, t: (b, t + t0, 0))
    ytile = pl.BlockSpec((1, D_HEADS, ROW_TILE), lambda b, t: (b, 0, t + t0))
    hrow = pl.BlockSpec((1, D_HEADS), lambda b, t: (0, 0))
    tile = pl.BlockSpec((1, ROW_TILE, D), lambda b, t: (b, t, 0))
    row = pl.BlockSpec((1, D), lambda b, t: (0, 0))
    cspec, xspec = _z_specs(D, n_ctx_tiles, sub, t0)
    wt = jnp.pad(router_w[:, _EXPERT_ORDER], ((0, 0), (0, 128 - N_EXPERTS)))
    wt_hi = wt.astype(BF16)
    wt_lo = (wt - wt_hi.astype(F32)).astype(BF16)
    full = lambda a: pl.BlockSpec(a.shape, lambda b, t: (0, 0))
    return pl.pallas_call(
        functools.partial(_outproj_kernel, d=D, alpha=alpha, n_ctx_tiles=n_ctx_tiles, t0=t0),
        grid=(B, nt),
        in_specs=[
            half, ytile, ytile, half, half, hrow, hrow,
            pl.BlockSpec((2 * D_HEADS, D), lambda b, t: (0, 0)),
            cspec, xspec,
            _mod_spec(D, n_ctx_tiles, t0),
            row, row,
            full(wt_hi), full(wt_lo),
        ],
        out_specs=[
            tile,
            pl.BlockSpec((ROW_TILE * TOK_SUB, 128), lambda b, t: (b * nt + t, 0)),
            pl.BlockSpec((N_EXPERTS, ROW_TILE), lambda b, t: (0, b * nt + t)),
        ],
        out_shape=[
            jax.ShapeDtypeStruct((B, nt * ROW_TILE, D), F32),
            jax.ShapeDtypeStruct((B * nt * ROW_TILE * TOK_SUB, 128), F32),
            jax.ShapeDtypeStruct((N_EXPERTS, B * nt * ROW_TILE), F32),
        ],
        compiler_params=_cparams(("arbitrary", "arbitrary")),
    )(na, yf, yb, bonus, rgate, gn_g, gn_b, w_out_b, zc, zx, modsel, ln_g, ln_b, wt_hi, wt_lo)


def _tok(ref, i):
    return ref.at[pl.ds(pl.multiple_of(i * TOK_SUB, TOK_SUB), TOK_SUB)]


def _dispatch_kernel(dest_ref, hx_ref, xs_in, xs_out, sem):
    del xs_in
    n = dest_ref.shape[0]

    def start(i, c):
        for par in range(2):
            r = 2 * i + par
            pltpu.make_async_copy(_tok(hx_ref, r), _tok(xs_out, dest_ref[r]), sem).start(priority=par)
        return c

    lax.fori_loop(0, n // 2, start, 0, unroll=4)

    def wait(r, c):
        pltpu.make_async_copy(_tok(hx_ref, 0), _tok(xs_out, 0), sem).wait()
        return c

    lax.fori_loop(0, n, wait, 0, unroll=8)


def _dispatch(dest, hx_tiles, n_rows_pad):
    N = dest.shape[0]
    tile = next(t for t in (4 * ROW_TILE, 2 * ROW_TILE, ROW_TILE) if N % t == 0)
    xs0 = jnp.zeros((n_rows_pad * TOK_SUB, 128), F32)
    return pl.pallas_call(
        _dispatch_kernel,
        grid=(N // tile,),
        in_specs=[
            pl.BlockSpec((tile,), lambda i: (i,), memory_space=pltpu.SMEM),
            pl.BlockSpec((tile * TOK_SUB, 128), lambda i: (i, 0)),
            pl.BlockSpec(memory_space=pl.ANY),
        ],
        out_specs=pl.BlockSpec(memory_space=pl.ANY),
        out_shape=jax.ShapeDtypeStruct((n_rows_pad * TOK_SUB, 128), F32),
        scratch_shapes=[pltpu.SemaphoreType.DMA(())],
        input_output_aliases={2: 0},
        compiler_params=_cparams(("arbitrary",)),
    )(dest, hx_tiles, xs0)


def _pack_bf16_pair(a, b):
    ha = lax.bitcast_convert_type(a.astype(BF16).astype(F32), jnp.uint32)
    hb = lax.bitcast_convert_type(b.astype(BF16).astype(F32), jnp.uint32)
    return jnp.bitwise_or(ha, jnp.right_shift(hb, jnp.uint32(16)))


def _unpack_bf16_pair(w):
    a = lax.bitcast_convert_type(jnp.bitwise_and(w, jnp.uint32(0xFFFF0000)), F32)
    b = lax.bitcast_convert_type(jnp.left_shift(w, jnp.uint32(16)), F32)
    return a, b


def _expert_kernel(sc_ref, xs_ref, w1_hbm, w3_hbm, w2_hbm, ys_ref, s13, s2, c13, c2, sem, *, layer):
    i = pl.program_id(0)
    slot = sc_ref[2, i]

    def copies(sl, ea, eb):
        out = []
        for j, e in enumerate((ea, eb)):
            out.append(pltpu.make_async_copy(w1_hbm.at[layer, e], s13.at[sl, 2 * j], sem.at[sl, 3 * j]))
            out.append(pltpu.make_async_copy(w3_hbm.at[layer, e], s13.at[sl, 2 * j + 1], sem.at[sl, 3 * j + 1]))
            out.append(pltpu.make_async_copy(w2_hbm.at[layer, e], s2.at[sl, j], sem.at[sl, 3 * j + 2]))
        return out

    @pl.when(i == 0)
    def _():
        for cp in copies(0, sc_ref[3, i], sc_ref[4, i]):
            cp.start()

    for sl in range(2):
        @pl.when(jnp.logical_and(sc_ref[1, i] != 0, slot == sl))
        def _():
            for cp in copies(sl, sc_ref[3, i], sc_ref[4, i]):
                cp.wait()
            def cast(r, c):
                rows = pl.ds(pl.multiple_of(r * 64, 64), 64)
                for j in range(4):
                    c13[j, rows, :] = s13[sl, j, rows, :].astype(BF16)
                return c

            lax.fori_loop(0, s13.shape[2] // 64, cast, 0)

            def cast2(r, c):
                rows = pl.ds(pl.multiple_of(r * 64, 64), 64)
                for j in range(2):
                    c2[j, rows, :] = s2[sl, j, rows, :].astype(BF16)
                return c

            lax.fori_loop(0, s2.shape[2] // 64, cast2, 0)

            @pl.when(sc_ref[5, i] != 0)
            def _():
                for cp in copies(1 - sl, sc_ref[6, i], sc_ref[7, i]):
                    cp.start()

    @pl.when(sc_ref[0, i] != 0)
    def _():
        x = jnp.concatenate(_from_token_tiles(xs_ref, MOE_TILE), axis=1).astype(BF16)

        def ffn(j):
            h1 = _dot(x, c13[2 * j])
            h3 = _dot(x, c13[2 * j + 1])
            h = (h1 * jax.nn.sigmoid(h1)) * h3
            return _dot(h.astype(BF16), c2[j])

        _to_token_tiles(ys_ref, _pack_bf16_pair(ffn(0), ffn(1)))

    @pl.when(sc_ref[0, i] == 0)
    def _():
        ys_ref[...] = jnp.zeros_like(ys_ref)


def _experts(blk_sched, xs, w1, w3, w2, l):
    nb = xs.shape[0] // (MOE_TILE * TOK_SUB)
    _, _, D, DE = w1.shape
    tok = pl.BlockSpec((MOE_TILE * TOK_SUB, 128), lambda i, sc: (i, 0))
    hbm = pl.BlockSpec(memory_space=pl.ANY)
    grid_spec = pltpu.PrefetchScalarGridSpec(
        num_scalar_prefetch=1,
        grid=(nb,),
        in_specs=[tok, hbm, hbm, hbm],
        out_specs=tok,
        scratch_shapes=[
            pltpu.VMEM((2, 4, D, DE), F32),
            pltpu.VMEM((2, 2, DE, D), F32),
            pltpu.VMEM((4, D, DE), BF16),
            pltpu.VMEM((2, DE, D), BF16),
            pltpu.SemaphoreType.DMA((2, 6)),
        ],
    )
    return pl.pallas_call(
        functools.partial(_expert_kernel, layer=l),
        grid_spec=grid_spec,
        out_shape=jax.ShapeDtypeStruct(xs.shape, jnp.uint32),
        compiler_params=_cparams(("arbitrary",)),
    )(blk_sched, xs, w1, w3, w2)


def _combine_kernel(dest_ref, dnext_ref, ys_ref, gate_ref, z1_ref, mod_ref, g_ref, b_ref, o_ref, buf, sem,
                    *, d, alpha):
    n = dest_ref.shape[0]
    step = pl.program_id(0) * pl.num_programs(1) + pl.program_id(1)
    n_steps = pl.num_programs(0) * pl.num_programs(1)
    slot = step % 2

    def gather(idx_ref, sl):
        def start(i, c):
            for par in range(2):
                r = 2 * i + par
                pltpu.make_async_copy(_tok(ys_ref, idx_ref[r]), _tok(buf.at[sl], r), sem.at[sl]).start(priority=par)
            return c

        lax.fori_loop(0, n // 2, start, 0, unroll=4)

    @pl.when(step == 0)
    def _():
        gather(dest_ref, 0)

    @pl.when(step + 1 < n_steps)
    def _():
        gather(dnext_ref, 1 - slot)

    def wait(r, c):
        pltpu.make_async_copy(_tok(ys_ref, 0), _tok(buf.at[slot], 0), sem.at[slot]).wait()
        return c

    lax.fori_loop(0, n, wait, 0, unroll=8)
    gates = gate_ref[...].T
    ga = gates[:, 0:1]
    gb = gates[:, 1:2]
    parts = []
    for w in _from_token_tiles(buf.at[slot], n):
        fa, fb = _unpack_bf16_pair(w)
        parts.append(ga * fa + gb * fb)
    y = jnp.concatenate(parts, axis=1)
    gate = mod_ref[0, 0][:, 5 * d:6 * d]
    o_ref[0] = _layer_norm(alpha * z1_ref[0] + gate * y, g_ref[...], b_ref[...])


def _combine(dest, ys, gates, z1, modsel, ln_g, ln_b, n_ctx_tiles, t0, t_out, alpha):
    B, T1, D = z1.shape
    nt1 = T1 // ROW_TILE
    skip = t_out - t0
    nt = nt1 - skip
    tile = lambda off: pl.BlockSpec((1, ROW_TILE, D), lambda b, t: (b, t + off, 0))
    row = pl.BlockSpec((1, D), lambda b, t: (0, 0))

    def nxt(b, t):
        last = jnp.logical_and(b == B - 1, t == nt - 1)
        wrap = t == nt - 1
        b2 = jnp.where(jnp.logical_and(wrap, jnp.logical_not(last)), b + 1, b)
        t2 = jnp.where(last, t, jnp.where(wrap, 0, t + 1))
        return b2 * nt1 + t2 + skip

    return pl.pallas_call(
        functools.partial(_combine_kernel, d=D, alpha=alpha),
        grid=(B, nt),
        in_specs=[
            pl.BlockSpec((ROW_TILE,), lambda b, t: (b * nt1 + t + skip,), memory_space=pltpu.SMEM),
            pl.BlockSpec((ROW_TILE,), lambda b, t: (nxt(b, t),), memory_space=pltpu.SMEM),
            pl.BlockSpec(memory_space=pl.ANY),
            pl.BlockSpec((8, ROW_TILE), lambda b, t: (0, b * nt1 + t + skip)),
            tile(skip),
            _mod_spec(D, n_ctx_tiles, t_out),
            row, row,
        ],
        out_specs=tile(0),
        out_shape=jax.ShapeDtypeStruct((B, nt * ROW_TILE, D), F32),
        scratch_shapes=[pltpu.VMEM((2, ROW_TILE * TOK_SUB, 128), jnp.uint32), pltpu.SemaphoreType.DMA((2,))],
        compiler_params=_cparams(("arbitrary", "arbitrary")),
    )(dest, dest, ys, gates, z1, modsel, ln_g, ln_b)


_PAIR_LO = np.array([0, 0, 0, 1, 1, 2], np.int32)
_PAIR_HI = np.array([1, 2, 3, 2, 3, 3], np.int32)


def _schedule(route, cnt):
    cls = route[2].astype(jnp.int32)
    rank = route[3].astype(jnp.int32)
    N = cls.shape[0]
    counts = cnt[:N_CLASSES, 0].astype(jnp.int32)
    onehot = (cls[:, None] == jnp.arange(N_CLASSES, dtype=jnp.int32)[None, :])
    padded = (counts + MOE_TILE - 1) // MOE_TILE * MOE_TILE
    cls_end = jnp.cumsum(padded)
    cls_start = cls_end - padded
    dest = jnp.sum(jnp.where(onehot, cls_start[None, :], 0), axis=1) + rank

    nb = N // MOE_TILE + N_CLASSES
    blk_row = jnp.arange(nb, dtype=jnp.int32) * MOE_TILE
    total = jnp.sum(padded)
    blk_valid = (blk_row < total).astype(jnp.int32)
    row_c = jnp.minimum(blk_row, jnp.maximum(total - MOE_TILE, 0))
    blk_cls = jnp.sum((row_c[:, None] >= cls_end[None, :]).astype(jnp.int32), axis=1)
    blk_cls = jnp.minimum(blk_cls, N_CLASSES - 1)

    def experts_of(c):
        hot = (c % 6)[:, None] == jnp.arange(6, dtype=jnp.int32)[None, :]
        base = (c // 6) * EXPERTS_PER_GROUP
        return (base + jnp.sum(jnp.where(hot, jnp.asarray(_PAIR_LO)[None, :], 0), axis=1),
                base + jnp.sum(jnp.where(hot, jnp.asarray(_PAIR_HI)[None, :], 0), axis=1))

    prev_cls = jnp.concatenate([jnp.full((1,), -1, jnp.int32), blk_cls[:-1]])
    first = jnp.logical_and(blk_valid != 0, blk_cls != prev_cls).astype(jnp.int32)
    parity = (jnp.cumsum(first) - 1) % 2
    cls_ids = jnp.arange(N_CLASSES, dtype=jnp.int32)
    used = jnp.where(counts > 0, cls_ids, N_CLASSES)
    later = jnp.where(cls_ids[None, :] > cls_ids[:, None], used[None, :], N_CLASSES)
    nxt_of_cls = jnp.min(later, axis=1)
    cls_hot = blk_cls[:, None] == cls_ids[None, :]
    nxt_cls = jnp.sum(jnp.where(cls_hot, nxt_of_cls[None, :], 0), axis=1)
    has_next = (nxt_cls < N_CLASSES).astype(jnp.int32)
    ea, eb = experts_of(blk_cls)
    na, nb_ = experts_of(jnp.minimum(nxt_cls, N_CLASSES - 1))
    blk_sched = jnp.stack([blk_valid, first, parity, ea, eb, has_next, na, nb_]).astype(jnp.int32)
    return dest.astype(jnp.int32), blk_sched, nb * MOE_TILE


def _rope_tables(n_ctx, seq):
    t = np.arange(seq)
    row = (t // GRID_W).astype(np.float32)
    col = (t % GRID_W).astype(np.float32)
    n_freq = HEAD_DIM // 4
    inv = jnp.asarray(ROPE_BASE, F32) ** (-jnp.arange(n_freq, dtype=F32) / n_freq)
    ar = jnp.asarray(row)[:, None] * inv
    ac = jnp.asarray(col)[:, None] * inv
    ang = jnp.concatenate([ar, ar, ac, ac], -1)
    cos = jnp.cos(ang)
    sin = jnp.sin(ang)
    quarter = (np.arange(HEAD_DIM) // n_freq) % 2
    sa = jnp.where(quarter == 0, -sin, 0.0)
    sb = jnp.where(quarter == 1, sin, 0.0)
    pad = lambda a, fill: jnp.concatenate([jnp.full((n_ctx, HEAD_DIM), fill, F32), a], 0)
    tile = lambda a: jnp.tile(a, (1, N_HEADS))
    return tile(pad(cos, 1.0)), tile(pad(sa, 0.0)), tile(pad(sb, 0.0))


def _rw_consts(l, n_ctx, seq, rw_mu_prev, rw_mu_next, rw_w0, rw_w2, rw_a0, rw_a2, rw_g2, rw_k_k, rw_k_a, rw_r_k):
    d_rw_in = rw_mu_prev.shape[1]
    padw = lambda a: jnp.pad(a[l], (0, D_RWP - d_rw_in)).reshape(1, D_RWP)
    cos, sa, sb = _rope_tables(n_ctx, seq)
    w2p = jnp.zeros((128, 2 * D_HEADS), F32)
    w2p = w2p.at[0:LORA, 0:D_HEADS].set(rw_w2[l, 0]).at[LORA:2 * LORA, D_HEADS:].set(rw_w2[l, 1])
    a2p = jnp.zeros((128, 2 * D_HEADS), F32)
    a2p = a2p.at[2 * LORA:3 * LORA, 0:D_HEADS].set(rw_a2[l, 0]).at[3 * LORA:4 * LORA, D_HEADS:].set(rw_a2[l, 1])
    g2p = jnp.zeros((128, D_HEADS), F32).at[0:GATE_LORA].set(rw_g2[l])
    head = np.arange(D_HEADS) // HEAD_DIM
    ones_bd = jnp.asarray(head[:, None] == head[None, :], BF16)
    return dict(
        mu_prev=padw(rw_mu_prev), mu_next=padw(rw_mu_next), cos=cos, sa=sa, sb=sb,
        k_k=rw_k_k[l].reshape(1, D_HEADS), k_a=rw_k_a[l].reshape(1, D_HEADS), r_k=rw_r_k[l].reshape(1, D_HEADS),
        w0=rw_w0[l].reshape(1, 2 * D_HEADS), a0=rw_a0[l].reshape(1, 2 * D_HEADS),
        w2p=w2p.astype(BF16), a2p=a2p.astype(BF16), g2p=g2p.astype(BF16), ones_bd=ones_bd,
    )


def kernel(x, c, ctx, c_ctx, ada_w, ada_b, w_in, na_rpb, rw_mu_prev, rw_mu_next, rw_w0, rw_w2, rw_a0, rw_a2, rw_g2, rw_k_k, rw_k_a, rw_r_k, rw_gn_g, rw_gn_b, w_out, ln1_g, ln1_b, ln2_g, ln2_b, router_w, router_bias, exp_w1, exp_w3, exp_w2):
    B, S, D = x.shape
    C = ctx.shape[1]
    L = ada_w.shape[0]
    T = C + S
    assert D == 1024 and C % ROW_TILE == 0 and S % ROW_TILE == 0 and C % (2 * CHUNK) == 0
    assert S % GRID_W == 0 and S // GRID_W >= WIN_ROWS and w_in.shape[2] == D_INP - 32
    n_ctx_tiles = C // ROW_TILE
    alpha = float((2 * L) ** 0.25)

    zc, zx, sub = ctx, x, n_ctx_tiles
    n_mod = (B + 1 + 7) // 8 * 8
    cc = jnp.zeros((n_mod, D), F32).at[0:B].set(c).at[B].set(c_ctx)
    mod_all = _ada(cc, ada_w, ada_b)

    for l in range(L):
        mod_c = jnp.broadcast_to(mod_all[l, B][None], (B, 6 * D))
        modsel = jnp.stack([mod_c, mod_all[l, 0:B]], axis=1).reshape(B, 2, 1, 6 * D)
        w_in_p = jnp.pad(w_in[l], ((0, 0), (0, D_INP - w_in.shape[2]))).astype(BF16)
        qkv, p_rw = _inproj(zc, zx, sub, T, modsel, w_in_p, n_ctx_tiles)

        consts = _rw_consts(l, C, S, rw_mu_prev, rw_mu_next, rw_w0, rw_w2, rw_a0, rw_a2, rw_g2,
                            rw_k_k, rw_k_a, rw_r_k)
        prep = _rwprep(p_rw, consts, n_ctx_tiles)
        yf, yb = _wkv(prep[0:15], C)
        na = _attention(qkv, _na_bias_table(na_rpb[l]), C)

        t0 = n_ctx_tiles if l == L - 1 else 0
        z1, hx_tiles, logits_t = _outproj(na, (yf, yb, prep[15], prep[16]), rw_gn_g[l].reshape(1, D_HEADS),
                                          rw_gn_b[l].reshape(1, D_HEADS), w_out[l].astype(BF16), zc, zx, sub,
                                          modsel, ln1_g[l].reshape(1, D), ln1_b[l].reshape(1, D), router_w,
                                          n_ctx_tiles, t0, alpha)
        route, cnt = _route(logits_t, router_bias)
        dest, blk_sched, n_rows_pad = _schedule(route, cnt)
        xs = _dispatch(dest, hx_tiles, n_rows_pad)
        ys = _experts(blk_sched, xs, exp_w1, exp_w3, exp_w2, l)
        z = _combine(dest, ys, route, z1, modsel, ln2_g[l].reshape(1, D), ln2_b[l].reshape(1, D),
                     n_ctx_tiles, t0, t0, alpha)
        zc, zx, sub = z, z, 0

    return z
```

```python
import functools
import math

import jax
import jax.numpy as jnp
import numpy as np
from jax import lax
from jax.experimental import pallas as pl
from jax.experimental.pallas import tpu as pltpu

F32 = jnp.float32
BF16 = jnp.bfloat16

HEAD_DIM = 64
N_HEADS = 8
D_HEADS = N_HEADS * HEAD_DIM
N_PAIRS = N_HEADS // 2
GRID_W = 64
WIN_ROWS = 8
WIN_COLS = 16
LORA = 32
GATE_LORA = 96
N_EXPERTS = 32
N_GROUPS = 8
EXPERTS_PER_GROUP = 4
N_CLASSES = N_GROUPS * 6
ROPE_BASE = 10000.0
LN_EPS = 1e-6
GN_EPS = 64e-5
CHUNK = 64
ROW_TILE = 256
MOE_TILE = 256
D_QKV = 3 * D_HEADS
D_RWP = 3 * D_HEADS + 256
D_INP = D_QKV + D_RWP
TOK_SUB = 8
NEG = -1e30
VMEM_LIMIT = 56 * 1024 * 1024


def _cparams(sem):
    return pltpu.CompilerParams(dimension_semantics=sem, vmem_limit_bytes=VMEM_LIMIT)


def _dot(a, b):
    return jnp.dot(a, b, preferred_element_type=F32)


def _dot_nt(a, b):
    return lax.dot_general(a, b, (((1,), (1,)), ((), ())), preferred_element_type=F32)


def _split2(a):
    hi = a.astype(BF16)
    lo = (a - hi.astype(F32)).astype(BF16)
    return hi, lo


def _dot3(a, b):
    ah, al = _split2(a)
    bh, bl = _split2(b)
    return _dot(ah, bh) + _dot(al, bh) + _dot(ah, bl)


def _ada_kernel(cc_ref, w_ref, b_ref, o_ref):
    cc = cc_ref[...]
    s = cc * jax.nn.sigmoid(cc)
    o_ref[0] = _dot3(s, w_ref[0]) + b_ref[0]


def _ada(cc, ada_w, ada_b):
    L, D, D6 = ada_w.shape
    R = cc.shape[0]
    tn = 1536
    return pl.pallas_call(
        _ada_kernel,
        grid=(L, D6 // tn),
        in_specs=[
            pl.BlockSpec((R, D), lambda l, n: (0, 0)),
            pl.BlockSpec((1, D, tn), lambda l, n: (l, 0, n)),
            pl.BlockSpec((1, 1, tn), lambda l, n: (l, 0, n)),
        ],
        out_specs=pl.BlockSpec((1, R, tn), lambda l, n: (l, 0, n)),
        out_shape=jax.ShapeDtypeStruct((L, R, D6), F32),
        compiler_params=_cparams(("arbitrary", "arbitrary")),
    )(cc, ada_w, ada_b.reshape(L, 1, D6))


def _z_specs(D, n_ctx_tiles, sub, t0=0):
    cspec = pl.BlockSpec((1, ROW_TILE, D), lambda b, t: (b, jnp.minimum(t + t0, n_ctx_tiles - 1), 0))
    xspec = pl.BlockSpec((1, ROW_TILE, D), lambda b, t: (b, jnp.maximum(t + t0, n_ctx_tiles) - sub, 0))
    return cspec, xspec


def _z_tile(zc_ref, zx_ref, n_ctx_tiles, t0=0):
    return jnp.where(pl.program_id(1) + t0 < n_ctx_tiles, zc_ref[0], zx_ref[0])


def _mod_spec(D, n_ctx_tiles, t0=0):
    return pl.BlockSpec((1, 1, 1, 6 * D), lambda b, t: (b, jnp.where(t + t0 < n_ctx_tiles, 0, 1), 0, 0))


def _inproj_kernel(zc_ref, zx_ref, mod_ref, w_ref, qkv_ref, rw_ref, *, d, n_ctx_tiles):
    z = _z_tile(zc_ref, zx_ref, n_ctx_tiles)
    mod = mod_ref[0, 0]
    shift = mod[:, 0:d]
    scale = mod[:, d:2 * d]
    h = (z * (1.0 + scale) + shift).astype(BF16)
    q = _dot(h, w_ref[:, 0:D_HEADS])
    qkv_ref[0, :, 0:D_HEADS] = (q * (HEAD_DIM ** -0.5)).astype(BF16)
    kv = _dot(h, w_ref[:, D_HEADS:D_QKV])
    qkv_ref[0, :, D_HEADS:D_QKV] = kv.astype(BF16)
    rw_ref[0] = _dot(h, w_ref[:, D_QKV:D_INP])


def _inproj(zc, zx, sub, T, modsel, w_in_p, n_ctx_tiles):
    B, _, D = zc.shape
    nt = T // ROW_TILE
    cspec, xspec = _z_specs(D, n_ctx_tiles, sub)
    return pl.pallas_call(
        functools.partial(_inproj_kernel, d=D, n_ctx_tiles=n_ctx_tiles),
        grid=(B, nt),
        in_specs=[
            cspec, xspec,
            pl.BlockSpec((1, 1, 1, 6 * D), lambda b, t: (b, jnp.where(t < n_ctx_tiles, 0, 1), 0, 0)),
            pl.BlockSpec((D, D_INP), lambda b, t: (0, 0)),
        ],
        out_specs=[
            pl.BlockSpec((1, ROW_TILE, D_QKV), lambda b, t: (b, t, 0)),
            pl.BlockSpec((1, ROW_TILE, D_RWP), lambda b, t: (b, t, 0)),
        ],
        out_shape=[
            jax.ShapeDtypeStruct((B, T, D_QKV), BF16),
            jax.ShapeDtypeStruct((B, T, D_RWP), F32),
        ],
        compiler_params=_cparams(("arbitrary", "arbitrary")),
    )(zc, zx, modsel, w_in_p)


def _rwprep_kernel(p_ref, pp_ref, pn_ref, mup_ref, mun_ref, cos_ref, sa_ref, sb_ref,
                   kk_ref, ka_ref, rk_ref, w0_ref, a0_ref, w2_ref, a2_ref, g2_ref, ones_ref,
                   at_f, bt_f, kt_f, rt_f, bh_f, kh_f, pe_f,
                   at_b, bt_b, kt_b, rt_b, bh_b, kh_b, pe_b,
                   vt_ref, bonus_ref, g_ref, sh_ref, *, n_ctx_tiles, n_tiles):
    t = pl.program_id(1)
    P = p_ref[0]
    R = P.shape[0]
    prev_ok = jnp.logical_and(t != 0, t != n_ctx_tiles)
    next_ok = jnp.logical_and(t != n_ctx_tiles - 1, t != n_tiles - 1)
    sh_ref[8:R + 8, :] = P
    sh_ref[0:8, :] = jnp.where(prev_ok, pp_ref[0], 0.0)
    sh_ref[R + 8:R + 16, :] = jnp.where(next_ok, pn_ref[0], 0.0)
    prev = sh_ref[7:R + 7, :]
    nxt = sh_ref[9:R + 9, :]
    mup = mup_ref[...]
    mun = mun_ref[...]
    z = (1.0 - mup - mun) * P + mup * prev + mun * nxt

    cos = cos_ref[...]
    sa = sa_ref[...]
    sb = sb_ref[...]

    def rope(u):
        q = HEAD_DIM // 4
        parts = []
        for m in range(D_HEADS // 128):
            ls = slice(m * 128, (m + 1) * 128)
            um = u[:, ls]
            parts.append(um * cos[:, ls] + pltpu.roll(um, 128 - q, axis=1) * sa[:, ls]
                         + pltpu.roll(um, q, axis=1) * sb[:, ls])
        return jnp.concatenate(parts, axis=1)

    r = rope(z[:, 0:D_HEADS])
    k = rope(z[:, D_HEADS:2 * D_HEADS])
    v = z[:, 2 * D_HEADS:3 * D_HEADS]
    ones_bd = ones_ref[...]

    kk = k * kk_ref[...]
    sq_hi, sq_lo = _split2(kk * kk)
    kk = kk * lax.rsqrt(jnp.maximum(_dot(sq_hi, ones_bd) + _dot(sq_lo, ones_bd), 1e-24))

    sigmoid = lambda u: 0.5 * jnp.tanh(0.5 * u) + 0.5
    slab = z[:, 3 * D_HEADS:3 * D_HEADS + 128]
    u_w = w0_ref[...] + _dot(jnp.tanh(slab).astype(BF16), w2_ref[...])
    u_a = a0_ref[...] + _dot(slab.astype(BF16), a2_ref[...])
    g_ref[0] = _dot(sigmoid(z[:, 3 * D_HEADS + 128:D_RWP]).astype(BF16), g2_ref[...]).astype(BF16)
    e_all = math.exp(-0.5) * sigmoid(u_w)
    a_all = sigmoid(u_a)

    ci = lax.broadcasted_iota(jnp.int32, (CHUNK, CHUNK), 0)
    cj = lax.broadcasted_iota(jnp.int32, (CHUNK, CHUNK), 1)
    ka = ka_ref[...]
    outs = ((at_f, bt_f, kt_f, rt_f, bh_f, kh_f, pe_f), (at_b, bt_b, kt_b, rt_b, bh_b, kh_b, pe_b))
    kd_sum = None
    for d in range(2):
        e = e_all[:, d * D_HEADS:(d + 1) * D_HEADS]
        a = a_all[:, d * D_HEADS:(d + 1) * D_HEADS]
        tri = (cj <= ci) if d == 0 else (cj >= ci)
        tri = tri.astype(BF16)
        cs, ce = [], []
        for q in range(R // CHUNK):
            e_hi, e_lo = _split2(e[q * CHUNK:(q + 1) * CHUNK])
            cq = -(_dot(tri, e_hi) + _dot(tri, e_lo))
            end = cq[CHUNK - 1:CHUNK] if d == 0 else cq[0:1]
            cs.append(cq)
            ce.append(jnp.broadcast_to(end, cq.shape))
        c = jnp.concatenate(cs, axis=0)
        cend = jnp.concatenate(ce, axis=0)
        kd = k * (1.0 + (a - 1.0) * ka)
        kd_sum = kd if kd_sum is None else kd_sum + kd
        beta = a * kk
        en = jnp.exp(-c)
        eh = jnp.exp(cend - c)
        o_at, o_bt, o_kt, o_rt, o_bh, o_kh, o_pe = outs[d]
        o_at[0] = (-kk * jnp.exp(c + e)).astype(BF16)
        o_bt[0] = (beta * en).astype(BF16)
        o_kt[0] = (kd * en).astype(BF16)
        o_rt[0] = (r * jnp.exp(c)).astype(BF16)
        o_bh[0] = (beta * eh).astype(BF16)
        o_kh[0] = (kd * eh).astype(BF16)
        for q in range(R // CHUNK):
            o_pe[0, q] = jnp.exp(cend[q * CHUNK:q * CHUNK + 8])
    bonus_ref[0] = (_dot((r * rk_ref[...] * kd_sum).astype(BF16), ones_bd) * v).astype(BF16)
    vt_ref[0] = v.T.astype(BF16)


def _rwprep(p_rw, consts, n_ctx_tiles):
    B, T, _ = p_rw.shape
    nt = T // ROW_TILE
    nh = ROW_TILE // 8
    row = lambda w: pl.BlockSpec((1, w), lambda b, t: (0, 0))
    full = lambda a: pl.BlockSpec(a.shape, lambda b, t: (0, 0))
    tm = pl.BlockSpec((1, ROW_TILE, D_HEADS), lambda b, t: (b, t, 0))
    tab = pl.BlockSpec((ROW_TILE, D_HEADS), lambda b, t: (t, 0))
    pe = pl.BlockSpec((1, ROW_TILE // CHUNK, 8, D_HEADS), lambda b, t: (b, t, 0, 0))
    tm_shape = jax.ShapeDtypeStruct((B, T, D_HEADS), BF16)
    pe_shape = jax.ShapeDtypeStruct((B, T // CHUNK, 8, D_HEADS), F32)
    dir_specs = [tm] * 6 + [pe]
    dir_shapes = [tm_shape] * 6 + [pe_shape]
    return pl.pallas_call(
        functools.partial(_rwprep_kernel, n_ctx_tiles=n_ctx_tiles, n_tiles=nt),
        grid=(B, nt),
        in_specs=[
            pl.BlockSpec((1, ROW_TILE, D_RWP), lambda b, t: (b, t, 0)),
            pl.BlockSpec((1, 8, D_RWP), lambda b, t: (b, jnp.maximum(t * nh - 1, 0), 0)),
            pl.BlockSpec((1, 8, D_RWP), lambda b, t: (b, jnp.minimum((t + 1) * nh, T // 8 - 1), 0)),
            row(D_RWP), row(D_RWP), tab, tab, tab,
            row(D_HEADS), row(D_HEADS), row(D_HEADS), row(2 * D_HEADS), row(2 * D_HEADS),
            full(consts["w2p"]), full(consts["a2p"]), full(consts["g2p"]), full(consts["ones_bd"]),
        ],
        out_specs=dir_specs + dir_specs + [
            pl.BlockSpec((1, D_HEADS, ROW_TILE), lambda b, t: (b, 0, t)),
            tm, tm,
        ],
        out_shape=dir_shapes + dir_shapes + [
            jax.ShapeDtypeStruct((B, D_HEADS, T), BF16),
            jax.ShapeDtypeStruct((B, T, D_HEADS), BF16),
            jax.ShapeDtypeStruct((B, T, D_HEADS), BF16),
        ],
        scratch_shapes=[pltpu.VMEM((ROW_TILE + 16, D_RWP), F32)],
        compiler_params=_cparams(("arbitrary", "arbitrary")),
    )(p_rw, p_rw, p_rw, consts["mu_prev"], consts["mu_next"], consts["cos"], consts["sa"], consts["sb"],
      consts["k_k"], consts["k_a"], consts["r_k"], consts["w0"], consts["a0"],
      consts["w2p"], consts["a2p"], consts["g2p"], consts["ones_bd"])


def _bd(y, m0):
    zero = jnp.zeros_like(y)
    return jnp.concatenate([jnp.where(m0, y, zero), jnp.where(m0, zero, y)], axis=0)


def _sel(w, m0):
    return jnp.where(m0, w[0:CHUNK], w[CHUNK:2 * CHUNK])


def _wkv_kernel(*refs):
    (at_f, bt_f, kt_f, rt_f, bh_f, kh_f, pe_f, vt_f,
     at_b, bt_b, kt_b, rt_b, bh_b, kh_b, pe_b, vt_b,
     yf_ref, yb_ref, s_ref) = refs
    s = pl.program_id(1)

    @pl.when(s == 0)
    def _():
        s_ref[...] = jnp.zeros_like(s_ref)

    lane = lax.broadcasted_iota(jnp.int32, (CHUNK, 2 * CHUNK), 1)
    rowi = lax.broadcasted_iota(jnp.int32, (CHUNK, 2 * CHUNK), 0)
    lm = jnp.bitwise_and(lane, CHUNK - 1)
    m0 = lane < CHUNK
    dirs = ((at_f, bt_f, kt_f, rt_f, bh_f, kh_f, pe_f, vt_f, yf_ref),
            (at_b, bt_b, kt_b, rt_b, bh_b, kh_b, pe_b, vt_b, yb_ref))
    masks = (((rowi < lm), (rowi <= lm)), ((rowi > lm), (rowi >= lm)))
    zero = jnp.zeros((CHUNK, 2 * CHUNK), F32)
    bd = lambda y: _bd(y, m0)
    bf = lambda y: y.astype(BF16)

    lane2 = lax.broadcasted_iota(jnp.int32, (2 * CHUNK, 2 * CHUNK), 1)
    cat2 = lambda y: jnp.concatenate([y, y], axis=0)
    probs = []
    for rnd in range(WKV_CPS):
        for d in range(2):
            for p in range(N_PAIRS):
                ck = rnd if d == 0 else WKV_CPS - 1 - rnd
                probs.append(dict(d=d, p=p, ck=ck, rnd=rnd,
                                  rs=slice(ck * CHUNK, (ck + 1) * CHUNK), ls=slice(p * 128, (p + 1) * 128)))

    def ld(pr, i):
        return dirs[pr["d"]][i][0, pr["rs"], pr["ls"]]

    def rhs1(pr):
        return jnp.concatenate([bd(ld(pr, 0)), bd(ld(pr, 3))], axis=0)

    for pr in probs:
        G = _dot_nt(jnp.concatenate([ld(pr, 1), ld(pr, 2)], axis=0), rhs1(pr))
        strict, incl = masks[pr["d"]]
        pr["N"] = jnp.where(strict, G[0:CHUNK, 0:128], zero)
        pr["N_br"] = bf(jnp.where(incl, G[0:CHUNK, 128:256], zero))
        pr["A_ak"] = bf(jnp.where(strict, G[CHUNK:128, 0:128], zero))
        pr["N_kr"] = bf(jnp.where(incl, G[CHUNK:128, 128:256], zero))
    for pr in probs:
        Ab = bf(pr["N"])
        pr["M"] = _dot(Ab, bd(Ab))
    for _ in range(4):
        for pr in probs:
            Mb = bf(pr["M"])
            Rm = _dot(jnp.concatenate([bf(pr["N"]), Mb], axis=0), bd(Mb))
            pr["N"] = pr["N"] + pr["M"] + Rm[0:CHUNK]
            pr["M"] = Rm[CHUNK:2 * CHUNK]
    for pr in probs:
        pr["N"] = bf(pr["N"] + pr["M"] + _dot(bf(pr["N"]), bd(bf(pr["M"]))))
        del pr["M"]
    for pr in probs:
        bh = ld(pr, 4)
        NZ = _dot(pr["N"], jnp.concatenate([bd(pr["N_br"]), bd(bh)], axis=1))
        z_br = bf(pr["N_br"].astype(F32) + NZ[:, 0:128])
        z_bh = bf(bh.astype(F32) + NZ[:, 128:256])
        pr["Z"] = jnp.concatenate([bd(z_br), bd(z_bh)], axis=1)
    for pr in probs:
        tile, half = pr["ck"] // 2, pr["ck"] % 2
        vt_p = dirs[pr["d"]][7][0, pr["ls"], tile * 128:(tile + 1) * 128]
        in_half = (lane2 < CHUNK) if half == 0 else (lane2 >= CHUNK)
        vtm = jnp.where(in_half, vt_p, jnp.zeros_like(vt_p))
        VG = _dot(vtm, jnp.concatenate([cat2(pr["A_ak"]), cat2(pr["N_kr"]), cat2(ld(pr, 5))], axis=1))
        pr["VA"] = _sel(VG[:, 0:128], m0)
        pr["VN"] = _sel(VG[:, 128:256], m0)
        pr["VK"] = _sel(VG[:, 256:384], m0)

    S = {(d, p): s_ref[d, p] for d in range(2) for p in range(N_PAIRS)}
    ys = {}
    for rnd in range(WKV_CPS):
        cur = [pr for pr in probs if pr["rnd"] == rnd]
        for pr in cur:
            St = S[(pr["d"], pr["p"])]
            SG = _dot_nt(bf(St), rhs1(pr))
            pr["X"] = bf(SG[:, 0:128] + pr["VA"])
            pr["Y"] = SG[:, 128:256] + pr["VN"]
        for pr in cur:
            UG = _dot(pr["X"], pr["Z"])
            key = (pr["d"], pr["p"])
            pend = dirs[pr["d"]][6][0, pr["ck"], 0:1, pr["ls"]]
            S[key] = S[key] * pend + UG[:, 128:256] + pr["VK"]
            ys[(pr["d"], pr["p"], pr["ck"])] = pr["Y"] + UG[:, 0:128]
    for d in range(2):
        y_ref = dirs[d][8]
        for p in range(N_PAIRS):
            s_ref[d, p] = S[(d, p)]
            for tile in range(WKV_CPS // 2):
                y0, y1 = ys[(d, p, 2 * tile)], ys[(d, p, 2 * tile + 1)]
                ts = slice(tile * 128, (tile + 1) * 128)
                y_ref[0, p * 128:p * 128 + CHUNK, ts] = jnp.where(m0, y0, pltpu.roll(y1, CHUNK, axis=1)).astype(BF16)
                y_ref[0, p * 128 + CHUNK:(p + 1) * 128, ts] = jnp.where(m0, pltpu.roll(y0, CHUNK, axis=1), y1).astype(BF16)


def _wkv(prep, n_ctx):
    (at_f, bt_f, kt_f, rt_f, bh_f, kh_f, pe_f, at_b, bt_b, kt_b, rt_b, bh_b, kh_b, pe_b, vt) = prep
    B, T, _ = at_f.shape
    blk = WKV_CPS * CHUNK
    assert T % blk == 0 and n_ctx % blk == 0
    ns = T // blk
    nc2 = n_ctx // blk

    def mrev(s):
        return jnp.where(s < nc2, nc2 - 1 - s, ns - 1 - (s - nc2))

    def specs(idx):
        tm = pl.BlockSpec((1, blk, D_HEADS), lambda b, s: (b, idx(s), 0))
        pe = pl.BlockSpec((1, WKV_CPS, 8, D_HEADS), lambda b, s: (b, idx(s), 0, 0))
        vts = pl.BlockSpec((1, D_HEADS, blk), lambda b, s: (b, 0, idx(s)))
        return [tm] * 6 + [pe, vts]

    fwd = lambda s: s
    yt = lambda idx: pl.BlockSpec((1, D_HEADS, blk), lambda b, s: (b, 0, idx(s)))
    return pl.pallas_call(
        _wkv_kernel,
        grid=(B, ns),
        in_specs=specs(fwd) + specs(mrev),
        out_specs=[yt(fwd), yt(mrev)],
        out_shape=[jax.ShapeDtypeStruct((B, D_HEADS, T), BF16)] * 2,
        scratch_shapes=[pltpu.VMEM((2, N_PAIRS, CHUNK, 2 * CHUNK), F32)],
        compiler_params=_cparams(("arbitrary", "arbitrary")),
    )(at_f, bt_f, kt_f, rt_f, bh_f, kh_f, pe_f, vt, at_b, bt_b, kt_b, rt_b, bh_b, kh_b, pe_b, vt)


def _rwpost_tile(yf, yb, bonus, g, gn_g, gn_b):
    y = yf.astype(F32) + yb.astype(F32)
    R = y.shape[1]
    y3 = y.reshape(N_HEADS, HEAD_DIM, R)
    mu = jnp.mean(y3, axis=1, keepdims=True)
    var = jnp.mean(jnp.square(y3 - mu), axis=1, keepdims=True)
    yn = ((y3 - mu) * lax.rsqrt(var + GN_EPS)).reshape(D_HEADS, R)
    out = (yn.T * gn_g + gn_b + bonus.astype(F32)) * g.astype(F32)
    return out.astype(BF16)


ATT_ROWS = 4
WKV_CPS = 4


def _attn_kernel(q_ref, k_ref, v_ref, *rest, n_ctx, n_rows, j0):
    bias_refs, o_ref = rest[:ATT_ROWS], rest[ATT_ROWS]
    j = pl.program_id(1) + j0
    n_cstep = n_ctx // (GRID_W * ATT_ROWS)
    lane = lax.broadcasted_iota(jnp.int32, (GRID_W, 128), 1)
    m0 = lane < HEAD_DIM
    win = WIN_ROWS * GRID_W
    rmax = lambda a: jnp.max(a, axis=-1, keepdims=True)
    rsum = lambda a: jnp.sum(a, axis=-1, keepdims=True)
    probs = [(u, p, slice(u * GRID_W, (u + 1) * GRID_W), slice(p * 128, (p + 1) * 128))
             for u in range(ATT_ROWS) for p in range(N_PAIRS)]

    def stacked_q(rs, ls):
        return _bd(q_ref[0, rs, ls], m0)

    @pl.when(j < n_cstep)
    def _():
        sc = [_dot_nt(stacked_q(rs, ls), k_ref[0, 0:n_ctx, ls]) for _, _, rs, ls in probs]
        mx = [rmax(a) for a in sc]
        ex = [jnp.exp(a - m) for a, m in zip(sc, mx)]
        den = [rsum(e) for e in ex]
        for n, (_, _, rs, ls) in enumerate(probs):
            o = _dot(ex[n].astype(BF16), v_ref[0, 0:n_ctx, ls]) / den[n]
            o_ref[0, rs, ls] = _sel(o, m0).astype(BF16)

    @pl.when(j >= n_cstep)
    def _():
        starts = []
        for u in range(ATT_ROWS):
            i = (j - n_cstep) * ATT_ROWS + u
            r0 = jnp.clip(i - WIN_ROWS // 2, 0, n_rows - WIN_ROWS)
            starts.append(pl.multiple_of(n_ctx + r0 * GRID_W, GRID_W))
        qs = [stacked_q(rs, ls) for _, _, rs, ls in probs]
        s_loc = [_dot_nt(qs[n], k_ref[0, pl.ds(starts[u], win), ls]) + bias_refs[u][0, p]
                 for n, (u, p, _, ls) in enumerate(probs)]
        s_ctx = [_dot_nt(qs[n], k_ref[0, 0:n_ctx, ls]) for n, (_, _, _, ls) in enumerate(probs)]
        mx = [jnp.maximum(rmax(a), rmax(b)) for a, b in zip(s_loc, s_ctx)]
        e_loc = [jnp.exp(a - m) for a, m in zip(s_loc, mx)]
        e_ctx = [jnp.exp(a - m) for a, m in zip(s_ctx, mx)]
        den = [rsum(a) + rsum(b) for a, b in zip(e_loc, e_ctx)]
        for n, (u, _, rs, ls) in enumerate(probs):
            o = _dot(e_loc[n].astype(BF16), v_ref[0, pl.ds(starts[u], win), ls])
            o = (o + _dot(e_ctx[n].astype(BF16), v_ref[0, 0:n_ctx, ls])) / den[n]
            o_ref[0, rs, ls] = _sel(o, m0).astype(BF16)


def _attention(qkv, bias_tab, n_ctx, latent_only):
    B, T, _ = qkv.shape
    n_rows = (T - n_ctx) // GRID_W
    n_cstep = n_ctx // (GRID_W * ATT_ROWS)
    half = WIN_ROWS // 2
    blk = GRID_W * ATT_ROWS
    j0 = n_cstep if latent_only else 0

    def delta(j, u):
        i = jnp.maximum(j + j0 - n_cstep, 0) * ATT_ROWS + u
        return jnp.minimum(i, half) + jnp.maximum(i - (n_rows - half), 0)

    bias_specs = [pl.BlockSpec((1, N_PAIRS, 128, WIN_ROWS * GRID_W), functools.partial(
        lambda b, j, u: (delta(j, u), 0, 0, 0), u=u)) for u in range(ATT_ROWS)]
    return pl.pallas_call(
        functools.partial(_attn_kernel, n_ctx=n_ctx, n_rows=n_rows, j0=j0),
        grid=(B, T // blk - j0),
        in_specs=[
            pl.BlockSpec((1, blk, D_HEADS), lambda b, j: (b, j + j0, 0)),
            pl.BlockSpec((1, T, D_HEADS), lambda b, j: (b, 0, 1)),
            pl.BlockSpec((1, T, D_HEADS), lambda b, j: (b, 0, 2)),
        ] + bias_specs,
        out_specs=pl.BlockSpec((1, blk, D_HEADS), lambda b, j: (b, j, 0)),
        out_shape=jax.ShapeDtypeStruct((B, T - j0 * blk, D_HEADS), BF16),
        compiler_params=_cparams(("arbitrary", "arbitrary")),
    )(qkv, qkv, qkv, *([bias_tab] * ATT_ROWS))


def _na_bias_table(rpb):
    H = rpb.shape[0]
    c = np.arange(GRID_W)[:, None]
    kc = np.arange(GRID_W)[None, :]
    cs = np.clip(c - WIN_COLS // 2, 0, GRID_W - WIN_COLS)
    valid = (kc >= cs) & (kc < cs + WIN_COLS)
    cidx = np.clip(kc - c + (WIN_COLS - 1), 0, 2 * WIN_COLS - 2)
    hot = jnp.asarray(cidx[None] == np.arange(2 * WIN_COLS - 1)[:, None, None], F32)
    t = jnp.einsum("hro,ock->hcrk", rpb.astype(F32), hot, precision=lax.Precision.HIGHEST)
    t = jnp.where(valid[None, :, None, :], t, NEG)
    tabs = [t[:, :, WIN_ROWS - 1 - dl:2 * WIN_ROWS - 1 - dl, :].reshape(H, GRID_W, WIN_ROWS * GRID_W)
            for dl in range(WIN_ROWS)]
    return jnp.stack(tabs, 0).reshape(WIN_ROWS, H // 2, 2 * GRID_W, WIN_ROWS * GRID_W)


def _layer_norm(h, g, b):
    mu = jnp.mean(h, axis=-1, keepdims=True)
    var = jnp.mean(jnp.square(h - mu), axis=-1, keepdims=True)
    return (h - mu) * lax.rsqrt(var + LN_EPS) * g + b


def _to_token_tiles(ref, val):
    n = val.shape[0]
    for j in range(TOK_SUB):
        ref[pl.ds(j, n, stride=TOK_SUB), :] = val[:, j * 128:(j + 1) * 128]


def _from_token_tiles(ref, n):
    return [ref[pl.ds(j, n, stride=TOK_SUB), :] for j in range(TOK_SUB)]


CLS_ROWS = 64
_EXPERT_ORDER = np.arange(N_EXPERTS).reshape(N_GROUPS, EXPERTS_PER_GROUP).T.reshape(-1)


def _route_tile(logits_t, bias_col, tri_t, carry):
    n = logits_t.shape[1]
    neg = jnp.float32(-jnp.inf)
    scores = jax.nn.sigmoid(logits_t)
    sel = scores + bias_col
    s = [sel[N_GROUPS * k:N_GROUPS * (k + 1)] for k in range(EXPERTS_PER_GROUP)]
    c = [scores[N_GROUPS * k:N_GROUPS * (k + 1)] for k in range(EXPERTS_PER_GROUP)]
    hi01, lo01 = jnp.maximum(s[0], s[1]), jnp.minimum(s[0], s[1])
    hi23, lo23 = jnp.maximum(s[2], s[3]), jnp.minimum(s[2], s[3])
    top2 = jnp.maximum(hi01, hi23) + jnp.maximum(jnp.minimum(hi01, hi23), jnp.maximum(lo01, lo23))
    grp = lax.broadcasted_iota(jnp.int32, (N_GROUPS, n), 0).astype(F32)
    gmax = jnp.max(top2, axis=0, keepdims=True)
    g_idx = jnp.min(jnp.where(top2 == gmax, grp, jnp.float32(N_GROUPS)), axis=0, keepdims=True)
    pick = grp == g_idx
    v = [jnp.sum(jnp.where(pick, a, 0.0), axis=0, keepdims=True) for a in s]
    w = [jnp.sum(jnp.where(pick, a, 0.0), axis=0, keepdims=True) for a in c]

    def first_argmax(vals):
        best, idx = vals[0], jnp.zeros_like(vals[0])
        for k in range(1, len(vals)):
            upd = vals[k] > best
            best = jnp.where(upd, vals[k], best)
            idx = jnp.where(upd, jnp.float32(k), idx)
        return idx

    i1 = first_argmax(v)
    i2 = first_argmax([jnp.where(i1 == k, neg, v[k]) for k in range(EXPERTS_PER_GROUP)])
    lo, hi = jnp.minimum(i1, i2), jnp.maximum(i1, i2)
    g_lo = sum(jnp.where(lo == k, w[k], 0.0) for k in range(EXPERTS_PER_GROUP))
    g_hi = sum(jnp.where(hi == k, w[k], 0.0) for k in range(EXPERTS_PER_GROUP))
    tot = g_lo + g_hi
    pair = lo * 3.0 - lo * (lo - 1.0) * 0.5 + (hi - lo - 1.0)
    cls = g_idx * 6.0 + pair
    ranks = []
    m = tri_t.shape[0]
    crow = lax.broadcasted_iota(jnp.int32, (CLS_ROWS, m), 0).astype(F32)
    for q in range(n // m):
        onehot = crow == cls[:, q * m:(q + 1) * m]
        within = _dot(onehot.astype(BF16), tri_t)
        ranks.append(jnp.sum(jnp.where(onehot, within + carry, 0.0), axis=0, keepdims=True))
        carry = carry + jnp.sum(onehot.astype(F32), axis=1, keepdims=True)
    zero = jnp.zeros((4, n), F32)
    return jnp.concatenate([g_lo / tot, g_hi / tot, cls, jnp.concatenate(ranks, axis=1), zero], axis=0), carry


def _route_kernel(lg_ref, rb_ref, tri_ref, route_ref, cnt_ref, carry_ref):
    @pl.when(pl.program_id(0) == 0)
    def _():
        carry_ref[...] = jnp.zeros_like(carry_ref)

    route, carry = _route_tile(lg_ref[...], rb_ref[...], tri_ref[...], carry_ref[:, 0:1])
    route_ref[...] = route
    carry_ref[...] = jnp.broadcast_to(carry, carry_ref.shape)
    cnt_ref[...] = jnp.broadcast_to(carry, cnt_ref.shape)


def _route(logits_t, router_bias):
    N = logits_t.shape[1]
    tile = next(t for t in (8 * ROW_TILE, 4 * ROW_TILE, 2 * ROW_TILE, ROW_TILE) if N % t == 0)
    r = np.arange(ROW_TILE)
    tri = jnp.asarray(r[:, None] < r[None, :], BF16)
    bias_col = router_bias.astype(F32)[_EXPERT_ORDER].reshape(N_EXPERTS, 1)
    full = lambda a: pl.BlockSpec(a.shape, lambda i: (0, 0))
    return pl.pallas_call(
        _route_kernel,
        grid=(N // tile,),
        in_specs=[pl.BlockSpec((N_EXPERTS, tile), lambda i: (0, i)), full(bias_col), full(tri)],
        out_specs=[pl.BlockSpec((8, tile), lambda i: (0, i)), pl.BlockSpec((CLS_ROWS, 128), lambda i: (0, 0))],
        out_shape=[jax.ShapeDtypeStruct((8, N), F32), jax.ShapeDtypeStruct((CLS_ROWS, 128), F32)],
        scratch_shapes=[pltpu.VMEM((CLS_ROWS, 128), F32)],
        compiler_params=_cparams(("arbitrary",)),
    )(logits_t, bias_col, tri)


def _outproj_kernel(na_ref, yf_ref, yb_ref, bonus_ref, rg_ref, gg_ref, gb_ref, w_ref, zc_ref, zx_ref, mod_ref,
                    g_ref, b_ref, rwh_ref, rwl_ref, z1_ref, hx_ref, lg_ref, *, d, alpha, n_ctx_tiles, t0):
    rw = _rwpost_tile(yf_ref[0], yb_ref[0], bonus_ref[0], rg_ref[0], gg_ref[...], gb_ref[...])
    o = _dot(na_ref[0], w_ref[0:D_HEADS, :]) + _dot(rw, w_ref[D_HEADS:2 * D_HEADS, :])
    mod = mod_ref[0, 0]
    gate = mod[:, 2 * d:3 * d]
    z = _z_tile(zc_ref, zx_ref, n_ctx_tiles, t0)
    z1 = _layer_norm(alpha * z + gate * o, g_ref[...], b_ref[...])
    z1_ref[0] = z1
    hx = z1 * (1.0 + mod[:, 4 * d:5 * d]) + mod[:, 3 * d:4 * d]
    _to_token_tiles(hx_ref, hx)
    hx_hi, hx_lo = _split2(hx)
    wh, wl = rwh_ref[...], rwl_ref[...]
    logits = _dot(hx_hi, wh) + _dot(hx_hi, wl) + _dot(hx_lo, wh)
    lg_ref[...] = logits.T[0:N_EXPERTS]


def _outproj(na, rwkv, gn_g, gn_b, w_out_b, zc, zx, sub, modsel, ln_g, ln_b, router_w, n_ctx_tiles, t0, alpha):
    yf, yb, bonus, rgate = rwkv
    B, _, T = yf.shape
    D = zc.shape[2]
    nt = T // ROW_TILE - t0
    na_off = (T - na.shape[1]) // ROW_TILE
    half = pl.BlockSpec((1, ROW_TILE, D_HEADS), lambda b, t: (b, t + t0, 0))
    na_spec = pl.BlockSpec((1, ROW_TILE, D_HEADS), lambda b, t: (b, t + t0 - na_off, 0))
    ytile = pl.BlockSpec((1, D_HEADS, ROW_TILE), lambda b, t: (b, 0, t + t0))
    hrow = pl.BlockSpec((1, D_HEADS), lambda b, t: (0, 0))
    tile = pl.BlockSpec((1, ROW_TILE, D), lambda b, t: (b, t, 0))
    row = pl.BlockSpec((1, D), lambda b, t: (0, 0))
    cspec, xspec = _z_specs(D, n_ctx_tiles, sub, t0)
    wt = jnp.pad(router_w[:, _EXPERT_ORDER], ((0, 0), (0, 128 - N_EXPERTS)))
    wt_hi = wt.astype(BF16)
    wt_lo = (wt - wt_hi.astype(F32)).astype(BF16)
    full = lambda a: pl.BlockSpec(a.shape, lambda b, t: (0, 0))
    return pl.pallas_call(
        functools.partial(_outproj_kernel, d=D, alpha=alpha, n_ctx_tiles=n_ctx_tiles, t0=t0),
        grid=(B, nt),
        in_specs=[
            na_spec, ytile, ytile, half, half, hrow, hrow,
            pl.BlockSpec((2 * D_HEADS, D), lambda b, t: (0, 0)),
            cspec, xspec,
            _mod_spec(D, n_ctx_tiles, t0),
            row, row,
            full(wt_hi), full(wt_lo),
        ],
        out_specs=[
            tile,
            pl.BlockSpec((ROW_TILE * TOK_SUB, 128), lambda b, t: (b * nt + t, 0)),
            pl.BlockSpec((N_EXPERTS, ROW_TILE), lambda b, t: (0, b * nt + t)),
        ],
        out_shape=[
            jax.ShapeDtypeStruct((B, nt * ROW_TILE, D), F32),
            jax.ShapeDtypeStruct((B * nt * ROW_TILE * TOK_SUB, 128), F32),
            jax.ShapeDtypeStruct((N_EXPERTS, B * nt * ROW_TILE), F32),
        ],
        compiler_params=_cparams(("arbitrary", "arbitrary")),
    )(na, yf, yb, bonus, rgate, gn_g, gn_b, w_out_b, zc, zx, modsel, ln_g, ln_b, wt_hi, wt_lo)


def _tok(ref, i):
    return ref.at[pl.ds(pl.multiple_of(i * TOK_SUB, TOK_SUB), TOK_SUB)]


def _dispatch_kernel(dest_ref, hx_ref, xs_in, xs_out, sem):
    del xs_in
    n = dest_ref.shape[0]

    def start(i, c):
        for par in range(2):
            r = 2 * i + par
            pltpu.make_async_copy(_tok(hx_ref, r), _tok(xs_out, dest_ref[r]), sem).start(priority=par)
        return c

    lax.fori_loop(0, n // 2, start, 0, unroll=4)

    def wait(r, c):
        pltpu.make_async_copy(_tok(hx_ref, 0), _tok(xs_out, 0), sem).wait()
        return c

    lax.fori_loop(0, n, wait, 0, unroll=8)


def _dispatch(dest, hx_tiles, n_rows_pad):
    N = dest.shape[0]
    tile = next(t for t in (8 * ROW_TILE, 4 * ROW_TILE, 2 * ROW_TILE, ROW_TILE) if N % t == 0)
    xs0 = jnp.zeros((n_rows_pad * TOK_SUB, 128), F32)
    return pl.pallas_call(
        _dispatch_kernel,
        grid=(N // tile,),
        in_specs=[
            pl.BlockSpec((tile,), lambda i: (i,), memory_space=pltpu.SMEM),
            pl.BlockSpec((tile * TOK_SUB, 128), lambda i: (i, 0)),
            pl.BlockSpec(memory_space=pl.ANY),
        ],
        out_specs=pl.BlockSpec(memory_space=pl.ANY),
        out_shape=jax.ShapeDtypeStruct((n_rows_pad * TOK_SUB, 128), F32),
        scratch_shapes=[pltpu.SemaphoreType.DMA(())],
        input_output_aliases={2: 0},
        compiler_params=_cparams(("arbitrary",)),
    )(dest, hx_tiles, xs0)


def _pack_bf16_pair(a, b):
    ha = lax.bitcast_convert_type(a.astype(BF16).astype(F32), jnp.uint32)
    hb = lax.bitcast_convert_type(b.astype(BF16).astype(F32), jnp.uint32)
    return jnp.bitwise_or(ha, jnp.right_shift(hb, jnp.uint32(16)))


def _unpack_bf16_pair(w):
    a = lax.bitcast_convert_type(jnp.bitwise_and(w, jnp.uint32(0xFFFF0000)), F32)
    b = lax.bitcast_convert_type(jnp.left_shift(w, jnp.uint32(16)), F32)
    return a, b


def _expert_kernel(sc_ref, xs_ref, w1_hbm, w3_hbm, w2_hbm, ys_ref, s13, s2, c13, c2, sem, *, layer):
    i = pl.program_id(0)
    slot = sc_ref[2, i]

    def copies(sl, ea, eb):
        out = []
        for j, e in enumerate((ea, eb)):
            out.append(pltpu.make_async_copy(w1_hbm.at[layer, e], s13.at[sl, 2 * j], sem.at[sl, 3 * j]))
            out.append(pltpu.make_async_copy(w3_hbm.at[layer, e], s13.at[sl, 2 * j + 1], sem.at[sl, 3 * j + 1]))
            out.append(pltpu.make_async_copy(w2_hbm.at[layer, e], s2.at[sl, j], sem.at[sl, 3 * j + 2]))
        return out

    @pl.when(i == 0)
    def _():
        for cp in copies(0, sc_ref[3, i], sc_ref[4, i]):
            cp.start()

    for sl in range(2):
        @pl.when(jnp.logical_and(sc_ref[1, i] != 0, slot == sl))
        def _():
            for cp in copies(sl, sc_ref[3, i], sc_ref[4, i]):
                cp.wait()
            def cast(r, c):
                rows = pl.ds(pl.multiple_of(r * 64, 64), 64)
                for j in range(4):
                    c13[j, rows, :] = s13[sl, j, rows, :].astype(BF16)
                return c

            lax.fori_loop(0, s13.shape[2] // 64, cast, 0)

            def cast2(r, c):
                rows = pl.ds(pl.multiple_of(r * 64, 64), 64)
                for j in range(2):
                    c2[j, rows, :] = s2[sl, j, rows, :].astype(BF16)
                return c

            lax.fori_loop(0, s2.shape[2] // 64, cast2, 0)

            @pl.when(sc_ref[5, i] != 0)
            def _():
                for cp in copies(1 - sl, sc_ref[6, i], sc_ref[7, i]):
                    cp.start()

    @pl.when(sc_ref[0, i] != 0)
    def _():
        x = jnp.concatenate(_from_token_tiles(xs_ref, MOE_TILE), axis=1).astype(BF16)

        def ffn(j):
            h1 = _dot(x, c13[2 * j])
            h3 = _dot(x, c13[2 * j + 1])
            h = (h1 * jax.nn.sigmoid(h1)) * h3
            return _dot(h.astype(BF16), c2[j])

        _to_token_tiles(ys_ref, _pack_bf16_pair(ffn(0), ffn(1)))

    @pl.when(sc_ref[0, i] == 0)
    def _():
        ys_ref[...] = jnp.zeros_like(ys_ref)


def _experts(blk_sched, xs, w1, w3, w2, l):
    nb = xs.shape[0] // (MOE_TILE * TOK_SUB)
    _, _, D, DE = w1.shape
    tok = pl.BlockSpec((MOE_TILE * TOK_SUB, 128), lambda i, sc: (i, 0))
    hbm = pl.BlockSpec(memory_space=pl.ANY)
    grid_spec = pltpu.PrefetchScalarGridSpec(
        num_scalar_prefetch=1,
        grid=(nb,),
        in_specs=[tok, hbm, hbm, hbm],
        out_specs=tok,
        scratch_shapes=[
            pltpu.VMEM((2, 4, D, DE), F32),
            pltpu.VMEM((2, 2, DE, D), F32),
            pltpu.VMEM((4, D, DE), BF16),
            pltpu.VMEM((2, DE, D), BF16),
            pltpu.SemaphoreType.DMA((2, 6)),
        ],
    )
    return pl.pallas_call(
        functools.partial(_expert_kernel, layer=l),
        grid_spec=grid_spec,
        out_shape=jax.ShapeDtypeStruct(xs.shape, jnp.uint32),
        compiler_params=_cparams(("arbitrary",)),
    )(blk_sched, xs, w1, w3, w2)


def _combine_kernel(dest_ref, dnext_ref, ys_ref, gate_ref, z1_ref, mod_ref, g_ref, b_ref, o_ref, buf, sem,
                    *, d, alpha):
    n = dest_ref.shape[0]
    step = pl.program_id(0) * pl.num_programs(1) + pl.program_id(1)
    n_steps = pl.num_programs(0) * pl.num_programs(1)
    slot = step % 2

    def gather(idx_ref, sl):
        def start(i, c):
            for par in range(2):
                r = 2 * i + par
                pltpu.make_async_copy(_tok(ys_ref, idx_ref[r]), _tok(buf.at[sl], r), sem.at[sl]).start(priority=par)
            return c

        lax.fori_loop(0, n // 2, start, 0, unroll=4)

    @pl.when(step == 0)
    def _():
        gather(dest_ref, 0)

    @pl.when(step + 1 < n_steps)
    def _():
        gather(dnext_ref, 1 - slot)

    def wait(r, c):
        pltpu.make_async_copy(_tok(ys_ref, 0), _tok(buf.at[slot], 0), sem.at[slot]).wait()
        return c

    lax.fori_loop(0, n, wait, 0, unroll=8)
    gates = gate_ref[...].T
    ga = gates[:, 0:1]
    gb = gates[:, 1:2]
    parts = []
    for w in _from_token_tiles(buf.at[slot], n):
        fa, fb = _unpack_bf16_pair(w)
        parts.append(ga * fa + gb * fb)
    y = jnp.concatenate(parts, axis=1)
    gate = mod_ref[0, 0][:, 5 * d:6 * d]
    o_ref[0] = _layer_norm(alpha * z1_ref[0] + gate * y, g_ref[...], b_ref[...])


def _combine(dest, ys, gates, z1, modsel, ln_g, ln_b, n_ctx_tiles, t0, t_out, alpha):
    B, T1, D = z1.shape
    nt1 = T1 // ROW_TILE
    skip = t_out - t0
    nt = nt1 - skip
    tile = lambda off: pl.BlockSpec((1, ROW_TILE, D), lambda b, t: (b, t + off, 0))
    row = pl.BlockSpec((1, D), lambda b, t: (0, 0))

    def nxt(b, t):
        last = jnp.logical_and(b == B - 1, t == nt - 1)
        wrap = t == nt - 1
        b2 = jnp.where(jnp.logical_and(wrap, jnp.logical_not(last)), b + 1, b)
        t2 = jnp.where(last, t, jnp.where(wrap, 0, t + 1))
        return b2 * nt1 + t2 + skip

    return pl.pallas_call(
        functools.partial(_combine_kernel, d=D, alpha=alpha),
        grid=(B, nt),
        in_specs=[
            pl.BlockSpec((ROW_TILE,), lambda b, t: (b * nt1 + t + skip,), memory_space=pltpu.SMEM),
            pl.BlockSpec((ROW_TILE,), lambda b, t: (nxt(b, t),), memory_space=pltpu.SMEM),
            pl.BlockSpec(memory_space=pl.ANY),
            pl.BlockSpec((8, ROW_TILE), lambda b, t: (0, b * nt1 + t + skip)),
            tile(skip),
            _mod_spec(D, n_ctx_tiles, t_out),
            row, row,
        ],
        out_specs=tile(0),
        out_shape=jax.ShapeDtypeStruct((B, nt * ROW_TILE, D), F32),
        scratch_shapes=[pltpu.VMEM((2, ROW_TILE * TOK_SUB, 128), jnp.uint32), pltpu.SemaphoreType.DMA((2,))],
        compiler_params=_cparams(("arbitrary", "arbitrary")),
    )(dest, dest, ys, gates, z1, modsel, ln_g, ln_b)


_PAIR_LO = np.array([0, 0, 0, 1, 1, 2], np.int32)
_PAIR_HI = np.array([1, 2, 3, 2, 3, 3], np.int32)


def _schedule(route, cnt):
    cls = route[2].astype(jnp.int32)
    rank = route[3].astype(jnp.int32)
    N = cls.shape[0]
    counts = cnt[:N_CLASSES, 0].astype(jnp.int32)
    onehot = (cls[:, None] == jnp.arange(N_CLASSES, dtype=jnp.int32)[None, :])
    padded = (counts + MOE_TILE - 1) // MOE_TILE * MOE_TILE
    cls_end = jnp.cumsum(padded)
    cls_start = cls_end - padded
    dest = jnp.sum(jnp.where(onehot, cls_start[None, :], 0), axis=1) + rank

    nb = N // MOE_TILE + N_CLASSES
    blk_row = jnp.arange(nb, dtype=jnp.int32) * MOE_TILE
    total = jnp.sum(padded)
    blk_valid = (blk_row < total).astype(jnp.int32)
    row_c = jnp.minimum(blk_row, jnp.maximum(total - MOE_TILE, 0))
    blk_cls = jnp.sum((row_c[:, None] >= cls_end[None, :]).astype(jnp.int32), axis=1)
    blk_cls = jnp.minimum(blk_cls, N_CLASSES - 1)

    def experts_of(c):
        hot = (c % 6)[:, None] == jnp.arange(6, dtype=jnp.int32)[None, :]
        base = (c // 6) * EXPERTS_PER_GROUP
        return (base + jnp.sum(jnp.where(hot, jnp.asarray(_PAIR_LO)[None, :], 0), axis=1),
                base + jnp.sum(jnp.where(hot, jnp.asarray(_PAIR_HI)[None, :], 0), axis=1))

    prev_cls = jnp.concatenate([jnp.full((1,), -1, jnp.int32), blk_cls[:-1]])
    first = jnp.logical_and(blk_valid != 0, blk_cls != prev_cls).astype(jnp.int32)
    parity = (jnp.cumsum(first) - 1) % 2
    cls_ids = jnp.arange(N_CLASSES, dtype=jnp.int32)
    used = jnp.where(counts > 0, cls_ids, N_CLASSES)
    later = jnp.where(cls_ids[None, :] > cls_ids[:, None], used[None, :], N_CLASSES)
    nxt_of_cls = jnp.min(later, axis=1)
    cls_hot = blk_cls[:, None] == cls_ids[None, :]
    nxt_cls = jnp.sum(jnp.where(cls_hot, nxt_of_cls[None, :], 0), axis=1)
    has_next = (nxt_cls < N_CLASSES).astype(jnp.int32)
    ea, eb = experts_of(blk_cls)
    na, nb_ = experts_of(jnp.minimum(nxt_cls, N_CLASSES - 1))
    blk_sched = jnp.stack([blk_valid, first, parity, ea, eb, has_next, na, nb_]).astype(jnp.int32)
    return dest.astype(jnp.int32), blk_sched, nb * MOE_TILE


def _rope_tables(n_ctx, seq):
    t = np.arange(seq)
    row = (t // GRID_W).astype(np.float32)
    col = (t % GRID_W).astype(np.float32)
    n_freq = HEAD_DIM // 4
    inv = jnp.asarray(ROPE_BASE, F32) ** (-jnp.arange(n_freq, dtype=F32) / n_freq)
    ar = jnp.asarray(row)[:, None] * inv
    ac = jnp.asarray(col)[:, None] * inv
    ang = jnp.concatenate([ar, ar, ac, ac], -1)
    cos = jnp.cos(ang)
    sin = jnp.sin(ang)
    quarter = (np.arange(HEAD_DIM) // n_freq) % 2
    sa = jnp.where(quarter == 0, -sin, 0.0)
    sb = jnp.where(quarter == 1, sin, 0.0)
    pad = lambda a, fill: jnp.concatenate([jnp.full((n_ctx, HEAD_DIM), fill, F32), a], 0)
    tile = lambda a: jnp.tile(a, (1, N_HEADS))
    return tile(pad(cos, 1.0)), tile(pad(sa, 0.0)), tile(pad(sb, 0.0))


def _rw_consts(l, n_ctx, seq, rw_mu_prev, rw_mu_next, rw_w0, rw_w2, rw_a0, rw_a2, rw_g2, rw_k_k, rw_k_a, rw_r_k):
    d_rw_in = rw_mu_prev.shape[1]
    padw = lambda a: jnp.pad(a[l], (0, D_RWP - d_rw_in)).reshape(1, D_RWP)
    cos, sa, sb = _rope_tables(n_ctx, seq)
    w2p = jnp.zeros((128, 2 * D_HEADS), F32)
    w2p = w2p.at[0:LORA, 0:D_HEADS].set(rw_w2[l, 0]).at[LORA:2 * LORA, D_HEADS:].set(rw_w2[l, 1])
    a2p = jnp.zeros((128, 2 * D_HEADS), F32)
    a2p = a2p.at[2 * LORA:3 * LORA, 0:D_HEADS].set(rw_a2[l, 0]).at[3 * LORA:4 * LORA, D_HEADS:].set(rw_a2[l, 1])
    g2p = jnp.zeros((128, D_HEADS), F32).at[0:GATE_LORA].set(rw_g2[l])
    head = np.arange(D_HEADS) // HEAD_DIM
    ones_bd = jnp.asarray(head[:, None] == head[None, :], BF16)
    return dict(
        mu_prev=padw(rw_mu_prev), mu_next=padw(rw_mu_next), cos=cos, sa=sa, sb=sb,
        k_k=rw_k_k[l].reshape(1, D_HEADS), k_a=rw_k_a[l].reshape(1, D_HEADS), r_k=rw_r_k[l].reshape(1, D_HEADS),
        w0=rw_w0[l].reshape(1, 2 * D_HEADS), a0=rw_a0[l].reshape(1, 2 * D_HEADS),
        w2p=w2p.astype(BF16), a2p=a2p.astype(BF16), g2p=g2p.astype(BF16), ones_bd=ones_bd,
    )


def kernel(x, c, ctx, c_ctx, ada_w, ada_b, w_in, na_rpb, rw_mu_prev, rw_mu_next, rw_w0, rw_w2, rw_a0, rw_a2, rw_g2, rw_k_k, rw_k_a, rw_r_k, rw_gn_g, rw_gn_b, w_out, ln1_g, ln1_b, ln2_g, ln2_b, router_w, router_bias, exp_w1, exp_w3, exp_w2):
    B, S, D = x.shape
    C = ctx.shape[1]
    L = ada_w.shape[0]
    T = C + S
    assert D == 1024 and C % ROW_TILE == 0 and S % ROW_TILE == 0 and C % (2 * CHUNK) == 0
    assert S % GRID_W == 0 and S // GRID_W >= WIN_ROWS and w_in.shape[2] == D_INP - 32
    n_ctx_tiles = C // ROW_TILE
    alpha = float((2 * L) ** 0.25)

    zc, zx, sub = ctx, x, n_ctx_tiles
    n_mod = (B + 1 + 7) // 8 * 8
    cc = jnp.zeros((n_mod, D), F32).at[0:B].set(c).at[B].set(c_ctx)
    mod_all = _ada(cc, ada_w, ada_b)

    for l in range(L):
        mod_c = jnp.broadcast_to(mod_all[l, B][None], (B, 6 * D))
        modsel = jnp.stack([mod_c, mod_all[l, 0:B]], axis=1).reshape(B, 2, 1, 6 * D)
        w_in_p = jnp.pad(w_in[l], ((0, 0), (0, D_INP - w_in.shape[2]))).astype(BF16)
        qkv, p_rw = _inproj(zc, zx, sub, T, modsel, w_in_p, n_ctx_tiles)

        consts = _rw_consts(l, C, S, rw_mu_prev, rw_mu_next, rw_w0, rw_w2, rw_a0, rw_a2, rw_g2,
                            rw_k_k, rw_k_a, rw_r_k)
        prep = _rwprep(p_rw, consts, n_ctx_tiles)
        yf, yb = _wkv(prep[0:15], C)
        last = l == L - 1
        na = _attention(qkv, _na_bias_table(na_rpb[l]), C, latent_only=last)
        t0 = n_ctx_tiles if last else 0
        z1, hx_tiles, logits_t = _outproj(na, (yf, yb, prep[15], prep[16]), rw_gn_g[l].reshape(1, D_HEADS),
                                          rw_gn_b[l].reshape(1, D_HEADS), w_out[l].astype(BF16), zc, zx, sub,
                                          modsel, ln1_g[l].reshape(1, D), ln1_b[l].reshape(1, D), router_w,
                                          n_ctx_tiles, t0, alpha)
        route, cnt = _route(logits_t, router_bias)
        dest, blk_sched, n_rows_pad = _schedule(route, cnt)
        xs = _dispatch(dest, hx_tiles, n_rows_pad)
        ys = _experts(blk_sched, xs, exp_w1, exp_w3, exp_w2, l)
        z = _combine(dest, ys, route, z1, modsel, ln2_g[l].reshape(1, D), ln2_b[l].reshape(1, D),
                     n_ctx_tiles, t0, t0, alpha)
        zc, zx, sub = z, z, 0

    return z
```
